```python
import math
import jax, jax.numpy as jnp
from jax import lax
import numpy as np

D_MODEL = 1024
BATCH = 16
SEQ = 4096
DEPTH = 4

CHUNK = 64
Q_BLOCK = 128
RMS_EPS = 1e-6
ROPE_THETA = 500000.0
MLA_ROPE_THETA = 10000.0
MACARON_WEIGHT = 0.5
N_MOD = 9
D_FF = 2816

FOX_HEADS = 8
FOX_HD = 64
FOX_GATE_BIAS = 3.0
DSA_HEADS = 8
DSA_HD = 64
DSA_ROT = DSA_HD // 4
IDX_HEADS = 4
IDX_HD = 64
IDX_ROT = IDX_HD // 4
DSA_TOPK_MAX = 256
MLA_HEADS = 8
MLA_NOPE = 128
MLA_ROPE = 64
MLA_V = 128
MLA_Q_LORA = 384
MLA_KV_LORA = 256
MLA_DOWN = MLA_Q_LORA + MLA_KV_LORA + MLA_ROPE

HYB_SIZES = (FOX_HEADS * FOX_HD, FOX_HEADS * FOX_HD, FOX_HEADS * FOX_HD, FOX_HEADS,
             DSA_HEADS * DSA_HD, DSA_HEADS * DSA_HD, DSA_HEADS * DSA_HD,
             IDX_HEADS * IDX_HD, IDX_HEADS, IDX_HD)
HYB_IN = sum(HYB_SIZES)
HYB_OUT = FOX_HEADS * FOX_HD + DSA_HEADS * DSA_HD
N_EVEN = (DEPTH + 1) // 2
N_ODD = DEPTH // 2

kernel_name = 'chunk_causal_hybrid_fox_dsa_mla_trunk'


def _rmsnorm(x, g):
    x32 = x.astype(jnp.float32)
    y = x32 * lax.rsqrt(jnp.mean(x32 * x32, axis=-1, keepdims=True) + RMS_EPS)
    return (y * g.astype(jnp.float32)).astype(x.dtype)


def _rope(x, pos, rot_dim, theta):
    half = rot_dim // 2
    inv_freq = jnp.exp(-math.log(theta) * 2.0 * jnp.arange(half, dtype=jnp.float32) / rot_dim)
    ang = pos.astype(jnp.float32)[:, :, None, None] * inv_freq
    cos, sin = jnp.cos(ang), jnp.sin(ang)
    xr = x[..., :rot_dim].astype(jnp.float32)
    x1, x2 = xr[..., :half], xr[..., half:]
    rot = jnp.concatenate([x1 * cos - x2 * sin, x2 * cos + x1 * sin], axis=-1).astype(x.dtype)
    return jnp.concatenate([rot, x[..., rot_dim:]], axis=-1)


def _sweep_query_blocks(fn, *q_args):
    b, s = q_args[0].shape[:2]
    nblk = s // Q_BLOCK
    blocked = tuple(jnp.moveaxis(a.reshape(b, nblk, Q_BLOCK, *a.shape[2:]), 1, 0) for a in q_args)
    out = lax.map(lambda xs: fn(xs[0] * Q_BLOCK + jnp.arange(Q_BLOCK, dtype=jnp.int32), *xs[1:]),
                  (jnp.arange(nblk, dtype=jnp.int32),) + blocked)
    out = jnp.moveaxis(out, 0, 1)
    return out.reshape(b, s, *out.shape[3:])


def _fox_attention(q, k, v, log_f):
    s = k.shape[1]
    cum = jnp.moveaxis(jnp.cumsum(log_f, axis=1), 1, 2)
    s_idx = jnp.arange(s, dtype=jnp.int32)
    scale = q.shape[-1] ** -0.5

    def block(t_idx, qb, cum_q):
        logits = jnp.einsum('bqhd,bshd->bhqs', qb, k, preferred_element_type=jnp.float32) * scale
        logits = logits + (jnp.moveaxis(cum_q, 1, 2)[..., :, None] - cum[..., None, :])
        mask = s_idx[None, :] <= t_idx[:, None]
        logits = jnp.where(mask, logits, -jnp.inf)
        p = jax.nn.softmax(logits, axis=-1).astype(v.dtype)
        return jnp.einsum('bhqs,bshd->bqhd', p, v)

    return _sweep_query_blocks(block, q, jnp.moveaxis(cum, 1, 2))


def _dsa_attention(q, k, v, iq, iw, ik):
    s = k.shape[1]
    top_k = min(DSA_TOPK_MAX, s // 4)
    s_chunk = jnp.arange(s, dtype=jnp.int32) // CHUNK
    scale = q.shape[-1] ** -0.5
    idx_scale = iq.shape[-1] ** -0.5
    gather = jax.vmap(lambda table, ix: table[ix])

    def block(t_idx, qb, iqb, iwb):
        dots = jnp.einsum('bqhd,bsd->bqhs', iqb, ik, preferred_element_type=jnp.float32) * idx_scale
        score = jnp.einsum('bqhs,bqh->bqs', jax.nn.relu(dots), iwb.astype(jnp.float32))
        admissible = s_chunk[None, :] <= (t_idx // CHUNK)[:, None]
        score = jnp.where(admissible[None], score, -jnp.inf)
        sel_score, sel = lax.top_k(score, top_k)
        kg = gather(k, sel)
        vg = gather(v, sel)
        logits = jnp.einsum('bqhd,bqkhd->bhqk', qb, kg, preferred_element_type=jnp.float32) * scale
        valid = jnp.isfinite(sel_score)[:, None]
        logits = jnp.where(valid, logits, -jnp.inf)
        p = jax.nn.softmax(logits, axis=-1).astype(v.dtype)
        return jnp.einsum('bhqk,bqkhd->bqhd', p, vg)

    return _sweep_query_blocks(block, q, iq, iw)


def _mla_attention(q_nope, q_rope, k_nope, k_rope, v):
    s = k_nope.shape[1]
    s_chunk = jnp.arange(s, dtype=jnp.int32) // CHUNK
    scale = (q_nope.shape[-1] + q_rope.shape[-1]) ** -0.5

    def block(t_idx, qn, qr):
        logits = (jnp.einsum('bqhd,bshd->bhqs', qn, k_nope, preferred_element_type=jnp.float32)
                  + jnp.einsum('bqhr,bsr->bhqs', qr, k_rope, preferred_element_type=jnp.float32)) * scale
        mask = s_chunk[None, :] <= (t_idx // CHUNK)[:, None]
        logits = jnp.where(mask, logits, -jnp.inf)
        p = jax.nn.softmax(logits, axis=-1).astype(v.dtype)
        return jnp.einsum('bhqs,bshd->bqhd', p, v)

    return _sweep_query_blocks(block, q_nope, q_rope)


def _hybrid_mixer(u, positions, w_in, b_f, w_out):
    b, s, _ = u.shape
    offsets = np.cumsum(HYB_SIZES)[:-1].tolist()
    fq, fk, fv, ff, dq, dk, dv, iq, iw, ik = jnp.split(u @ w_in, offsets, axis=-1)
    log_f = jax.nn.log_sigmoid((ff + b_f).astype(jnp.float32))
    out_a = _fox_attention(fq.reshape(b, s, FOX_HEADS, FOX_HD), fk.reshape(b, s, FOX_HEADS, FOX_HD),
                           fv.reshape(b, s, FOX_HEADS, FOX_HD), log_f)
    dq = _rope(dq.reshape(b, s, DSA_HEADS, DSA_HD), positions, DSA_ROT, ROPE_THETA)
    dk = _rope(dk.reshape(b, s, DSA_HEADS, DSA_HD), positions, DSA_ROT, ROPE_THETA)
    iq = _rope(iq.reshape(b, s, IDX_HEADS, IDX_HD), positions, IDX_ROT, ROPE_THETA)
    ik = _rope(ik[:, :, None, :], positions, IDX_ROT, ROPE_THETA)[:, :, 0, :]
    iw = iw * IDX_HEADS ** -0.5
    out_b = _dsa_attention(dq, dk, dv.reshape(b, s, DSA_HEADS, DSA_HD), iq, iw, ik)
    mixed = jnp.concatenate([out_a.reshape(b, s, -1), out_b.reshape(b, s, -1)], axis=-1)
    return mixed @ w_out


def _mla_mixer(u, positions, w_down, q_norm, kv_norm, w_uq, w_ukv, w_out):
    b, s, _ = u.shape
    cq, ckv, k_rope = jnp.split(u @ w_down, [MLA_Q_LORA, MLA_Q_LORA + MLA_KV_LORA], axis=-1)
    q = (_rmsnorm(cq, q_norm) @ w_uq).reshape(b, s, MLA_HEADS, MLA_NOPE + MLA_ROPE)
    kv = (_rmsnorm(ckv, kv_norm) @ w_ukv).reshape(b, s, MLA_HEADS, MLA_NOPE + MLA_V)
    q_nope = q[..., :MLA_NOPE]
    q_rope = _rope(q[..., MLA_NOPE:], positions, MLA_ROPE, MLA_ROPE_THETA)
    k_nope, v = kv[..., :MLA_NOPE], kv[..., MLA_NOPE:]
    k_rope = _rope(k_rope[:, :, None, :], positions, MLA_ROPE, MLA_ROPE_THETA)[:, :, 0, :]
    out = _mla_attention(q_nope, q_rope, k_nope, k_rope, v)
    return out.reshape(b, s, -1) @ w_out


def _swiglu(u, w_gate, w_up, w_down):
    return (jax.nn.silu(u @ w_gate) * (u @ w_up)) @ w_down


def setup_inputs(seed: int = 0) -> dict:
    key = jax.random.key(seed)
    ks = jax.random.split(key, 20)
    f32 = jnp.float32

    def nrm(k, shape, fan_in):
        return jax.random.normal(k, shape, f32) * fan_in ** -0.5

    x = jax.random.normal(ks[0], (BATCH, SEQ, D_MODEL), f32)
    c = jax.random.normal(ks[1], (BATCH, D_MODEL), f32)
    start = jax.random.randint(ks[2], (BATCH, 1), 0, 4096, dtype=jnp.int32)
    positions = start + jnp.arange(SEQ, dtype=jnp.int32)[None, :]
    ada_w = 0.5 * nrm(ks[3], (DEPTH, D_MODEL, N_MOD * D_MODEL), D_MODEL)
    ada_b = 0.02 * jax.random.normal(ks[4], (DEPTH, N_MOD * D_MODEL), f32)
    norm_g = 1.0 + 0.02 * jax.random.normal(ks[5], (DEPTH, 3, D_MODEL), f32)
    ffn_w_gate = nrm(ks[6], (DEPTH, 2, D_MODEL, D_FF), D_MODEL)
    ffn_w_up = nrm(ks[7], (DEPTH, 2, D_MODEL, D_FF), D_MODEL)
    ffn_w_down = nrm(ks[8], (DEPTH, 2, D_FF, D_MODEL), D_FF)
    hyb_w_in = nrm(ks[9], (N_EVEN, D_MODEL, HYB_IN), D_MODEL)
    fox_b_f = FOX_GATE_BIAS + 0.5 * jax.random.normal(ks[10], (N_EVEN, FOX_HEADS), f32)
    hyb_w_out = nrm(ks[11], (N_EVEN, HYB_OUT, D_MODEL), HYB_OUT)
    mla_w_down = nrm(ks[12], (N_ODD, D_MODEL, MLA_DOWN), D_MODEL)
    mla_q_norm = 1.0 + 0.02 * jax.random.normal(ks[13], (N_ODD, MLA_Q_LORA), f32)
    mla_kv_norm = 1.0 + 0.02 * jax.random.normal(ks[14], (N_ODD, MLA_KV_LORA), f32)
    mla_w_uq = nrm(ks[15], (N_ODD, MLA_Q_LORA, MLA_HEADS * (MLA_NOPE + MLA_ROPE)), MLA_Q_LORA)
    mla_w_ukv = nrm(ks[16], (N_ODD, MLA_KV_LORA, MLA_HEADS * (MLA_NOPE + MLA_V)), MLA_KV_LORA)
    mla_w_out = nrm(ks[17], (N_ODD, MLA_HEADS * MLA_V, D_MODEL), MLA_HEADS * MLA_V)
    final_g = 1.0 + 0.02 * jax.random.normal(ks[18], (D_MODEL,), f32)
    return {'x': x, 'c': c, 'positions': positions, 'ada_w': ada_w, 'ada_b': ada_b,
            'norm_g': norm_g, 'ffn_w_gate': ffn_w_gate, 'ffn_w_up': ffn_w_up,
            'ffn_w_down': ffn_w_down, 'hyb_w_in': hyb_w_in, 'fox_b_f': fox_b_f,
            'hyb_w_out': hyb_w_out, 'mla_w_down': mla_w_down, 'mla_q_norm': mla_q_norm,
            'mla_kv_norm': mla_kv_norm, 'mla_w_uq': mla_w_uq, 'mla_w_ukv': mla_w_ukv,
            'mla_w_out': mla_w_out, 'final_g': final_g}


def reference(x, c, positions, ada_w, ada_b, norm_g, ffn_w_gate, ffn_w_up, ffn_w_down,
              hyb_w_in, fox_b_f, hyb_w_out, mla_w_down, mla_q_norm, mla_kv_norm,
              mla_w_uq, mla_w_ukv, mla_w_out, final_g):
    h = x
    cond = jax.nn.silu(c)
    for i in range(DEPTH):
        mod = (cond @ ada_w[i] + ada_b[i]).reshape(c.shape[0], N_MOD, D_MODEL)[:, :, None, :]
        sh1, sc1, g1, sh2, sc2, g2, sh3, sc3, g3 = [mod[:, j] for j in range(N_MOD)]
        u = _rmsnorm(h, norm_g[i, 0]) * (1 + sc1) + sh1
        h = h + MACARON_WEIGHT * g1 * _swiglu(u, ffn_w_gate[i, 0], ffn_w_up[i, 0], ffn_w_down[i, 0])
        u = _rmsnorm(h, norm_g[i, 1]) * (1 + sc2) + sh2
        j = i // 2
        if i % 2 == 0:
            mix = _hybrid_mixer(u, positions, hyb_w_in[j], fox_b_f[j], hyb_w_out[j])
        else:
            mix = _mla_mixer(u, positions, mla_w_down[j], mla_q_norm[j], mla_kv_norm[j],
                             mla_w_uq[j], mla_w_ukv[j], mla_w_out[j])
        h = h + g2 * mix
        u = _rmsnorm(h, norm_g[i, 2]) * (1 + sc3) + sh3
        h = h + MACARON_WEIGHT * g3 * _swiglu(u, ffn_w_gate[i, 1], ffn_w_up[i, 1], ffn_w_down[i, 1])
    return _rmsnorm(h, final_g)
```

```python
import functools
import math

import jax
import jax.numpy as jnp
from jax import lax
from jax.experimental import pallas as pl
from jax.experimental.pallas import tpu as pltpu

F32 = jnp.float32
BF16 = jnp.bfloat16

CHUNK = 64
RMS_EPS = 1e-6
ROPE_THETA = 500000.0
MLA_ROPE_THETA = 10000.0
MACARON_WEIGHT = 0.5
N_MOD = 9
FOX_HEADS = 8
HEAD_DIM = 64
DSA_ROT = 16
IDX_HEADS = 4
DSA_TOPK_MAX = 256
MLA_HEADS = 8
MLA_NOPE = 128
MLA_ROPE = 64
MLA_V = 128
MLA_Q_LORA = 384
MLA_KV_LORA = 256

LANES = 128
VMEM_LIMIT = 56 * 1024 * 1024

NEG_LOGIT = -1e30
NEG_SCORE = -3e38
IDX_NONE = 1e9

TM = 512
FFN_CHUNK = 256
ATT_TQ = 512
ATT_TK = 512
DSA_TQ = 256
DSA_TK = 512


def _cparams(sem):
    return pltpu.CompilerParams(dimension_semantics=sem, vmem_limit_bytes=VMEM_LIMIT)


def _const_spec(shape):
    nd = len(shape)
    return pl.BlockSpec(shape, lambda *_: (0,) * nd, pipeline_mode=pl.Buffered(1))


def _silu(x):
    return x * jax.nn.sigmoid(x)


def _modulated_norm(x, g, sc, sh):
    ms = jnp.mean(x * x, axis=-1, keepdims=True)
    y = x * lax.rsqrt(ms + RMS_EPS) * g
    return y * (1.0 + sc) + sh


def _mod_body(c_ref, w_ref, b_ref, o_ref):
    cond = _silu(c_ref[...]).astype(BF16)
    o_ref[0] = jnp.dot(cond, w_ref[0].astype(BF16), preferred_element_type=F32) + b_ref[0]


def _modulation(c, ada_w, ada_b):
    depth, d, n = ada_w.shape
    b = c.shape[0]
    tn = n // N_MOD
    return pl.pallas_call(
        _mod_body,
        grid=(depth, n // tn),
        in_specs=[pl.BlockSpec((b, d), lambda i, j: (0, 0)),
                  pl.BlockSpec((1, d, tn), lambda i, j: (i, 0, j)),
                  pl.BlockSpec((1, 1, tn), lambda i, j: (i, 0, j))],
        out_specs=pl.BlockSpec((1, b, tn), lambda i, j: (i, 0, j)),
        out_shape=jax.ShapeDtypeStruct((depth, b, n), F32),
        compiler_params=_cparams(("arbitrary", "arbitrary")),
        name="adaln_mod",
    )(c, ada_w, ada_b.reshape(depth, 1, n))


def _rope_table_body(pos_ref, invd_ref, sgnd_ref, invm_ref, sgnm_ref, cd_ref, sd_ref, cm_ref, sm_ref):
    pos = pos_ref[...].astype(F32)
    for inv_ref, sgn_ref, c_ref, s_ref in ((invd_ref, sgnd_ref, cd_ref, sd_ref),
                                           (invm_ref, sgnm_ref, cm_ref, sm_ref)):
        ang = pos * inv_ref[...]
        sgn = sgn_ref[...]
        c_ref[...] = jnp.where(sgn != 0.0, jnp.cos(ang), 1.0)
        s_ref[...] = sgn * jnp.sin(ang)


def _lane_pattern(rot, theta):
    half = rot // 2
    inv_freq = jnp.exp(-math.log(theta) * 2.0 * jnp.arange(half, dtype=F32) / rot)
    d = jnp.arange(LANES) % HEAD_DIM
    inv = jnp.where(d < rot, inv_freq[d % half], 0.0).astype(F32)
    sgn = jnp.where(d < half, -1.0, jnp.where(d < rot, 1.0, 0.0)).astype(F32)
    return inv.reshape(1, LANES), sgn.reshape(1, LANES)


def _rope_tables(positions):
    t = positions.size
    invd, sgnd = _lane_pattern(DSA_ROT, ROPE_THETA)
    invm, sgnm = _lane_pattern(MLA_ROPE, MLA_ROPE_THETA)
    tm = 1024
    row = pl.BlockSpec((tm, LANES), lambda i: (i, 0))
    vec = pl.BlockSpec((1, LANES), lambda i: (0, 0))
    tab = jax.ShapeDtypeStruct((t, LANES), F32)
    return pl.pallas_call(
        _rope_table_body,
        grid=(t // tm,),
        in_specs=[pl.BlockSpec((tm, 1), lambda i: (i, 0)), vec, vec, vec, vec],
        out_specs=[row, row, row, row],
        out_shape=[tab, tab, tab, tab],
        compiler_params=_cparams(("arbitrary",)),
        name="rope_tables",
    )(positions.reshape(t, 1), invd, sgnd, invm, sgnm)


def _rope_slab(y, cos, sin, half):
    lane = lax.broadcasted_iota(jnp.int32, (1, LANES), 1) % HEAD_DIM
    first = lane < half
    partner = jnp.where(first, pltpu.roll(y, LANES - half, 1), pltpu.roll(y, half, 1))
    return y * cos + partner * sin


def _ffn_body(h_ref, sh_ref, sc_ref, gt_ref, g_ref, wg_ref, wu_ref, wd_ref, *rest, nf, fc, final):
    if final:
        fg_ref, o_ref, acc_ref = rest
    else:
        o_ref, acc_ref = rest
    x = h_ref[...]
    u = _modulated_norm(x, g_ref[...], sc_ref[0], sh_ref[0]).astype(BF16)
    for f in range(nf):
        sl = slice(f * fc, (f + 1) * fc)
        gp = jnp.dot(u, wg_ref[:, sl], preferred_element_type=F32)
        up = jnp.dot(u, wu_ref[:, sl], preferred_element_type=F32)
        a = (_silu(gp) * up).astype(BF16)
        d = jnp.dot(a, wd_ref[sl, :], preferred_element_type=F32)
        if f == 0:
            acc_ref[...] = d
        else:
            acc_ref[...] += d
    y = x + (MACARON_WEIGHT * gt_ref[0]) * acc_ref[...]
    if final:
        ms = jnp.mean(y * y, axis=-1, keepdims=True)
        y = y * lax.rsqrt(ms + RMS_EPS) * fg_ref[...]
    o_ref[...] = y


def _ffn(h, sh, sc, gt, g, wg, wu, wd, s, final_g=None):
    t, d = h.shape
    f = wg.shape[1]
    tpb = s // TM
    row = pl.BlockSpec((TM, d), lambda i: (i, 0))
    mod = pl.BlockSpec((1, 1, d), lambda i: (i // tpb, 0, 0))
    in_specs = [row, mod, mod, mod, _const_spec((1, d)),
                _const_spec((d, f)), _const_spec((d, f)), _const_spec((f, d))]
    args = [h, sh, sc, gt, g, wg, wu, wd]
    if final_g is not None:
        in_specs.append(_const_spec((1, d)))
        args.append(final_g)
    return pl.pallas_call(
        functools.partial(_ffn_body, nf=f // FFN_CHUNK, fc=FFN_CHUNK, final=final_g is not None),
        grid=(t // TM,),
        in_specs=in_specs,
        out_specs=row,
        out_shape=jax.ShapeDtypeStruct((t, d), F32),
        scratch_shapes=[pltpu.VMEM((TM, d), F32)],
        compiler_params=_cparams(("arbitrary",)),
        name="swiglu_half_step",
    )(*args)


HYB_W = 512
HYB_COLS = 6 * HYB_W + 256 + 128 + 128


def _hyb_proj_body(h_ref, sh_ref, sc_ref, g_ref, w_ref, bf_ref, cos_ref, sin_ref,
                   fq_ref, fk_ref, fv_ref, dq_ref, dk_ref, dv_ref, iq_ref, ik_ref, cum_ref, iw_ref,
                   carry_ref, *, tpb):
    i = pl.program_id(0)
    x = h_ref[...]
    tm = x.shape[0]
    u = _modulated_norm(x, g_ref[...], sc_ref[0], sh_ref[0]).astype(BF16)
    cos = cos_ref[...]
    sin = sin_ref[...]
    half = DSA_ROT // 2

    def proj(c0, width):
        return jnp.dot(u, w_ref[:, c0:c0 + width], preferred_element_type=F32)

    def roped(y):
        return jnp.concatenate(
            [_rope_slab(y[:, j * LANES:(j + 1) * LANES], cos, sin, half) for j in range(y.shape[1] // LANES)],
            axis=1)

    fq_ref[0] = proj(0 * HYB_W, HYB_W).astype(BF16)
    fk_ref[0] = proj(1 * HYB_W, HYB_W).astype(BF16)
    fv_ref[0] = proj(2 * HYB_W, HYB_W).astype(BF16)
    dq_ref[0] = roped(proj(3 * HYB_W, HYB_W)).astype(BF16)
    dk_ref[0] = roped(proj(4 * HYB_W, HYB_W)).astype(BF16)
    dv_ref[0] = proj(5 * HYB_W, HYB_W).astype(BF16)
    tail = proj(6 * HYB_W, 512)
    iq_ref[0] = roped(tail[:, 0:256]).astype(BF16)
    ik_ref[0] = roped(tail[:, 256:384]).astype(BF16)
    gates = tail[:, 384:512]
    iw_ref[0] = gates

    z = gates + bf_ref[...]
    logf = jnp.minimum(z, 0.0) - jnp.log1p(jnp.exp(-jnp.abs(z)))
    rows = lax.broadcasted_iota(jnp.int32, (tm, LANES), 0)
    c = logf
    k = 1
    while k < tm:
        c = c + jnp.where(rows >= k, pltpu.roll(c, k, 0), 0.0)
        k *= 2

    @pl.when(i % tpb == 0)
    def _():
        carry_ref[...] = jnp.zeros_like(carry_ref)

    c = c + carry_ref[0:1, :]
    carry_ref[...] = jnp.broadcast_to(c[tm - 1:tm, :], carry_ref.shape)
    cum_ref[0] = c.T[0:FOX_HEADS, :]


def _hyb_proj(h, sh, sc, g, w, bf, cos, sin, b, s):
    t, d = h.shape
    tpb = s // TM
    row = pl.BlockSpec((TM, d), lambda i: (i, 0))
    mod = pl.BlockSpec((1, 1, d), lambda i: (i // tpb, 0, 0))
    tab = pl.BlockSpec((TM, LANES), lambda i: (i, 0))

    def out(width):
        return pl.BlockSpec((1, TM, width), lambda i: (i // tpb, i % tpb, 0))

    def shp(width, dt=BF16):
        return jax.ShapeDtypeStruct((b, s, width), dt)

    return pl.pallas_call(
        functools.partial(_hyb_proj_body, tpb=tpb),
        grid=(t // TM,),
        in_specs=[row, mod, mod, _const_spec((1, d)), _const_spec((d, HYB_COLS)), _const_spec((1, LANES)), tab, tab],
        out_specs=[out(HYB_W)] * 6 + [out(256), out(LANES),
                                      pl.BlockSpec((1, FOX_HEADS, TM), lambda i: (i // tpb, 0, i % tpb)),
                                      out(LANES)],
        out_shape=[shp(HYB_W)] * 6 + [shp(256), shp(LANES),
                                      jax.ShapeDtypeStruct((b, FOX_HEADS, s), F32), shp(LANES, F32)],
        scratch_shapes=[pltpu.VMEM((8, LANES), F32)],
        compiler_params=_cparams(("arbitrary",)),
        name="hybrid_in_proj",
    )(h, sh, sc, g, w, bf, cos, sin)


def _softmax_step(s, v, m, l, acc):
    m_new = jnp.maximum(m, jnp.max(s, axis=-1, keepdims=True))
    alpha = jnp.exp(m - m_new)
    p = jnp.exp(s - m_new)
    l = alpha * l + jnp.sum(p, axis=-1, keepdims=True)
    acc = alpha * acc + jnp.dot(p.astype(BF16), v, preferred_element_type=F32)
    return m_new, l, acc


def _softmax_init(tq, n):
    return (jnp.full((tq, 1), NEG_LOGIT, F32), jnp.zeros((tq, 1), F32), jnp.zeros((tq, n), F32))


def _head_mask(j):
    lane = lax.broadcasted_iota(jnp.int32, (1, LANES), 1)
    return (lane // HEAD_DIM) == j


def _qk(q, k):
    return lax.dot_general(q, k, (((1,), (1,)), ((), ())), preferred_element_type=F32)


def _fox_body(q_ref, k_ref, v_ref, cum_ref, o_ref, *, tq, tk):
    hp = pl.program_id(1)
    qi = pl.program_id(2)
    qfull = q_ref[0]
    row = lax.broadcasted_iota(jnp.int32, (tq, tk), 0)
    col = lax.broadcasted_iota(jnp.int32, (tq, tk), 1)
    outs = []
    for j in range(2):
        q = jnp.where(_head_mask(j), qfull, jnp.zeros_like(qfull))
        h = hp * 2 + j

        def logits(kt):
            ks = pl.multiple_of(kt * tk, tk)
            s = _qk(q, k_ref[0, pl.ds(ks, tk), :])
            return s - cum_ref[0, pl.ds(h, 1), pl.ds(ks, tk)], v_ref[0, pl.ds(ks, tk), :]

        def body(kt, carry):
            s, v = logits(kt)
            return _softmax_step(s, v, *carry)

        carry = lax.fori_loop(0, qi, body, _softmax_init(tq, LANES))
        s, v = logits(qi)
        s = jnp.where(col <= row, s, NEG_LOGIT)
        m, l, acc = _softmax_step(s, v, *carry)
        outs.append(acc / l)
    o_ref[0] = jnp.where(_head_mask(0), outs[0], outs[1]).astype(BF16)


def _fox_attention(fq, fk, fv, cum):
    b, s, w = fq.shape
    tq = tk = ATT_TQ
    qspec = pl.BlockSpec((1, tq, LANES), lambda bi, hp, qi: (bi, qi, hp))
    kspec = pl.BlockSpec((1, s, LANES), lambda bi, hp, qi: (bi, 0, hp))
    return pl.pallas_call(
        functools.partial(_fox_body, tq=tq, tk=tk),
        grid=(b, w // LANES, s // tq),
        in_specs=[qspec, kspec, kspec, pl.BlockSpec((1, FOX_HEADS, s), lambda bi, hp, qi: (bi, 0, 0))],
        out_specs=qspec,
        out_shape=jax.ShapeDtypeStruct((b, s, w), BF16),
        compiler_params=_cparams(("arbitrary", "arbitrary", "arbitrary")),
        name="fox_attention",
    )(fq, fk, fv, cum)


def _key_to_float(u):
    bits = jnp.where(u < 0, u & jnp.int32(0x7FFFFFFF), ~u)
    return pltpu.bitcast(bits, F32)


def _dsa_body(dq_ref, iq_ref, iw_ref, dk_ref, dv_ref, ik_ref, o_ref, sc_ref, tie_ref, *, tq, tk, top_k):
    qi = pl.program_id(1)
    idx_bits = max(1, (sc_ref.shape[1] - 1).bit_length())
    n_tiles = ((qi + 1) * tq + tk - 1) // tk
    row = lax.broadcasted_iota(jnp.int32, (tq, 1), 0) + qi * tq
    limit = (row // CHUNK + 1) * CHUNK
    col0 = lax.broadcasted_iota(jnp.int32, (tq, tk), 1)
    kf = jnp.float32(top_k)

    iqf = iq_ref[0]
    iw = iw_ref[0]
    iqs = [jnp.where(_head_mask(h % 2), iqf[:, (h // 2) * LANES:(h // 2 + 1) * LANES], jnp.zeros((tq, LANES), BF16))
           for h in range(IDX_HEADS)]
    ws = [iw[:, FOX_HEADS + h:FOX_HEADS + h + 1] for h in range(IDX_HEADS)]

    def score_tile(kt, _):
        ks = pl.multiple_of(kt * tk, tk)
        ik = ik_ref[0, pl.ds(ks, tk), :]
        sc = None
        for h in range(IDX_HEADS):
            term = jnp.maximum(_qk(iqs[h], ik), 0.0) * ws[h]
            sc = term if sc is None else sc + term
        sc_ref[:, pl.ds(ks, tk)] = jnp.where(col0 + ks < limit, sc, NEG_SCORE)
        return 0

    lax.fori_loop(0, n_tiles, score_tile, 0)

    def count(pred):
        def body(kt, cnt):
            ks = pl.multiple_of(kt * tk, tk)
            c = jnp.where(pred(kt, ks), 1.0, 0.0)
            part = c[:, 0:LANES]
            for j in range(1, tk // LANES):
                part = part + c[:, j * LANES:(j + 1) * LANES]
            return cnt + part
        cnt = lax.fori_loop(0, n_tiles, body, jnp.zeros((tq, LANES), F32))
        return jnp.sum(cnt, axis=-1, keepdims=True)

    def bit_step(it, u):
        cand = u | lax.shift_left(jnp.int32(1), 31 - it)
        thr = _key_to_float(cand)
        cnt = count(lambda kt, ks: sc_ref[:, pl.ds(ks, tk)] >= thr)
        return jnp.where(cnt >= kf, cand, u)

    u = lax.fori_loop(0, 32, bit_step, jnp.zeros((tq, 1), jnp.int32))
    thr = _key_to_float(u)

    n_gt = count(lambda kt, ks: sc_ref[:, pl.ds(ks, tk)] > thr)
    n_ge = count(lambda kt, ks: sc_ref[:, pl.ds(ks, tk)] >= thr)
    need = kf - n_gt

    def tie_tile(kt, _):
        ks = pl.multiple_of(kt * tk, tk)
        tie_ref[:, pl.ds(ks, tk)] = jnp.where(sc_ref[:, pl.ds(ks, tk)] == thr, (col0 + ks).astype(F32), IDX_NONE)
        return 0

    lax.fori_loop(0, n_tiles, tie_tile, 0)
    surplus = jnp.max(n_ge - n_gt - need)

    def tie_search():
        def step(it, x):
            cand = x + lax.shift_left(jnp.int32(1), idx_bits - 1 - it).astype(F32)
            cnt = count(lambda kt, ks: tie_ref[:, pl.ds(ks, tk)] < cand)
            return jnp.where(cnt < need, cand, x)
        return lax.fori_loop(0, idx_bits, step, jnp.zeros((tq, 1), F32))

    last_tie = lax.cond(surplus > 0.0, tie_search, lambda: jnp.full((tq, 1), IDX_NONE / 2, F32))

    def bias_tile(kt, _):
        ks = pl.multiple_of(kt * tk, tk)
        keep = ((sc_ref[:, pl.ds(ks, tk)] > thr) | (tie_ref[:, pl.ds(ks, tk)] <= last_tie)) & (col0 + ks < limit)
        sc_ref[:, pl.ds(ks, tk)] = jnp.where(keep, 0.0, NEG_LOGIT)
        return 0

    lax.fori_loop(0, n_tiles, bias_tile, 0)

    for hp in range(dq_ref.shape[2] // LANES):
        lanes = slice(hp * LANES, (hp + 1) * LANES)
        qfull = dq_ref[0, :, lanes]
        outs = []
        for j in range(2):
            q = jnp.where(_head_mask(j), qfull, jnp.zeros_like(qfull))

            def body(kt, carry):
                ks = pl.multiple_of(kt * tk, tk)
                s = _qk(q, dk_ref[0, pl.ds(ks, tk), lanes]) + sc_ref[:, pl.ds(ks, tk)]
                return _softmax_step(s, dv_ref[0, pl.ds(ks, tk), lanes], *carry)

            m, l, acc = lax.fori_loop(0, n_tiles, body, _softmax_init(tq, LANES))
            outs.append(acc / l)
        o_ref[0, :, lanes] = jnp.where(_head_mask(0), outs[0], outs[1]).astype(BF16)


def _dsa_attention(dq, dk, dv, iq, ik, iw):
    b, s, w = dq.shape
    tq, tk = DSA_TQ, DSA_TK
    top_k = min(DSA_TOPK_MAX, s // 4)

    def qspec(width):
        return pl.BlockSpec((1, tq, width), lambda bi, qi: (bi, qi, 0))

    def kspec(width):
        return pl.BlockSpec((1, s, width), lambda bi, qi: (bi, 0, 0))

    return pl.pallas_call(
        functools.partial(_dsa_body, tq=tq, tk=tk, top_k=top_k),
        grid=(b, s // tq),
        in_specs=[qspec(w), qspec(iq.shape[2]), qspec(LANES), kspec(w), kspec(w), kspec(LANES)],
        out_specs=qspec(w),
        out_shape=jax.ShapeDtypeStruct((b, s, w), BF16),
        scratch_shapes=[pltpu.VMEM((tq, s), F32), pltpu.VMEM((tq, s), F32)],
        compiler_params=_cparams(("arbitrary", "arbitrary")),
        name="dsa_attention",
    )(dq, iq, iw, dk, dv, ik)


MLA_DOWN_COLS = MLA_Q_LORA + MLA_KV_LORA + 2 * MLA_ROPE


def _mla_proj_body(h_ref, sh_ref, sc_ref, g_ref, wd_ref, qn_ref, kvn_ref, wuq_ref, wukv_ref, cos_ref, sin_ref,
                   qnope_ref, qrope_ref, knope_ref, v_ref, krope_ref, *, scale):
    x = h_ref[...]
    u = _modulated_norm(x, g_ref[...], sc_ref[0], sh_ref[0]).astype(BF16)
    cos = cos_ref[...]
    sin = sin_ref[...]
    half = MLA_ROPE // 2
    down = jnp.dot(u, wd_ref[...], preferred_element_type=F32)

    def norm(z, g):
        ms = jnp.mean(z * z, axis=-1, keepdims=True)
        return (z * lax.rsqrt(ms + RMS_EPS) * g).astype(BF16)

    cq = norm(down[:, 0:MLA_Q_LORA], qn_ref[...])
    ckv = norm(down[:, MLA_Q_LORA:MLA_Q_LORA + MLA_KV_LORA], kvn_ref[...])
    krope_ref[0] = _rope_slab(down[:, MLA_Q_LORA + MLA_KV_LORA:], cos, sin, half).astype(BF16)

    n_nope = MLA_HEADS * MLA_NOPE
    qn = jnp.dot(cq, wuq_ref[:, 0:n_nope], preferred_element_type=F32)
    qnope_ref[0] = (qn * scale).astype(BF16)
    qr = jnp.dot(cq, wuq_ref[:, n_nope:], preferred_element_type=F32)
    qr = jnp.concatenate(
        [_rope_slab(qr[:, j * LANES:(j + 1) * LANES], cos, sin, half) for j in range(qr.shape[1] // LANES)], axis=1)
    qrope_ref[0] = (qr * scale).astype(BF16)
    knope_ref[0] = jnp.dot(ckv, wukv_ref[:, 0:n_nope], preferred_element_type=F32).astype(BF16)
    v_ref[0] = jnp.dot(ckv, wukv_ref[:, n_nope:], preferred_element_type=F32).astype(BF16)


def _mla_proj(h, sh, sc, g, wd, qn, kvn, wuq, wukv, cos, sin, b, s):
    t, d = h.shape
    tpb = s // TM
    row = pl.BlockSpec((TM, d), lambda i: (i, 0))
    mod = pl.BlockSpec((1, 1, d), lambda i: (i // tpb, 0, 0))
    tab = pl.BlockSpec((TM, LANES), lambda i: (i, 0))

    def out(width):
        return pl.BlockSpec((1, TM, width), lambda i: (i // tpb, i % tpb, 0))

    def shp(width):
        return jax.ShapeDtypeStruct((b, s, width), BF16)

    n_nope = MLA_HEADS * MLA_NOPE
    n_rope = MLA_HEADS * MLA_ROPE
    n_v = MLA_HEADS * MLA_V
    scale = (MLA_NOPE + MLA_ROPE) ** -0.5
    return pl.pallas_call(
        functools.partial(_mla_proj_body, scale=scale),
        grid=(t // TM,),
        in_specs=[row, mod, mod, _const_spec((1, d)), _const_spec((d, MLA_DOWN_COLS)),
                  _const_spec((1, MLA_Q_LORA)), _const_spec((1, MLA_KV_LORA)),
                  _const_spec((MLA_Q_LORA, n_nope + n_rope)), _const_spec((MLA_KV_LORA, n_nope + n_v)), tab, tab],
        out_specs=[out(n_nope), out(n_rope), out(n_nope), out(n_v), out(LANES)],
        out_shape=[shp(n_nope), shp(n_rope), shp(n_nope), shp(n_v), shp(LANES)],
        compiler_params=_cparams(("arbitrary",)),
        name="mla_in_proj",
    )(h, sh, sc, g, wd, qn, kvn, wuq, wukv, cos, sin)


def _mla_body(qn_ref, qr_ref, kn_ref, kr_ref, v_ref, o_ref, *, tq, tk):
    h = pl.program_id(1)
    qi = pl.program_id(2)
    qr = qr_ref[0]
    lane = lax.broadcasted_iota(jnp.int32, (1, LANES), 1)
    mine = (lane // HEAD_DIM) == (h % 2)
    q = jnp.concatenate([qn_ref[0], jnp.where(mine, qr, jnp.zeros_like(qr))], axis=1)
    row = lax.broadcasted_iota(jnp.int32, (tq, tk), 0)
    col = lax.broadcasted_iota(jnp.int32, (tq, tk), 1)

    def logits(kt):
        ks = pl.multiple_of(kt * tk, tk)
        k = jnp.concatenate([kn_ref[0, pl.ds(ks, tk), :], kr_ref[0, pl.ds(ks, tk), :]], axis=1)
        return _qk(q, k), v_ref[0, pl.ds(ks, tk), :]

    def body(kt, carry):
        s, v = logits(kt)
        return _softmax_step(s, v, *carry)

    carry = lax.fori_loop(0, qi, body, _softmax_init(tq, LANES))
    s, v = logits(qi)
    s = jnp.where(col // CHUNK <= row // CHUNK, s, NEG_LOGIT)
    m, l, acc = _softmax_step(s, v, *carry)
    o_ref[0] = (acc / l).astype(BF16)


def _mla_attention(qnope, qrope, knope, krope, v):
    b, s, w = qnope.shape
    tq = tk = ATT_TQ
    qspec = pl.BlockSpec((1, tq, LANES), lambda bi, h, qi: (bi, qi, h))
    qrspec = pl.BlockSpec((1, tq, LANES), lambda bi, h, qi: (bi, qi, h // 2))
    kspec = pl.BlockSpec((1, s, LANES), lambda bi, h, qi: (bi, 0, h))
    krspec = pl.BlockSpec((1, s, LANES), lambda bi, h, qi: (bi, 0, 0))
    return pl.pallas_call(
        functools.partial(_mla_body, tq=tq, tk=tk),
        grid=(b, MLA_HEADS, s // tq),
        in_specs=[qspec, qrspec, kspec, krspec, kspec],
        out_specs=qspec,
        out_shape=jax.ShapeDtypeStruct((b, s, w), BF16),
        compiler_params=_cparams(("arbitrary", "arbitrary", "arbitrary")),
        name="mla_attention",
    )(qnope, qrope, knope, krope, v)


def _out_proj_body(*refs, n_in):
    h_ref, gt_ref = refs[0], refs[1]
    xs = refs[2:2 + n_in]
    ws = refs[2 + n_in:2 + 2 * n_in]
    o_ref = refs[2 + 2 * n_in]
    mix = None
    for x_ref, w_ref in zip(xs, ws):
        d = jnp.dot(x_ref[...], w_ref[...], preferred_element_type=F32)
        mix = d if mix is None else mix + d
    o_ref[...] = h_ref[...] + gt_ref[0] * mix


def _out_proj(h, gt, xs, ws, s):
    t, d = h.shape
    tpb = s // TM
    row = pl.BlockSpec((TM, d), lambda i: (i, 0))
    mod = pl.BlockSpec((1, 1, d), lambda i: (i // tpb, 0, 0))
    xspecs = [pl.BlockSpec((TM, x.shape[1]), lambda i: (i, 0)) for x in xs]
    wspecs = [_const_spec(w.shape) for w in ws]
    return pl.pallas_call(
        functools.partial(_out_proj_body, n_in=len(xs)),
        grid=(t // TM,),
        in_specs=[row, mod] + xspecs + wspecs,
        out_specs=row,
        out_shape=jax.ShapeDtypeStruct((t, d), F32),
        compiler_params=_cparams(("arbitrary",)),
        name="mixer_out_proj",
    )(h, gt, *xs, *ws)


def _hyb_weight(w_in, b_f):
    o = 0
    parts = {}
    for name, width in (("fq", 512), ("fk", 512), ("fv", 512), ("ff", FOX_HEADS), ("dq", 512), ("dk", 512),
                        ("dv", 512), ("iq", 256), ("iw", IDX_HEADS), ("ik", HEAD_DIM)):
        parts[name] = w_in[:, o:o + width]
        o += width
    qs = HEAD_DIM ** -0.5
    pad = jnp.zeros((w_in.shape[0], LANES - FOX_HEADS - IDX_HEADS), w_in.dtype)
    w = jnp.concatenate([parts["fq"] * qs, parts["fk"], parts["fv"], parts["dq"] * qs, parts["dk"], parts["dv"],
                         parts["iq"] * qs, parts["ik"], parts["ik"],
                         parts["ff"], parts["iw"] * IDX_HEADS ** -0.5, pad], axis=1)
    bf = jnp.concatenate([b_f, jnp.zeros((LANES - FOX_HEADS,), b_f.dtype)]).reshape(1, LANES)
    return w.astype(BF16), bf.astype(F32)


def _mla_weights(w_down, w_uq, w_ukv):
    kr = w_down[:, MLA_Q_LORA + MLA_KV_LORA:]
    wd = jnp.concatenate([w_down, kr], axis=1)
    uq = w_uq.reshape(MLA_Q_LORA, MLA_HEADS, MLA_NOPE + MLA_ROPE)
    wuq = jnp.concatenate([uq[:, :, :MLA_NOPE].reshape(MLA_Q_LORA, -1), uq[:, :, MLA_NOPE:].reshape(MLA_Q_LORA, -1)],
                          axis=1)
    ukv = w_ukv.reshape(MLA_KV_LORA, MLA_HEADS, MLA_NOPE + MLA_V)
    wukv = jnp.concatenate([ukv[:, :, :MLA_NOPE].reshape(MLA_KV_LORA, -1),
                            ukv[:, :, MLA_NOPE:].reshape(MLA_KV_LORA, -1)], axis=1)
    return wd.astype(BF16), wuq.astype(BF16), wukv.astype(BF16)


def kernel(x, c, positions, ada_w, ada_b, norm_g, ffn_w_gate, ffn_w_up, ffn_w_down, hyb_w_in, fox_b_f, hyb_w_out,
           mla_w_down, mla_q_norm, mla_kv_norm, mla_w_uq, mla_w_ukv, mla_w_out, final_g):
    b, s, d = x.shape
    depth = ada_w.shape[0]
    t = b * s
    assert s % TM == 0 and s % ATT_TQ == 0 and s % DSA_TK == 0 and ffn_w_gate.shape[-1] % FFN_CHUNK == 0

    mod = _modulation(c, ada_w, ada_b).reshape(depth, b, N_MOD, 1, d)
    cos_d, sin_d, cos_m, sin_m = _rope_tables(positions)
    wg = ffn_w_gate.astype(BF16)
    wu = ffn_w_up.astype(BF16)
    wdn = ffn_w_down.astype(BF16)

    h = x.reshape(t, d)
    for i in range(depth):
        sh1, sc1, g1, sh2, sc2, g2, sh3, sc3, g3 = [mod[i, :, j] for j in range(N_MOD)]
        ng = norm_g[i].reshape(3, 1, d)
        h = _ffn(h, sh1, sc1, g1, ng[0], wg[i, 0], wu[i, 0], wdn[i, 0], s)
        j = i // 2
        if i % 2 == 0:
            w_in, bf = _hyb_weight(hyb_w_in[j], fox_b_f[j])
            fq, fk, fv, dq, dk, dv, iq, ik, cum, iw = _hyb_proj(h, sh2, sc2, ng[1], w_in, bf, cos_d, sin_d, b, s)
            out_a = _fox_attention(fq, fk, fv, cum)
            out_b = _dsa_attention(dq, dk, dv, iq, ik, iw)
            w_out = hyb_w_out[j].astype(BF16)
            half = out_a.shape[2]
            h = _out_proj(h, g2, [out_a.reshape(t, half), out_b.reshape(t, half)], [w_out[:half], w_out[half:]], s)
        else:
            wd, wuq, wukv = _mla_weights(mla_w_down[j], mla_w_uq[j], mla_w_ukv[j])
            qn, qr, kn, v, kr = _mla_proj(h, sh2, sc2, ng[1], wd, mla_q_norm[j].reshape(1, -1),
                                          mla_kv_norm[j].reshape(1, -1), wuq, wukv, cos_m, sin_m, b, s)
            out = _mla_attention(qn, qr, kn, kr, v)
            h = _out_proj(h, g2, [out.reshape(t, -1)], [mla_w_out[j].astype(BF16)], s)
        last = i == depth - 1
        h = _ffn(h, sh3, sc3, g3, ng[2], wg[i, 1], wu[i, 1], wdn[i, 1], s,
                 final_g=final_g.reshape(1, d) if last else None)
    return h.reshape(b, s, d)
```

```python
import functools
import math

import jax
import jax.numpy as jnp
from jax import lax
from jax.experimental import pallas as pl
from jax.experimental.pallas import tpu as pltpu

F32 = jnp.float32
BF16 = jnp.bfloat16

CHUNK = 64
RMS_EPS = 1e-6
ROPE_THETA = 500000.0
MLA_ROPE_THETA = 10000.0
MACARON_WEIGHT = 0.5
N_MOD = 9
FOX_HEADS = 8
HEAD_DIM = 64
DSA_ROT = 16
IDX_HEADS = 4
DSA_TOPK_MAX = 256
MLA_HEADS = 8
MLA_NOPE = 128
MLA_ROPE = 64
MLA_V = 128
MLA_Q_LORA = 384
MLA_KV_LORA = 256

LANES = 128
VMEM_LIMIT = 56 * 1024 * 1024

NEG_LOGIT = -1e30
NEG_SCORE = -3e38
IDX_NONE = 1e9

TM = 512
FFN_CHUNK = 256
ATT_TQ = 512
ATT_TK = 512
DSA_TQ = 256
DSA_TK = 512
COUNT_ROWS = 64


def _cparams(sem):
    return pltpu.CompilerParams(dimension_semantics=sem, vmem_limit_bytes=VMEM_LIMIT)


def _const_spec(shape):
    nd = len(shape)
    return pl.BlockSpec(shape, lambda *_: (0,) * nd, pipeline_mode=pl.Buffered(1))


def _silu(x):
    return x * jax.nn.sigmoid(x)


def _modulated_norm(x, g, sc, sh):
    ms = jnp.mean(x * x, axis=-1, keepdims=True)
    y = x * lax.rsqrt(ms + RMS_EPS) * g
    return y * (1.0 + sc) + sh


def _mod_body(c_ref, w_ref, b_ref, o_ref):
    cond = _silu(c_ref[...]).astype(BF16)
    o_ref[0] = jnp.dot(cond, w_ref[0].astype(BF16), preferred_element_type=F32) + b_ref[0]


def _modulation(c, ada_w, ada_b):
    depth, d, n = ada_w.shape
    b = c.shape[0]
    tn = n // N_MOD
    return pl.pallas_call(
        _mod_body,
        grid=(depth, n // tn),
        in_specs=[pl.BlockSpec((b, d), lambda i, j: (0, 0)),
                  pl.BlockSpec((1, d, tn), lambda i, j: (i, 0, j)),
                  pl.BlockSpec((1, 1, tn), lambda i, j: (i, 0, j))],
        out_specs=pl.BlockSpec((1, b, tn), lambda i, j: (i, 0, j)),
        out_shape=jax.ShapeDtypeStruct((depth, b, n), F32),
        compiler_params=_cparams(("arbitrary", "arbitrary")),
        name="adaln_mod",
    )(c, ada_w, ada_b.reshape(depth, 1, n))


def _rope_table_body(pos_ref, invd_ref, sgnd_ref, invm_ref, sgnm_ref, cd_ref, sd_ref, cm_ref, sm_ref):
    pos = pos_ref[...].astype(F32)
    for inv_ref, sgn_ref, c_ref, s_ref in ((invd_ref, sgnd_ref, cd_ref, sd_ref),
                                           (invm_ref, sgnm_ref, cm_ref, sm_ref)):
        ang = pos * inv_ref[...]
        sgn = sgn_ref[...]
        c_ref[...] = jnp.where(sgn != 0.0, jnp.cos(ang), 1.0)
        s_ref[...] = sgn * jnp.sin(ang)


def _lane_pattern(rot, theta):
    half = rot // 2
    inv_freq = jnp.exp(-math.log(theta) * 2.0 * jnp.arange(half, dtype=F32) / rot)
    d = jnp.arange(LANES) % HEAD_DIM
    inv = jnp.where(d < rot, inv_freq[d % half], 0.0).astype(F32)
    sgn = jnp.where(d < half, -1.0, jnp.where(d < rot, 1.0, 0.0)).astype(F32)
    return inv.reshape(1, LANES), sgn.reshape(1, LANES)


def _rope_tables(positions):
    t = positions.size
    invd, sgnd = _lane_pattern(DSA_ROT, ROPE_THETA)
    invm, sgnm = _lane_pattern(MLA_ROPE, MLA_ROPE_THETA)
    tm = 1024
    row = pl.BlockSpec((tm, LANES), lambda i: (i, 0))
    vec = pl.BlockSpec((1, LANES), lambda i: (0, 0))
    tab = jax.ShapeDtypeStruct((t, LANES), F32)
    return pl.pallas_call(
        _rope_table_body,
        grid=(t // tm,),
        in_specs=[pl.BlockSpec((tm, 1), lambda i: (i, 0)), vec, vec, vec, vec],
        out_specs=[row, row, row, row],
        out_shape=[tab, tab, tab, tab],
        compiler_params=_cparams(("arbitrary",)),
        name="rope_tables",
    )(positions.reshape(t, 1), invd, sgnd, invm, sgnm)


def _rope_slab(y, cos, sin, half):
    lane = lax.broadcasted_iota(jnp.int32, (1, LANES), 1) % HEAD_DIM
    first = lane < half
    partner = jnp.where(first, pltpu.roll(y, LANES - half, 1), pltpu.roll(y, half, 1))
    return y * cos + partner * sin


def _ffn_body(h_ref, sh_ref, sc_ref, gt_ref, g_ref, wg_ref, wu_ref, wd_ref, *rest, nf, fc, final):
    if final:
        fg_ref, o_ref, acc_ref = rest
    else:
        o_ref, acc_ref = rest
    x = h_ref[...]
    u = _modulated_norm(x, g_ref[...], sc_ref[0], sh_ref[0]).astype(BF16)
    for f in range(nf):
        sl = slice(f * fc, (f + 1) * fc)
        gp = jnp.dot(u, wg_ref[:, sl], preferred_element_type=F32)
        up = jnp.dot(u, wu_ref[:, sl], preferred_element_type=F32)
        a = (_silu(gp) * up).astype(BF16)
        d = jnp.dot(a, wd_ref[sl, :], preferred_element_type=F32)
        if f == 0:
            acc_ref[...] = d
        else:
            acc_ref[...] += d
    y = x + (MACARON_WEIGHT * gt_ref[0]) * acc_ref[...]
    if final:
        ms = jnp.mean(y * y, axis=-1, keepdims=True)
        y = y * lax.rsqrt(ms + RMS_EPS) * fg_ref[...]
    o_ref[...] = y


def _ffn(h, sh, sc, gt, g, wg, wu, wd, s, final_g=None):
    t, d = h.shape
    f = wg.shape[1]
    tpb = s // TM
    row = pl.BlockSpec((TM, d), lambda i: (i, 0))
    mod = pl.BlockSpec((1, 1, d), lambda i: (i // tpb, 0, 0))
    in_specs = [row, mod, mod, mod, _const_spec((1, d)),
                _const_spec((d, f)), _const_spec((d, f)), _const_spec((f, d))]
    args = [h, sh, sc, gt, g, wg, wu, wd]
    if final_g is not None:
        in_specs.append(_const_spec((1, d)))
        args.append(final_g)
    return pl.pallas_call(
        functools.partial(_ffn_body, nf=f // FFN_CHUNK, fc=FFN_CHUNK, final=final_g is not None),
        grid=(t // TM,),
        in_specs=in_specs,
        out_specs=row,
        out_shape=jax.ShapeDtypeStruct((t, d), F32),
        scratch_shapes=[pltpu.VMEM((TM, d), F32)],
        compiler_params=_cparams(("arbitrary",)),
        name="swiglu_half_step",
    )(*args)


HYB_W = 512
HYB_COLS = 6 * HYB_W + 256 + 128 + 128


def _hyb_proj_body(h_ref, sh_ref, sc_ref, g_ref, w_ref, bf_ref, cos_ref, sin_ref,
                   fq_ref, fk_ref, fvt_ref, dq_ref, dk_ref, dvt_ref, iq_ref, ik_ref, cum_ref, iw_ref,
                   carry_ref, *, tpb):
    i = pl.program_id(0)
    x = h_ref[...]
    tm = x.shape[0]
    u = _modulated_norm(x, g_ref[...], sc_ref[0], sh_ref[0]).astype(BF16)
    cos = cos_ref[...]
    sin = sin_ref[...]
    half = DSA_ROT // 2

    def proj(c0, width):
        return jnp.dot(u, w_ref[:, c0:c0 + width], preferred_element_type=F32)

    def roped(y):
        return jnp.concatenate(
            [_rope_slab(y[:, j * LANES:(j + 1) * LANES], cos, sin, half) for j in range(y.shape[1] // LANES)],
            axis=1)

    fq_ref[0] = proj(0 * HYB_W, HYB_W).astype(BF16)
    fk_ref[0] = proj(1 * HYB_W, HYB_W).astype(BF16)
    fvt_ref[0] = proj(2 * HYB_W, HYB_W).T.astype(BF16)
    dq_ref[0] = roped(proj(3 * HYB_W, HYB_W)).astype(BF16)
    dk_ref[0] = roped(proj(4 * HYB_W, HYB_W)).astype(BF16)
    dvt_ref[0] = proj(5 * HYB_W, HYB_W).T.astype(BF16)
    tail = proj(6 * HYB_W, 512)
    iq_ref[0] = roped(tail[:, 0:256]).astype(BF16)
    ik_ref[0] = roped(tail[:, 256:384]).astype(BF16)
    gates = tail[:, 384:512]
    iw_ref[0] = gates.T[FOX_HEADS:2 * FOX_HEADS, :]

    z = gates + bf_ref[...]
    logf = jnp.minimum(z, 0.0) - jnp.log1p(jnp.exp(-jnp.abs(z)))
    rows = lax.broadcasted_iota(jnp.int32, (tm, LANES), 0)
    c = logf
    k = 1
    while k < tm:
        c = c + jnp.where(rows >= k, pltpu.roll(c, k, 0), 0.0)
        k *= 2

    @pl.when(i % tpb == 0)
    def _():
        carry_ref[...] = jnp.zeros_like(carry_ref)

    c = c + carry_ref[0:1, :]
    carry_ref[...] = jnp.broadcast_to(c[tm - 1:tm, :], carry_ref.shape)
    lane = lax.broadcasted_iota(jnp.int32, (1, LANES), 1)
    c = jnp.where(lane < FOX_HEADS, c, 0.0)
    hi = c.astype(BF16).astype(F32)
    mid = (c - hi).astype(BF16).astype(F32)
    lo = (c - hi - mid).astype(BF16).astype(F32)
    cum_ref[0] = (hi + pltpu.roll(mid, FOX_HEADS, 1) + pltpu.roll(lo, 2 * FOX_HEADS, 1)).astype(BF16)


def _hyb_proj(h, sh, sc, g, w, bf, cos, sin, b, s):
    t, d = h.shape
    tpb = s // TM
    row = pl.BlockSpec((TM, d), lambda i: (i, 0))
    mod = pl.BlockSpec((1, 1, d), lambda i: (i // tpb, 0, 0))
    tab = pl.BlockSpec((TM, LANES), lambda i: (i, 0))

    def out(width):
        return pl.BlockSpec((1, TM, width), lambda i: (i // tpb, i % tpb, 0))

    def shp(width, dt=BF16):
        return jax.ShapeDtypeStruct((b, s, width), dt)

    def out_t(width):
        return pl.BlockSpec((1, width, TM), lambda i: (i // tpb, 0, i % tpb))

    def shp_t(width, dt=BF16):
        return jax.ShapeDtypeStruct((b, width, s), dt)

    return pl.pallas_call(
        functools.partial(_hyb_proj_body, tpb=tpb),
        grid=(t // TM,),
        in_specs=[row, mod, mod, _const_spec((1, d)), _const_spec((d, HYB_COLS)), _const_spec((1, LANES)), tab, tab],
        out_specs=[out(HYB_W), out(HYB_W), out_t(HYB_W), out(HYB_W), out(HYB_W), out_t(HYB_W),
                   out(256), out(LANES), out(LANES), out_t(FOX_HEADS)],
        out_shape=[shp(HYB_W), shp(HYB_W), shp_t(HYB_W), shp(HYB_W), shp(HYB_W), shp_t(HYB_W),
                   shp(256), shp(LANES), shp(LANES), shp_t(FOX_HEADS, F32)],
        scratch_shapes=[pltpu.VMEM((8, LANES), F32)],
        compiler_params=_cparams(("arbitrary",)),
        name="hybrid_in_proj",
    )(h, sh, sc, g, w, bf, cos, sin)


def _softmax_step(s, smax, vt, m, l, acc):
    m_new = jnp.maximum(m, smax)
    alpha = jnp.exp(m - m_new)
    p = jnp.exp(s - m_new)
    l = alpha * l + jnp.sum(p, axis=0, keepdims=True)
    acc = alpha * acc + jnp.dot(vt, p.astype(BF16), preferred_element_type=F32)
    return m_new, l, acc


def _softmax_init(tq, n):
    return (jnp.full((1, tq), NEG_LOGIT, F32), jnp.zeros((1, tq), F32), jnp.zeros((n, tq), F32))


def _attend(n_heads, n_tiles, qk, values, tq, n, mask_last=None):
    def logits(hd, kt):
        s = qk(hd, kt)
        return s, jnp.max(s, axis=0, keepdims=True)

    def body(kt, carry):
        stats, cur = carry
        nxt = tuple(logits(hd, kt + 1) for hd in range(n_heads))
        out = []
        for hd in range(n_heads):
            out.extend(_softmax_step(*cur[hd], values(hd, kt), *stats[3 * hd:3 * hd + 3]))
        return tuple(out), nxt

    first = tuple(logits(hd, 0) for hd in range(n_heads))
    stats, last = lax.fori_loop(0, n_tiles - 1, body, (_softmax_init(tq, n) * n_heads, first))
    outs = []
    for hd in range(n_heads):
        s, smax = last[hd]
        if mask_last is not None:
            s = mask_last(s)
            smax = jnp.max(s, axis=0, keepdims=True)
        m, l, acc = _softmax_step(s, smax, values(hd, n_tiles - 1), *stats[3 * hd:3 * hd + 3])
        outs.append(acc / l)
    return outs


def _head_mask(j):
    lane = lax.broadcasted_iota(jnp.int32, (1, LANES), 1)
    return (lane // HEAD_DIM) == j


def _qk(q, k):
    return lax.dot_general(q, k, (((1,), (1,)), ((), ())), preferred_element_type=F32)


def _pair_out(outs):
    rows = lax.broadcasted_iota(jnp.int32, (LANES, 1), 0)
    return jnp.where(rows < HEAD_DIM, outs[0], outs[1]).T.astype(BF16)


def _fox_body(q_ref, k_ref, c_ref, vt_ref, o_ref, *, tq, tk):
    hp = pl.program_id(1)
    qi = pl.program_id(2)
    qfull = q_ref[0]
    krow = lax.broadcasted_iota(jnp.int32, (tk, tq), 0)
    qcol = lax.broadcasted_iota(jnp.int32, (tk, tq), 1)
    lane = lax.broadcasted_iota(jnp.int32, (tq, LANES), 1)
    qs = []
    for j in range(2):
        pick = (lane % FOX_HEADS == hp * 2 + j) & (lane < 3 * FOX_HEADS)
        sel = jnp.where(pick, -1.0, 0.0).astype(BF16)
        qs.append(jnp.concatenate([jnp.where(_head_mask(j), qfull, jnp.zeros_like(qfull)), sel], axis=1))

    def qk(j, kt):
        ks = pl.multiple_of(kt * tk, tk)
        k = jnp.concatenate([k_ref[0, pl.ds(ks, tk), :], c_ref[0, pl.ds(ks, tk), :]], axis=1)
        return _qk(k, qs[j])

    def values(j, kt):
        return vt_ref[0, :, pl.ds(pl.multiple_of(kt * tk, tk), tk)]

    def causal(s):
        return jnp.where(krow <= qcol, s, NEG_LOGIT)

    o_ref[0] = _pair_out(_attend(2, qi + 1, qk, values, tq, LANES, mask_last=causal))


def _fox_attention(fq, fk, cparts, fvt):
    b, s, w = fq.shape
    tq = tk = ATT_TQ
    qspec = pl.BlockSpec((1, tq, LANES), lambda bi, hp, qi: (bi, qi, hp))
    kspec = pl.BlockSpec((1, s, LANES), lambda bi, hp, qi: (bi, 0, hp))
    cspec = pl.BlockSpec((1, s, LANES), lambda bi, hp, qi: (bi, 0, 0))
    vspec = pl.BlockSpec((1, LANES, s), lambda bi, hp, qi: (bi, hp, 0))
    return pl.pallas_call(
        functools.partial(_fox_body, tq=tq, tk=tk),
        grid=(b, w // LANES, s // tq),
        in_specs=[qspec, kspec, cspec, vspec],
        out_specs=qspec,
        out_shape=jax.ShapeDtypeStruct((b, s, w), BF16),
        compiler_params=_cparams(("arbitrary", "arbitrary", "arbitrary")),
        name="fox_attention",
    )(fq, fk, cparts, fvt)


def _key_to_float(u):
    bits = jnp.where(u < 0, u & jnp.int32(0x7FFFFFFF), ~u)
    return pltpu.bitcast(bits, F32)


def _dsa_body(dq_ref, iq_ref, iw_ref, dk_ref, dvt_ref, ik_ref, o_ref, sc_ref, tie_ref, *, tq, tk, top_k):
    qi = pl.program_id(1)
    idx_bits = max(1, (sc_ref.shape[0] - 1).bit_length())
    n_tiles = ((qi + 1) * tq + tk - 1) // tk
    qpos = lax.broadcasted_iota(jnp.int32, (1, tq), 1) + qi * tq
    limit = (qpos // CHUNK + 1) * CHUNK
    krow = lax.broadcasted_iota(jnp.int32, (tk, tq), 0)
    kf = jnp.float32(top_k)

    iqf = iq_ref[0]
    iw = iw_ref[0]
    iqs = [jnp.where(_head_mask(h % 2), iqf[:, (h // 2) * LANES:(h // 2 + 1) * LANES], jnp.zeros((tq, LANES), BF16))
           for h in range(IDX_HEADS)]
    ws = [iw[h:h + 1, :] for h in range(IDX_HEADS)]

    def score_tile(kt, _):
        ks = pl.multiple_of(kt * tk, tk)
        ik = ik_ref[0, pl.ds(ks, tk), :]
        sc = None
        for h in range(IDX_HEADS):
            term = jnp.maximum(_qk(ik, iqs[h]), 0.0) * ws[h]
            sc = term if sc is None else sc + term
        sc_ref[pl.ds(ks, tk), :] = jnp.where(krow + ks < limit, sc, NEG_SCORE)
        return 0

    lax.fori_loop(0, n_tiles, score_tile, 0)

    def count(pred):
        def body(kt, cnt):
            ks = pl.multiple_of(kt * tk, tk)
            c = jnp.where(pred(ks), 1.0, 0.0)
            return cnt + jnp.sum(c.reshape(tk // COUNT_ROWS, COUNT_ROWS, tq), axis=0)
        cnt = lax.fori_loop(0, n_tiles, body, jnp.zeros((COUNT_ROWS, tq), F32))
        return jnp.sum(cnt, axis=0, keepdims=True)

    def bit_step(it, u):
        cand = u | lax.shift_left(jnp.int32(1), 31 - it)
        thr = _key_to_float(cand)
        cnt = count(lambda ks: sc_ref[pl.ds(ks, tk), :] >= thr)
        return jnp.where(cnt >= kf, cand, u)

    u = lax.fori_loop(0, 32, bit_step, jnp.zeros((1, tq), jnp.int32))
    thr = _key_to_float(u)

    n_gt = count(lambda ks: sc_ref[pl.ds(ks, tk), :] > thr)
    n_ge = count(lambda ks: sc_ref[pl.ds(ks, tk), :] >= thr)
    need = kf - n_gt

    def tie_tile(kt, _):
        ks = pl.multiple_of(kt * tk, tk)
        tie_ref[pl.ds(ks, tk), :] = jnp.where(sc_ref[pl.ds(ks, tk), :] == thr, (krow + ks).astype(F32), IDX_NONE)
        return 0

    lax.fori_loop(0, n_tiles, tie_tile, 0)
    surplus = jnp.max(n_ge - n_gt - need)

    def tie_search():
        def step(it, x):
            cand = x + lax.shift_left(jnp.int32(1), idx_bits - 1 - it).astype(F32)
            cnt = count(lambda ks: tie_ref[pl.ds(ks, tk), :] < cand)
            return jnp.where(cnt < need, cand, x)
        return lax.fori_loop(0, idx_bits, step, jnp.zeros((1, tq), F32))

    last_tie = lax.cond(surplus > 0.0, tie_search, lambda: jnp.full((1, tq), IDX_NONE / 2, F32))

    def bias_tile(kt, _):
        ks = pl.multiple_of(kt * tk, tk)
        keep = ((sc_ref[pl.ds(ks, tk), :] > thr) | (tie_ref[pl.ds(ks, tk), :] <= last_tie)) & (krow + ks < limit)
        sc_ref[pl.ds(ks, tk), :] = jnp.where(keep, 0.0, NEG_LOGIT)
        return 0

    lax.fori_loop(0, n_tiles, bias_tile, 0)

    n_heads = dq_ref.shape[2] // HEAD_DIM
    qs = []
    for hd in range(n_heads):
        qfull = dq_ref[0, :, (hd // 2) * LANES:(hd // 2 + 1) * LANES]
        qs.append(jnp.where(_head_mask(hd % 2), qfull, jnp.zeros_like(qfull)))

    def qk(hd, kt):
        ks = pl.multiple_of(kt * tk, tk)
        lanes = slice((hd // 2) * LANES, (hd // 2 + 1) * LANES)
        return _qk(dk_ref[0, pl.ds(ks, tk), lanes], qs[hd]) + sc_ref[pl.ds(ks, tk), :]

    def values(hd, kt):
        return dvt_ref[0, (hd // 2) * LANES:(hd // 2 + 1) * LANES, pl.ds(pl.multiple_of(kt * tk, tk), tk)]

    outs = _attend(n_heads, n_tiles, qk, values, tq, LANES)
    for hp in range(n_heads // 2):
        o_ref[0, :, hp * LANES:(hp + 1) * LANES] = _pair_out(outs[2 * hp:2 * hp + 2])


def _dsa_attention(dq, dk, dvt, iq, ik, iw):
    b, s, w = dq.shape
    tq, tk = DSA_TQ, DSA_TK
    top_k = min(DSA_TOPK_MAX, s // 4)

    def qspec(width):
        return pl.BlockSpec((1, tq, width), lambda bi, qi: (bi, qi, 0))

    def kspec(width):
        return pl.BlockSpec((1, s, width), lambda bi, qi: (bi, 0, 0))

    return pl.pallas_call(
        functools.partial(_dsa_body, tq=tq, tk=tk, top_k=top_k),
        grid=(b, s // tq),
        in_specs=[qspec(w), qspec(iq.shape[2]), pl.BlockSpec((1, iw.shape[1], tq), lambda bi, qi: (bi, 0, qi)),
                  kspec(w), pl.BlockSpec((1, w, s), lambda bi, qi: (bi, 0, 0)), kspec(LANES)],
        out_specs=qspec(w),
        out_shape=jax.ShapeDtypeStruct((b, s, w), BF16),
        scratch_shapes=[pltpu.VMEM((s, tq), F32), pltpu.VMEM((s, tq), F32)],
        compiler_params=_cparams(("arbitrary", "arbitrary")),
        name="dsa_attention",
    )(dq, iq, iw, dk, dvt, ik)


MLA_DOWN_COLS = MLA_Q_LORA + MLA_KV_LORA + 2 * MLA_ROPE


def _mla_proj_body(h_ref, sh_ref, sc_ref, g_ref, wd_ref, qn_ref, kvn_ref, wuq_ref, wukv_ref, cos_ref, sin_ref,
                   qnope_ref, qrope_ref, knope_ref, v_ref, krope_ref, *, scale):
    x = h_ref[...]
    u = _modulated_norm(x, g_ref[...], sc_ref[0], sh_ref[0]).astype(BF16)
    cos = cos_ref[...]
    sin = sin_ref[...]
    half = MLA_ROPE // 2
    down = jnp.dot(u, wd_ref[...], preferred_element_type=F32)

    def norm(z, g):
        ms = jnp.mean(z * z, axis=-1, keepdims=True)
        return (z * lax.rsqrt(ms + RMS_EPS) * g).astype(BF16)

    cq = norm(down[:, 0:MLA_Q_LORA], qn_ref[...])
    ckv = norm(down[:, MLA_Q_LORA:MLA_Q_LORA + MLA_KV_LORA], kvn_ref[...])
    krope_ref[0] = _rope_slab(down[:, MLA_Q_LORA + MLA_KV_LORA:], cos, sin, half).astype(BF16)

    n_nope = MLA_HEADS * MLA_NOPE
    qn = jnp.dot(cq, wuq_ref[:, 0:n_nope], preferred_element_type=F32)
    qnope_ref[0] = (qn * scale).astype(BF16)
    qr = jnp.dot(cq, wuq_ref[:, n_nope:], preferred_element_type=F32)
    qr = jnp.concatenate(
        [_rope_slab(qr[:, j * LANES:(j + 1) * LANES], cos, sin, half) for j in range(qr.shape[1] // LANES)], axis=1)
    qrope_ref[0] = (qr * scale).astype(BF16)
    knope_ref[0] = jnp.dot(ckv, wukv_ref[:, 0:n_nope], preferred_element_type=F32).astype(BF16)
    v_ref[0] = jnp.dot(ckv, wukv_ref[:, n_nope:], preferred_element_type=F32).T.astype(BF16)


def _mla_proj(h, sh, sc, g, wd, qn, kvn, wuq, wukv, cos, sin, b, s):
    t, d = h.shape
    tpb = s // TM
    row = pl.BlockSpec((TM, d), lambda i: (i, 0))
    mod = pl.BlockSpec((1, 1, d), lambda i: (i // tpb, 0, 0))
    tab = pl.BlockSpec((TM, LANES), lambda i: (i, 0))

    def out(width):
        return pl.BlockSpec((1, TM, width), lambda i: (i // tpb, i % tpb, 0))

    def shp(width):
        return jax.ShapeDtypeStruct((b, s, width), BF16)

    n_nope = MLA_HEADS * MLA_NOPE
    n_rope = MLA_HEADS * MLA_ROPE
    n_v = MLA_HEADS * MLA_V
    scale = (MLA_NOPE + MLA_ROPE) ** -0.5
    return pl.pallas_call(
        functools.partial(_mla_proj_body, scale=scale),
        grid=(t // TM,),
        in_specs=[row, mod, mod, _const_spec((1, d)), _const_spec((d, MLA_DOWN_COLS)),
                  _const_spec((1, MLA_Q_LORA)), _const_spec((1, MLA_KV_LORA)),
                  _const_spec((MLA_Q_LORA, n_nope + n_rope)), _const_spec((MLA_KV_LORA, n_nope + n_v)), tab, tab],
        out_specs=[out(n_nope), out(n_rope), out(n_nope),
                   pl.BlockSpec((1, n_v, TM), lambda i: (i // tpb, 0, i % tpb)), out(LANES)],
        out_shape=[shp(n_nope), shp(n_rope), shp(n_nope), jax.ShapeDtypeStruct((b, n_v, s), BF16), shp(LANES)],
        compiler_params=_cparams(("arbitrary",)),
        name="mla_in_proj",
    )(h, sh, sc, g, wd, qn, kvn, wuq, wukv, cos, sin)


def _mla_body(qn_ref, qr_ref, kn_ref, kr_ref, vt_ref, o_ref, *, tq, tk):
    qi = pl.program_id(2)
    qr = qr_ref[0]
    qs = [jnp.concatenate([qn_ref[0, :, j * LANES:(j + 1) * LANES],
                           jnp.where(_head_mask(j), qr, jnp.zeros_like(qr))], axis=1) for j in range(2)]
    krow = lax.broadcasted_iota(jnp.int32, (tk, tq), 0)
    qcol = lax.broadcasted_iota(jnp.int32, (tk, tq), 1)

    def qk(j, kt):
        ks = pl.multiple_of(kt * tk, tk)
        k = jnp.concatenate([kn_ref[0, pl.ds(ks, tk), j * LANES:(j + 1) * LANES], kr_ref[0, pl.ds(ks, tk), :]], axis=1)
        return _qk(k, qs[j])

    def values(j, kt):
        return vt_ref[0, j * LANES:(j + 1) * LANES, pl.ds(pl.multiple_of(kt * tk, tk), tk)]

    def chunk_causal(s):
        return jnp.where(krow // CHUNK <= qcol // CHUNK, s, NEG_LOGIT)

    outs = _attend(2, qi + 1, qk, values, tq, LANES, mask_last=chunk_causal)
    o_ref[0] = jnp.concatenate([o.T for o in outs], axis=1).astype(BF16)


def _mla_attention(qnope, qrope, knope, krope, vt):
    b, s, w = qnope.shape
    tq = tk = ATT_TQ
    pair = 2 * LANES
    qspec = pl.BlockSpec((1, tq, pair), lambda bi, hp, qi: (bi, qi, hp))
    qrspec = pl.BlockSpec((1, tq, LANES), lambda bi, hp, qi: (bi, qi, hp))
    kspec = pl.BlockSpec((1, s, pair), lambda bi, hp, qi: (bi, 0, hp))
    krspec = pl.BlockSpec((1, s, LANES), lambda bi, hp, qi: (bi, 0, 0))
    vspec = pl.BlockSpec((1, pair, s), lambda bi, hp, qi: (bi, hp, 0))
    return pl.pallas_call(
        functools.partial(_mla_body, tq=tq, tk=tk),
        grid=(b, w // pair, s // tq),
        in_specs=[qspec, qrspec, kspec, krspec, vspec],
        out_specs=qspec,
        out_shape=jax.ShapeDtypeStruct((b, s, w), BF16),
        compiler_params=_cparams(("arbitrary", "arbitrary", "arbitrary")),
        name="mla_attention",
    )(qnope, qrope, knope, krope, vt)


def _out_proj_body(*refs, n_in):
    h_ref, gt_ref = refs[0], refs[1]
    xs = refs[2:2 + n_in]
    ws = refs[2 + n_in:2 + 2 * n_in]
    o_ref = refs[2 + 2 * n_in]
    mix = None
    for x_ref, w_ref in zip(xs, ws):
        d = jnp.dot(x_ref[...], w_ref[...], preferred_element_type=F32)
        mix = d if mix is None else mix + d
    o_ref[...] = h_ref[...] + gt_ref[0] * mix


def _out_proj(h, gt, xs, ws, s):
    t, d = h.shape
    tpb = s // TM
    row = pl.BlockSpec((TM, d), lambda i: (i, 0))
    mod = pl.BlockSpec((1, 1, d), lambda i: (i // tpb, 0, 0))
    xspecs = [pl.BlockSpec((TM, x.shape[1]), lambda i: (i, 0)) for x in xs]
    wspecs = [_const_spec(w.shape) for w in ws]
    return pl.pallas_call(
        functools.partial(_out_proj_body, n_in=len(xs)),
        grid=(t // TM,),
        in_specs=[row, mod] + xspecs + wspecs,
        out_specs=row,
        out_shape=jax.ShapeDtypeStruct((t, d), F32),
        compiler_params=_cparams(("arbitrary",)),
        name="mixer_out_proj",
    )(h, gt, *xs, *ws)


def _hyb_weight(w_in, b_f):
    o = 0
    parts = {}
    for name, width in (("fq", 512), ("fk", 512), ("fv", 512), ("ff", FOX_HEADS), ("dq", 512), ("dk", 512),
                        ("dv", 512), ("iq", 256), ("iw", IDX_HEADS), ("ik", HEAD_DIM)):
        parts[name] = w_in[:, o:o + width]
        o += width
    qs = HEAD_DIM ** -0.5
    pad = jnp.zeros((w_in.shape[0], LANES - FOX_HEADS - IDX_HEADS), w_in.dtype)
    w = jnp.concatenate([parts["fq"] * qs, parts["fk"], parts["fv"], parts["dq"] * qs, parts["dk"], parts["dv"],
                         parts["iq"] * qs, parts["ik"], parts["ik"],
                         parts["ff"], parts["iw"] * IDX_HEADS ** -0.5, pad], axis=1)
    bf = jnp.concatenate([b_f, jnp.zeros((LANES - FOX_HEADS,), b_f.dtype)]).reshape(1, LANES)
    return w.astype(BF16), bf.astype(F32)


def _mla_weights(w_down, w_uq, w_ukv):
    kr = w_down[:, MLA_Q_LORA + MLA_KV_LORA:]
    wd = jnp.concatenate([w_down, kr], axis=1)
    uq = w_uq.reshape(MLA_Q_LORA, MLA_HEADS, MLA_NOPE + MLA_ROPE)
    wuq = jnp.concatenate([uq[:, :, :MLA_NOPE].reshape(MLA_Q_LORA, -1), uq[:, :, MLA_NOPE:].reshape(MLA_Q_LORA, -1)],
                          axis=1)
    ukv = w_ukv.reshape(MLA_KV_LORA, MLA_HEADS, MLA_NOPE + MLA_V)
    wukv = jnp.concatenate([ukv[:, :, :MLA_NOPE].reshape(MLA_KV_LORA, -1),
                            ukv[:, :, MLA_NOPE:].reshape(MLA_KV_LORA, -1)], axis=1)
    return wd.astype(BF16), wuq.astype(BF16), wukv.astype(BF16)


def kernel(x, c, positions, ada_w, ada_b, norm_g, ffn_w_gate, ffn_w_up, ffn_w_down, hyb_w_in, fox_b_f, hyb_w_out,
           mla_w_down, mla_q_norm, mla_kv_norm, mla_w_uq, mla_w_ukv, mla_w_out, final_g):
    b, s, d = x.shape
    depth = ada_w.shape[0]
    t = b * s
    assert s % TM == 0 and s % ATT_TQ == 0 and s % DSA_TK == 0 and ffn_w_gate.shape[-1] % FFN_CHUNK == 0

    mod = _modulation(c, ada_w, ada_b).reshape(depth, b, N_MOD, 1, d)
    cos_d, sin_d, cos_m, sin_m = _rope_tables(positions)
    wg = ffn_w_gate.astype(BF16)
    wu = ffn_w_up.astype(BF16)
    wdn = ffn_w_down.astype(BF16)

    h = x.reshape(t, d)
    for i in range(depth):
        sh1, sc1, g1, sh2, sc2, g2, sh3, sc3, g3 = [mod[i, :, j] for j in range(N_MOD)]
        ng = norm_g[i].reshape(3, 1, d)
        h = _ffn(h, sh1, sc1, g1, ng[0], wg[i, 0], wu[i, 0], wdn[i, 0], s)
        j = i // 2
        if i % 2 == 0:
            w_in, bf = _hyb_weight(hyb_w_in[j], fox_b_f[j])
            fq, fk, fvt, dq, dk, dvt, iq, ik, cparts, iw = _hyb_proj(h, sh2, sc2, ng[1], w_in, bf, cos_d, sin_d, b, s)
            out_a = _fox_attention(fq, fk, cparts, fvt)
            out_b = _dsa_attention(dq, dk, dvt, iq, ik, iw)
            w_out = hyb_w_out[j].astype(BF16)
            half = out_a.shape[2]
            h = _out_proj(h, g2, [out_a.reshape(t, half), out_b.reshape(t, half)], [w_out[:half], w_out[half:]], s)
        else:
            wd, wuq, wukv = _mla_weights(mla_w_down[j], mla_w_uq[j], mla_w_ukv[j])
            qn, qr, kn, vt, kr = _mla_proj(h, sh2, sc2, ng[1], wd, mla_q_norm[j].reshape(1, -1),
                                           mla_kv_norm[j].reshape(1, -1), wuq, wukv, cos_m, sin_m, b, s)
            out = _mla_attention(qn, qr, kn, kr, vt)
            h = _out_proj(h, g2, [out.reshape(t, -1)], [mla_w_out[j].astype(BF16)], s)
        last = i == depth - 1
        h = _ffn(h, sh3, sc3, g3, ng[2], wg[i, 1], wu[i, 1], wdn[i, 1], s,
                 final_g=final_g.reshape(1, d) if last else None)
    return h.reshape(b, s, d)
```

```python
import functools
import math

import jax
import jax.numpy as jnp
from jax import lax
from jax.experimental import pallas as pl
from jax.experimental.pallas import tpu as pltpu

F32 = jnp.float32
BF16 = jnp.bfloat16

CHUNK = 64
RMS_EPS = 1e-6
ROPE_THETA = 500000.0
MLA_ROPE_THETA = 10000.0
MACARON_WEIGHT = 0.5
N_MOD = 9
FOX_HEADS = 8
HEAD_DIM = 64
DSA_ROT = 16
IDX_HEADS = 4
DSA_TOPK_MAX = 256
MLA_HEADS = 8
MLA_NOPE = 128
MLA_ROPE = 64
MLA_V = 128
MLA_Q_LORA = 384
MLA_KV_LORA = 256

LANES = 128
VMEM_LIMIT = 56 * 1024 * 1024

LOG2E = math.log2(math.e)
NEG_LOGIT = -1e30
NEG_SCORE = -3e38
IDX_NONE = 1e9

TM = 512
FFN_CHUNK = 256
ATT_TQ = 512
ATT_TK = 512
ATT_HEADS = 4
DSA_TQ = 256
DSA_TK = 512
COUNT_ROWS = 32


def _cparams(sem):
    return pltpu.CompilerParams(dimension_semantics=sem, vmem_limit_bytes=VMEM_LIMIT)


def _const_spec(shape):
    nd = len(shape)
    return pl.BlockSpec(shape, lambda *_: (0,) * nd, pipeline_mode=pl.Buffered(1))


def _silu(x):
    return x * jax.nn.sigmoid(x)


def _modulated_norm(x, g, sc, sh):
    ms = jnp.mean(x * x, axis=-1, keepdims=True)
    y = x * lax.rsqrt(ms + RMS_EPS) * g
    return y * (1.0 + sc) + sh


def _mod_body(c_ref, w_ref, b_ref, o_ref):
    cond = _silu(c_ref[...]).astype(BF16)
    o_ref[0] = jnp.dot(cond, w_ref[0].astype(BF16), preferred_element_type=F32) + b_ref[0]


def _modulation(c, ada_w, ada_b):
    depth, d, n = ada_w.shape
    b = c.shape[0]
    tn = n // N_MOD
    return pl.pallas_call(
        _mod_body,
        grid=(depth, n // tn),
        in_specs=[pl.BlockSpec((b, d), lambda i, j: (0, 0)),
                  pl.BlockSpec((1, d, tn), lambda i, j: (i, 0, j)),
                  pl.BlockSpec((1, 1, tn), lambda i, j: (i, 0, j))],
        out_specs=pl.BlockSpec((1, b, tn), lambda i, j: (i, 0, j)),
        out_shape=jax.ShapeDtypeStruct((depth, b, n), F32),
        compiler_params=_cparams(("arbitrary", "arbitrary")),
        name="adaln_mod",
    )(c, ada_w, ada_b.reshape(depth, 1, n))


def _rope_table_body(pos_ref, invd_ref, sgnd_ref, invm_ref, sgnm_ref, cd_ref, sd_ref, cm_ref, sm_ref):
    pos = pos_ref[...].astype(F32)
    for inv_ref, sgn_ref, c_ref, s_ref in ((invd_ref, sgnd_ref, cd_ref, sd_ref),
                                           (invm_ref, sgnm_ref, cm_ref, sm_ref)):
        ang = pos * inv_ref[...]
        sgn = sgn_ref[...]
        c_ref[...] = jnp.where(sgn != 0.0, jnp.cos(ang), 1.0)
        s_ref[...] = sgn * jnp.sin(ang)


def _lane_pattern(rot, theta):
    half = rot // 2
    inv_freq = jnp.exp(-math.log(theta) * 2.0 * jnp.arange(half, dtype=F32) / rot)
    d = jnp.arange(LANES) % HEAD_DIM
    inv = jnp.where(d < rot, inv_freq[d % half], 0.0).astype(F32)
    sgn = jnp.where(d < half, -1.0, jnp.where(d < rot, 1.0, 0.0)).astype(F32)
    return inv.reshape(1, LANES), sgn.reshape(1, LANES)


def _rope_tables(positions):
    t = positions.size
    invd, sgnd = _lane_pattern(DSA_ROT, ROPE_THETA)
    invm, sgnm = _lane_pattern(MLA_ROPE, MLA_ROPE_THETA)
    tm = 1024
    row = pl.BlockSpec((tm, LANES), lambda i: (i, 0))
    vec = pl.BlockSpec((1, LANES), lambda i: (0, 0))
    tab = jax.ShapeDtypeStruct((t, LANES), F32)
    return pl.pallas_call(
        _rope_table_body,
        grid=(t // tm,),
        in_specs=[pl.BlockSpec((tm, 1), lambda i: (i, 0)), vec, vec, vec, vec],
        out_specs=[row, row, row, row],
        out_shape=[tab, tab, tab, tab],
        compiler_params=_cparams(("arbitrary",)),
        name="rope_tables",
    )(positions.reshape(t, 1), invd, sgnd, invm, sgnm)


def _rope_slab(y, cos, sin, half):
    lane = lax.broadcasted_iota(jnp.int32, (1, LANES), 1) % HEAD_DIM
    first = lane < half
    partner = jnp.where(first, pltpu.roll(y, LANES - half, 1), pltpu.roll(y, half, 1))
    return y * cos + partner * sin


def _ffn_body(h_ref, sh_ref, sc_ref, gt_ref, g_ref, wg_ref, wu_ref, wd_ref, *rest, nf, fc, final, n_mix):
    rest = list(rest)
    x = h_ref[...]
    if n_mix:
        mg_ref = rest.pop(0)
        mix = None
        for x_ref, w_ref in zip(rest[:n_mix], rest[n_mix:2 * n_mix]):
            part = jnp.dot(x_ref[...], w_ref[...], preferred_element_type=F32)
            mix = part if mix is None else mix + part
        x = x + mg_ref[0] * mix
        rest = rest[2 * n_mix:]
    if final:
        fg_ref, o_ref, acc_ref = rest
    else:
        o_ref, acc_ref = rest
    u = _modulated_norm(x, g_ref[...], sc_ref[0], sh_ref[0]).astype(BF16)
    for f in range(nf):
        sl = slice(f * fc, (f + 1) * fc)
        gp = jnp.dot(u, wg_ref[:, sl], preferred_element_type=F32)
        up = jnp.dot(u, wu_ref[:, sl], preferred_element_type=F32)
        a = (_silu(gp) * up).astype(BF16)
        d = jnp.dot(a, wd_ref[sl, :], preferred_element_type=F32)
        if f == 0:
            acc_ref[...] = d
        else:
            acc_ref[...] += d
    y = x + (MACARON_WEIGHT * gt_ref[0]) * acc_ref[...]
    if final:
        ms = jnp.mean(y * y, axis=-1, keepdims=True)
        y = y * lax.rsqrt(ms + RMS_EPS) * fg_ref[...]
    o_ref[...] = y


def _ffn(h, sh, sc, gt, g, wg, wu, wd, s, final_g=None, mix=None):
    t, d = h.shape
    f = wg.shape[1]
    tpb = s // TM
    row = pl.BlockSpec((TM, d), lambda i: (i, 0))
    mod = pl.BlockSpec((1, 1, d), lambda i: (i // tpb, 0, 0))
    in_specs = [row, mod, mod, mod, _const_spec((1, d)),
                _const_spec((d, f)), _const_spec((d, f)), _const_spec((f, d))]
    args = [h, sh, sc, gt, g, wg, wu, wd]
    n_mix = 0
    if mix is not None:
        mgate, xs, ws = mix
        n_mix = len(xs)
        in_specs += [mod] + [pl.BlockSpec((TM, x.shape[1]), lambda i: (i, 0)) for x in xs] + [_const_spec(w.shape) for w in ws]
        args += [mgate, *xs, *ws]
    if final_g is not None:
        in_specs.append(_const_spec((1, d)))
        args.append(final_g)
    return pl.pallas_call(
        functools.partial(_ffn_body, nf=f // FFN_CHUNK, fc=FFN_CHUNK, final=final_g is not None, n_mix=n_mix),
        grid=(t // TM,),
        in_specs=in_specs,
        out_specs=row,
        out_shape=jax.ShapeDtypeStruct((t, d), F32),
        scratch_shapes=[pltpu.VMEM((TM, d), F32)],
        compiler_params=_cparams(("arbitrary",)),
        name="swiglu_half_step",
    )(*args)


HYB_W = 512
HYB_COLS = 6 * HYB_W + 256 + 128 + 128


def _hyb_proj_body(h_ref, sh_ref, sc_ref, g_ref, w_ref, bf_ref, cos_ref, sin_ref,
                   fq_ref, fk_ref, fvt_ref, dq_ref, dk_ref, dvt_ref, iq_ref, ik_ref, cum_ref, iw_ref,
                   carry_ref, *, tpb):
    i = pl.program_id(0)
    x = h_ref[...]
    tm = x.shape[0]
    u = _modulated_norm(x, g_ref[...], sc_ref[0], sh_ref[0]).astype(BF16)
    cos = cos_ref[...]
    sin = sin_ref[...]
    half = DSA_ROT // 2

    def proj(c0, width):
        return jnp.dot(u, w_ref[:, c0:c0 + width], preferred_element_type=F32)

    def roped(y):
        return jnp.concatenate(
            [_rope_slab(y[:, j * LANES:(j + 1) * LANES], cos, sin, half) for j in range(y.shape[1] // LANES)],
            axis=1)

    fq_ref[0] = (proj(0 * HYB_W, HYB_W) * LOG2E).astype(BF16)
    fk_ref[0] = proj(1 * HYB_W, HYB_W).astype(BF16)
    fvt_ref[0] = proj(2 * HYB_W, HYB_W).T.astype(BF16)
    dq_ref[0] = (roped(proj(3 * HYB_W, HYB_W)) * LOG2E).astype(BF16)
    dk_ref[0] = roped(proj(4 * HYB_W, HYB_W)).astype(BF16)
    dvt_ref[0] = proj(5 * HYB_W, HYB_W).T.astype(BF16)
    tail = proj(6 * HYB_W, 512)
    iq_ref[0] = roped(tail[:, 0:256]).astype(BF16)
    ik_ref[0] = roped(tail[:, 256:384]).astype(BF16)
    gates = tail[:, 384:512]
    iw_ref[0] = gates.T[FOX_HEADS:2 * FOX_HEADS, :]

    z = gates + bf_ref[...]
    logf = jnp.minimum(z, 0.0) - jnp.log1p(jnp.exp(-jnp.abs(z)))
    rows = lax.broadcasted_iota(jnp.int32, (tm, LANES), 0)
    c = logf
    k = 1
    while k < tm:
        c = c + jnp.where(rows >= k, pltpu.roll(c, k, 0), 0.0)
        k *= 2

    @pl.when(i % tpb == 0)
    def _():
        carry_ref[...] = jnp.zeros_like(carry_ref)

    c = c + carry_ref[0:1, :]
    carry_ref[...] = jnp.broadcast_to(c[tm - 1:tm, :], carry_ref.shape)
    lane = lax.broadcasted_iota(jnp.int32, (1, LANES), 1)
    c = jnp.where(lane < FOX_HEADS, c * LOG2E, 0.0)
    hi = c.astype(BF16).astype(F32)
    mid = (c - hi).astype(BF16).astype(F32)
    lo = (c - hi - mid).astype(BF16).astype(F32)
    cum_ref[0] = (hi + pltpu.roll(mid, FOX_HEADS, 1) + pltpu.roll(lo, 2 * FOX_HEADS, 1)).astype(BF16)


def _hyb_proj(h, sh, sc, g, w, bf, cos, sin, b, s):
    t, d = h.shape
    tpb = s // TM
    row = pl.BlockSpec((TM, d), lambda i: (i, 0))
    mod = pl.BlockSpec((1, 1, d), lambda i: (i // tpb, 0, 0))
    tab = pl.BlockSpec((TM, LANES), lambda i: (i, 0))

    def out(width):
        return pl.BlockSpec((1, TM, width), lambda i: (i // tpb, i % tpb, 0))

    def shp(width, dt=BF16):
        return jax.ShapeDtypeStruct((b, s, width), dt)

    def out_t(width):
        return pl.BlockSpec((1, width, TM), lambda i: (i // tpb, 0, i % tpb))

    def shp_t(width, dt=BF16):
        return jax.ShapeDtypeStruct((b, width, s), dt)

    return pl.pallas_call(
        functools.partial(_hyb_proj_body, tpb=tpb),
        grid=(t // TM,),
        in_specs=[row, mod, mod, _const_spec((1, d)), _const_spec((d, HYB_COLS)), _const_spec((1, LANES)), tab, tab],
        out_specs=[out(HYB_W), out(HYB_W), out_t(HYB_W), out(HYB_W), out(HYB_W), out_t(HYB_W),
                   out(256), out(LANES), out(LANES), out_t(FOX_HEADS)],
        out_shape=[shp(HYB_W), shp(HYB_W), shp_t(HYB_W), shp(HYB_W), shp(HYB_W), shp_t(HYB_W),
                   shp(256), shp(LANES), shp(LANES), shp_t(FOX_HEADS, F32)],
        scratch_shapes=[pltpu.VMEM((8, LANES), F32)],
        compiler_params=_cparams(("arbitrary",)),
        name="hybrid_in_proj",
    )(h, sh, sc, g, w, bf, cos, sin)


def _attend_scratch(n_heads, tq, tk, n):
    return [pltpu.VMEM((2, n_heads, tk, tq), F32), pltpu.VMEM((2, n_heads, 1, tq), F32),
            pltpu.VMEM((n_heads, 1, tq), F32), pltpu.VMEM((n_heads, 1, tq), F32), pltpu.VMEM((n_heads, n, tq), F32)]


def _attend(n_heads, n_tiles, qk, values, scratch, mask_last=None):
    s_ref, smax_ref, m_ref, l_ref, acc_ref = scratch

    def stage(slot, kt):
        for hd in range(n_heads):
            s = qk(hd, kt)
            s_ref[slot, hd] = s
            smax_ref[slot, hd] = jnp.max(s, axis=0, keepdims=True)

    def step(slot, kt, mask=None):
        for hd in range(n_heads):
            s = s_ref[slot, hd]
            smax = smax_ref[slot, hd]
            if mask is not None:
                s = mask(s)
                smax = jnp.max(s, axis=0, keepdims=True)
            m = m_ref[hd]
            m_new = jnp.maximum(m, smax)
            alpha = jnp.exp2(m - m_new)
            p = jnp.exp2(s - m_new)
            m_ref[hd] = m_new
            l_ref[hd] = alpha * l_ref[hd] + jnp.sum(p, axis=0, keepdims=True)
            acc_ref[hd] = alpha * acc_ref[hd] + jnp.dot(values(hd, kt), p.astype(BF16), preferred_element_type=F32)

    m_ref[...] = jnp.full(m_ref.shape, NEG_LOGIT, F32)
    l_ref[...] = jnp.zeros(l_ref.shape, F32)
    acc_ref[...] = jnp.zeros(acc_ref.shape, F32)
    stage(0, 0)
    n_pairs = (n_tiles - 1) // 2

    def pair(i, _):
        stage(1, 2 * i + 1)
        step(0, 2 * i)
        stage(0, 2 * i + 2)
        step(1, 2 * i + 1)
        return 0

    lax.fori_loop(0, n_pairs, pair, 0)
    odd_tail = n_tiles - 2 * n_pairs == 2

    @pl.when(odd_tail)
    def _():
        stage(1, n_tiles - 1)
        step(0, n_tiles - 2)

    step(jnp.where(odd_tail, 1, 0), n_tiles - 1, mask=mask_last)
    return [acc_ref[hd] / l_ref[hd] for hd in range(n_heads)]


def _head_mask(j):
    lane = lax.broadcasted_iota(jnp.int32, (1, LANES), 1)
    return (lane // HEAD_DIM) == j


def _qk(q, k):
    return lax.dot_general(q, k, (((1,), (1,)), ((), ())), preferred_element_type=F32)


def _pair_out(outs):
    rows = lax.broadcasted_iota(jnp.int32, (LANES, 1), 0)
    return jnp.where(rows < HEAD_DIM, outs[0], outs[1]).T.astype(BF16)


def _fox_body(q_ref, k_ref, c_ref, vt_ref, o_ref, *scratch, tq, tk):
    n_heads = q_ref.shape[2] // HEAD_DIM
    hg = pl.program_id(1)
    qi = pl.program_id(2)
    krow = lax.broadcasted_iota(jnp.int32, (tk, tq), 0)
    qcol = lax.broadcasted_iota(jnp.int32, (tk, tq), 1)
    lane = lax.broadcasted_iota(jnp.int32, (tq, LANES), 1)
    qs = []
    for hd in range(n_heads):
        pick = (lane % FOX_HEADS == hg * n_heads + hd) & (lane < 3 * FOX_HEADS)
        sel = jnp.where(pick, -1.0, 0.0).astype(BF16)
        qfull = q_ref[0, :, (hd // 2) * LANES:(hd // 2 + 1) * LANES]
        qs.append(jnp.concatenate([jnp.where(_head_mask(hd % 2), qfull, jnp.zeros_like(qfull)), sel], axis=1))

    def qk(hd, kt):
        ks = pl.multiple_of(kt * tk, tk)
        lanes = slice((hd // 2) * LANES, (hd // 2 + 1) * LANES)
        k = jnp.concatenate([k_ref[0, pl.ds(ks, tk), lanes], c_ref[0, pl.ds(ks, tk), :]], axis=1)
        return _qk(k, qs[hd])

    def values(hd, kt):
        return vt_ref[0, (hd // 2) * LANES:(hd // 2 + 1) * LANES, pl.ds(pl.multiple_of(kt * tk, tk), tk)]

    def causal(s):
        return jnp.where(krow <= qcol, s, NEG_LOGIT)

    outs = _attend(n_heads, qi + 1, qk, values, scratch, mask_last=causal)
    for hp in range(n_heads // 2):
        o_ref[0, :, hp * LANES:(hp + 1) * LANES] = _pair_out(outs[2 * hp:2 * hp + 2])


def _fox_attention(fq, fk, cparts, fvt):
    b, s, w = fq.shape
    tq = tk = ATT_TQ
    gw = ATT_HEADS * HEAD_DIM
    qspec = pl.BlockSpec((1, tq, gw), lambda bi, hg, qi: (bi, qi, hg))
    kspec = pl.BlockSpec((1, s, gw), lambda bi, hg, qi: (bi, 0, hg))
    cspec = pl.BlockSpec((1, s, LANES), lambda bi, hg, qi: (bi, 0, 0))
    vspec = pl.BlockSpec((1, gw, s), lambda bi, hg, qi: (bi, hg, 0))
    return pl.pallas_call(
        functools.partial(_fox_body, tq=tq, tk=tk),
        grid=(b, w // gw, s // tq),
        in_specs=[qspec, kspec, cspec, vspec],
        out_specs=qspec,
        out_shape=jax.ShapeDtypeStruct((b, s, w), BF16),
        scratch_shapes=_attend_scratch(ATT_HEADS, tq, tk, LANES),
        compiler_params=_cparams(("arbitrary", "arbitrary", "arbitrary")),
        name="fox_attention",
    )(fq, fk, cparts, fvt)


def _key_to_float(u):
    bits = jnp.where(u < 0, u & jnp.int32(0x7FFFFFFF), ~u)
    return pltpu.bitcast(bits, F32)


def _dsa_body(dq_ref, iq_ref, iw_ref, dk_ref, dvt_ref, ik_ref, o_ref, sc_ref, tie_ref, *scratch, tq, tk, top_k):
    qi = pl.program_id(1)
    idx_bits = max(1, (sc_ref.shape[0] - 1).bit_length())
    n_tiles = ((qi + 1) * tq + tk - 1) // tk
    qpos = lax.broadcasted_iota(jnp.int32, (1, tq), 1) + qi * tq
    limit = (qpos // CHUNK + 1) * CHUNK
    krow = lax.broadcasted_iota(jnp.int32, (tk, tq), 0)
    kf = jnp.float32(top_k)

    iqf = iq_ref[0]
    iw = iw_ref[0]
    iqs = [jnp.where(_head_mask(h % 2), iqf[:, (h // 2) * LANES:(h // 2 + 1) * LANES], jnp.zeros((tq, LANES), BF16))
           for h in range(IDX_HEADS)]
    ws = [iw[h:h + 1, :] for h in range(IDX_HEADS)]

    def score_tile(kt, _):
        ks = pl.multiple_of(kt * tk, tk)
        ik = ik_ref[0, pl.ds(ks, tk), :]
        sc = None
        for h in range(IDX_HEADS):
            term = jnp.maximum(_qk(ik, iqs[h]), 0.0) * ws[h]
            sc = term if sc is None else sc + term
        sc_ref[pl.ds(ks, tk), :] = jnp.where(krow + ks < limit, sc, NEG_SCORE)
        return 0

    lax.fori_loop(0, n_tiles, score_tile, 0)

    def count(pred):
        def body(kt, cnt):
            ks = pl.multiple_of(kt * tk, tk)
            c = jnp.where(pred(ks), 1.0, 0.0)
            return cnt + jnp.sum(c.reshape(tk // COUNT_ROWS, COUNT_ROWS, tq), axis=0)
        cnt = lax.fori_loop(0, n_tiles, body, jnp.zeros((COUNT_ROWS, tq), F32))
        return jnp.sum(cnt, axis=0, keepdims=True)

    def bit_step(it, u):
        cand = u | lax.shift_left(jnp.int32(1), 31 - it)
        thr = _key_to_float(cand)
        cnt = count(lambda ks: sc_ref[pl.ds(ks, tk), :] >= thr)
        return jnp.where(cnt >= kf, cand, u)

    u = lax.fori_loop(0, 32, bit_step, jnp.zeros((1, tq), jnp.int32))
    thr = _key_to_float(u)

    n_gt = count(lambda ks: sc_ref[pl.ds(ks, tk), :] > thr)
    n_ge = count(lambda ks: sc_ref[pl.ds(ks, tk), :] >= thr)
    need = kf - n_gt

    def tie_tile(kt, _):
        ks = pl.multiple_of(kt * tk, tk)
        tie_ref[pl.ds(ks, tk), :] = jnp.where(sc_ref[pl.ds(ks, tk), :] == thr, (krow + ks).astype(F32), IDX_NONE)
        return 0

    lax.fori_loop(0, n_tiles, tie_tile, 0)
    surplus = jnp.max(n_ge - n_gt - need)

    def tie_search():
        def step(it, x):
            cand = x + lax.shift_left(jnp.int32(1), idx_bits - 1 - it).astype(F32)
            cnt = count(lambda ks: tie_ref[pl.ds(ks, tk), :] < cand)
            return jnp.where(cnt < need, cand, x)
        return lax.fori_loop(0, idx_bits, step, jnp.zeros((1, tq), F32))

    last_tie = lax.cond(surplus > 0.0, tie_search, lambda: jnp.full((1, tq), IDX_NONE / 2, F32))

    def bias_tile(kt, _):
        ks = pl.multiple_of(kt * tk, tk)
        keep = ((sc_ref[pl.ds(ks, tk), :] > thr) | (tie_ref[pl.ds(ks, tk), :] <= last_tie)) & (krow + ks < limit)
        sc_ref[pl.ds(ks, tk), :] = jnp.where(keep, 0.0, NEG_LOGIT)
        return 0

    lax.fori_loop(0, n_tiles, bias_tile, 0)

    n_heads = dq_ref.shape[2] // HEAD_DIM
    qs = []
    for hd in range(n_heads):
        qfull = dq_ref[0, :, (hd // 2) * LANES:(hd // 2 + 1) * LANES]
        qs.append(jnp.where(_head_mask(hd % 2), qfull, jnp.zeros_like(qfull)))

    def qk(hd, kt):
        ks = pl.multiple_of(kt * tk, tk)
        lanes = slice((hd // 2) * LANES, (hd // 2 + 1) * LANES)
        return _qk(dk_ref[0, pl.ds(ks, tk), lanes], qs[hd]) + sc_ref[pl.ds(ks, tk), :]

    def values(hd, kt):
        return dvt_ref[0, (hd // 2) * LANES:(hd // 2 + 1) * LANES, pl.ds(pl.multiple_of(kt * tk, tk), tk)]

    outs = _attend(n_heads, n_tiles, qk, values, scratch)
    for hp in range(n_heads // 2):
        o_ref[0, :, hp * LANES:(hp + 1) * LANES] = _pair_out(outs[2 * hp:2 * hp + 2])


def _dsa_attention(dq, dk, dvt, iq, ik, iw):
    b, s, w = dq.shape
    tq, tk = DSA_TQ, DSA_TK
    top_k = min(DSA_TOPK_MAX, s // 4)

    def qspec(width):
        return pl.BlockSpec((1, tq, width), lambda bi, qi: (bi, qi, 0))

    def kspec(width):
        return pl.BlockSpec((1, s, width), lambda bi, qi: (bi, 0, 0))

    return pl.pallas_call(
        functools.partial(_dsa_body, tq=tq, tk=tk, top_k=top_k),
        grid=(b, s // tq),
        in_specs=[qspec(w), qspec(iq.shape[2]), pl.BlockSpec((1, iw.shape[1], tq), lambda bi, qi: (bi, 0, qi)),
                  kspec(w), pl.BlockSpec((1, w, s), lambda bi, qi: (bi, 0, 0)), kspec(LANES)],
        out_specs=qspec(w),
        out_shape=jax.ShapeDtypeStruct((b, s, w), BF16),
        scratch_shapes=[pltpu.VMEM((s, tq), F32), pltpu.VMEM((s, tq), F32)] + _attend_scratch(w // HEAD_DIM, tq, tk, LANES),
        compiler_params=_cparams(("arbitrary", "arbitrary")),
        name="dsa_attention",
    )(dq, iq, iw, dk, dvt, ik)


MLA_DOWN_COLS = MLA_Q_LORA + MLA_KV_LORA + 2 * MLA_ROPE


def _mla_proj_body(h_ref, sh_ref, sc_ref, g_ref, wd_ref, qn_ref, kvn_ref, wuq_ref, wukv_ref, cos_ref, sin_ref,
                   qnope_ref, qrope_ref, knope_ref, v_ref, krope_ref, *, scale):
    x = h_ref[...]
    u = _modulated_norm(x, g_ref[...], sc_ref[0], sh_ref[0]).astype(BF16)
    cos = cos_ref[...]
    sin = sin_ref[...]
    half = MLA_ROPE // 2
    down = jnp.dot(u, wd_ref[...], preferred_element_type=F32)

    def norm(z, g):
        ms = jnp.mean(z * z, axis=-1, keepdims=True)
        return (z * lax.rsqrt(ms + RMS_EPS) * g).astype(BF16)

    cq = norm(down[:, 0:MLA_Q_LORA], qn_ref[...])
    ckv = norm(down[:, MLA_Q_LORA:MLA_Q_LORA + MLA_KV_LORA], kvn_ref[...])
    krope_ref[0] = _rope_slab(down[:, MLA_Q_LORA + MLA_KV_LORA:], cos, sin, half).astype(BF16)

    n_nope = MLA_HEADS * MLA_NOPE
    qn = jnp.dot(cq, wuq_ref[:, 0:n_nope], preferred_element_type=F32)
    qnope_ref[0] = (qn * scale).astype(BF16)
    qr = jnp.dot(cq, wuq_ref[:, n_nope:], preferred_element_type=F32)
    qr = jnp.concatenate(
        [_rope_slab(qr[:, j * LANES:(j + 1) * LANES], cos, sin, half) for j in range(qr.shape[1] // LANES)], axis=1)
    qrope_ref[0] = (qr * scale).astype(BF16)
    knope_ref[0] = jnp.dot(ckv, wukv_ref[:, 0:n_nope], preferred_element_type=F32).astype(BF16)
    v_ref[0] = jnp.dot(ckv, wukv_ref[:, n_nope:], preferred_element_type=F32).T.astype(BF16)


def _mla_proj(h, sh, sc, g, wd, qn, kvn, wuq, wukv, cos, sin, b, s):
    t, d = h.shape
    tpb = s // TM
    row = pl.BlockSpec((TM, d), lambda i: (i, 0))
    mod = pl.BlockSpec((1, 1, d), lambda i: (i // tpb, 0, 0))
    tab = pl.BlockSpec((TM, LANES), lambda i: (i, 0))

    def out(width):
        return pl.BlockSpec((1, TM, width), lambda i: (i // tpb, i % tpb, 0))

    def shp(width):
        return jax.ShapeDtypeStruct((b, s, width), BF16)

    n_nope = MLA_HEADS * MLA_NOPE
    n_rope = MLA_HEADS * MLA_ROPE
    n_v = MLA_HEADS * MLA_V
    scale = (MLA_NOPE + MLA_ROPE) ** -0.5 * LOG2E
    return pl.pallas_call(
        functools.partial(_mla_proj_body, scale=scale),
        grid=(t // TM,),
        in_specs=[row, mod, mod, _const_spec((1, d)), _const_spec((d, MLA_DOWN_COLS)),
                  _const_spec((1, MLA_Q_LORA)), _const_spec((1, MLA_KV_LORA)),
                  _const_spec((MLA_Q_LORA, n_nope + n_rope)), _const_spec((MLA_KV_LORA, n_nope + n_v)), tab, tab],
        out_specs=[out(n_nope), out(n_rope), out(n_nope),
                   pl.BlockSpec((1, n_v, TM), lambda i: (i // tpb, 0, i % tpb)), out(LANES)],
        out_shape=[shp(n_nope), shp(n_rope), shp(n_nope), jax.ShapeDtypeStruct((b, n_v, s), BF16), shp(LANES)],
        compiler_params=_cparams(("arbitrary",)),
        name="mla_in_proj",
    )(h, sh, sc, g, wd, qn, kvn, wuq, wukv, cos, sin)


def _mla_body(qn_ref, qr_ref, kn_ref, kr_ref, vt_ref, o_ref, *scratch, tq, tk):
    n_heads = qn_ref.shape[2] // MLA_NOPE
    qi = pl.program_id(2)
    qs = []
    for hd in range(n_heads):
        qr = qr_ref[0, :, (hd // 2) * LANES:(hd // 2 + 1) * LANES]
        qs.append(jnp.concatenate([qn_ref[0, :, hd * LANES:(hd + 1) * LANES],
                                   jnp.where(_head_mask(hd % 2), qr, jnp.zeros_like(qr))], axis=1))
    krow = lax.broadcasted_iota(jnp.int32, (tk, tq), 0)
    qcol = lax.broadcasted_iota(jnp.int32, (tk, tq), 1)

    def qk(hd, kt):
        ks = pl.multiple_of(kt * tk, tk)
        k = jnp.concatenate([kn_ref[0, pl.ds(ks, tk), hd * LANES:(hd + 1) * LANES], kr_ref[0, pl.ds(ks, tk), :]],
                            axis=1)
        return _qk(k, qs[hd])

    def values(hd, kt):
        return vt_ref[0, hd * LANES:(hd + 1) * LANES, pl.ds(pl.multiple_of(kt * tk, tk), tk)]

    def chunk_causal(s):
        return jnp.where(krow // CHUNK <= qcol // CHUNK, s, NEG_LOGIT)

    outs = _attend(n_heads, qi + 1, qk, values, scratch, mask_last=chunk_causal)
    for hd in range(n_heads):
        o_ref[0, :, hd * LANES:(hd + 1) * LANES] = outs[hd].T.astype(BF16)


def _mla_attention(qnope, qrope, knope, krope, vt):
    b, s, w = qnope.shape
    tq = tk = ATT_TQ
    gw = ATT_HEADS * MLA_NOPE
    gr = ATT_HEADS * MLA_ROPE
    qspec = pl.BlockSpec((1, tq, gw), lambda bi, hg, qi: (bi, qi, hg))
    qrspec = pl.BlockSpec((1, tq, gr), lambda bi, hg, qi: (bi, qi, hg))
    kspec = pl.BlockSpec((1, s, gw), lambda bi, hg, qi: (bi, 0, hg))
    krspec = pl.BlockSpec((1, s, LANES), lambda bi, hg, qi: (bi, 0, 0))
    vspec = pl.BlockSpec((1, gw, s), lambda bi, hg, qi: (bi, hg, 0))
    return pl.pallas_call(
        functools.partial(_mla_body, tq=tq, tk=tk),
        grid=(b, w // gw, s // tq),
        in_specs=[qspec, qrspec, kspec, krspec, vspec],
        out_specs=qspec,
        out_shape=jax.ShapeDtypeStruct((b, s, w), BF16),
        scratch_shapes=_attend_scratch(ATT_HEADS, tq, tk, LANES),
        compiler_params=_cparams(("arbitrary", "arbitrary", "arbitrary")),
        name="mla_attention",
    )(qnope, qrope, knope, krope, vt)


def _hyb_weight(w_in, b_f):
    o = 0
    parts = {}
    for name, width in (("fq", 512), ("fk", 512), ("fv", 512), ("ff", FOX_HEADS), ("dq", 512), ("dk", 512),
                        ("dv", 512), ("iq", 256), ("iw", IDX_HEADS), ("ik", HEAD_DIM)):
        parts[name] = w_in[:, o:o + width]
        o += width
    qs = HEAD_DIM ** -0.5
    pad = jnp.zeros((w_in.shape[0], LANES - FOX_HEADS - IDX_HEADS), w_in.dtype)
    w = jnp.concatenate([parts["fq"] * qs, parts["fk"], parts["fv"], parts["dq"] * qs, parts["dk"], parts["dv"],
                         parts["iq"] * qs, parts["ik"], parts["ik"],
                         parts["ff"], parts["iw"] * IDX_HEADS ** -0.5, pad], axis=1)
    bf = jnp.concatenate([b_f, jnp.zeros((LANES - FOX_HEADS,), b_f.dtype)]).reshape(1, LANES)
    return w.astype(BF16), bf.astype(F32)


def _mla_weights(w_down, w_uq, w_ukv):
    kr = w_down[:, MLA_Q_LORA + MLA_KV_LORA:]
    wd = jnp.concatenate([w_down, kr], axis=1)
    uq = w_uq.reshape(MLA_Q_LORA, MLA_HEADS, MLA_NOPE + MLA_ROPE)
    wuq = jnp.concatenate([uq[:, :, :MLA_NOPE].reshape(MLA_Q_LORA, -1), uq[:, :, MLA_NOPE:].reshape(MLA_Q_LORA, -1)],
                          axis=1)
    ukv = w_ukv.reshape(MLA_KV_LORA, MLA_HEADS, MLA_NOPE + MLA_V)
    wukv = jnp.concatenate([ukv[:, :, :MLA_NOPE].reshape(MLA_KV_LORA, -1),
                            ukv[:, :, MLA_NOPE:].reshape(MLA_KV_LORA, -1)], axis=1)
    return wd.astype(BF16), wuq.astype(BF16), wukv.astype(BF16)


def kernel(x, c, positions, ada_w, ada_b, norm_g, ffn_w_gate, ffn_w_up, ffn_w_down, hyb_w_in, fox_b_f, hyb_w_out,
           mla_w_down, mla_q_norm, mla_kv_norm, mla_w_uq, mla_w_ukv, mla_w_out, final_g):
    b, s, d = x.shape
    depth = ada_w.shape[0]
    t = b * s
    assert s % TM == 0 and s % ATT_TQ == 0 and s % DSA_TK == 0 and ffn_w_gate.shape[-1] % FFN_CHUNK == 0

    mod = _modulation(c, ada_w, ada_b).reshape(depth, b, N_MOD, 1, d)
    cos_d, sin_d, cos_m, sin_m = _rope_tables(positions)
    wg = ffn_w_gate.astype(BF16)
    wu = ffn_w_up.astype(BF16)
    wdn = ffn_w_down.astype(BF16)

    h = x.reshape(t, d)
    for i in range(depth):
        sh1, sc1, g1, sh2, sc2, g2, sh3, sc3, g3 = [mod[i, :, j] for j in range(N_MOD)]
        ng = norm_g[i].reshape(3, 1, d)
        h = _ffn(h, sh1, sc1, g1, ng[0], wg[i, 0], wu[i, 0], wdn[i, 0], s)
        j = i // 2
        if i % 2 == 0:
            w_in, bf = _hyb_weight(hyb_w_in[j], fox_b_f[j])
            fq, fk, fvt, dq, dk, dvt, iq, ik, cparts, iw = _hyb_proj(h, sh2, sc2, ng[1], w_in, bf, cos_d, sin_d, b, s)
            out_a = _fox_attention(fq, fk, cparts, fvt)
            out_b = _dsa_attention(dq, dk, dvt, iq, ik, iw)
            w_out = hyb_w_out[j].astype(BF16)
            half = out_a.shape[2]
            mix = (g2, [out_a.reshape(t, half), out_b.reshape(t, half)], [w_out[:half], w_out[half:]])
        else:
            wd, wuq, wukv = _mla_weights(mla_w_down[j], mla_w_uq[j], mla_w_ukv[j])
            qn, qr, kn, vt, kr = _mla_proj(h, sh2, sc2, ng[1], wd, mla_q_norm[j].reshape(1, -1),
                                           mla_kv_norm[j].reshape(1, -1), wuq, wukv, cos_m, sin_m, b, s)
            out = _mla_attention(qn, qr, kn, kr, vt)
            mix = (g2, [out.reshape(t, -1)], [mla_w_out[j].astype(BF16)])
        last = i == depth - 1
        h = _ffn(h, sh3, sc3, g3, ng[2], wg[i, 1], wu[i, 1], wdn[i, 1], s,
                 final_g=final_g.reshape(1, d) if last else None, mix=mix)
    return h.reshape(b, s, d)
```

```python
import functools
import math

import jax
import jax.numpy as jnp
from jax import lax
from jax.experimental import pallas as pl
from jax.experimental.pallas import tpu as pltpu

F32 = jnp.float32
BF16 = jnp.bfloat16

CHUNK = 64
RMS_EPS = 1e-6
ROPE_THETA = 500000.0
MLA_ROPE_THETA = 10000.0
MACARON_WEIGHT = 0.5
N_MOD = 9
FOX_HEADS = 8
HEAD_DIM = 64
DSA_ROT = 16
IDX_HEADS = 4
DSA_TOPK_MAX = 256
MLA_HEADS = 8
MLA_NOPE = 128
MLA_ROPE = 64
MLA_V = 128
MLA_Q_LORA = 384
MLA_KV_LORA = 256

LANES = 128
VMEM_LIMIT = 56 * 1024 * 1024

LOG2E = math.log2(math.e)
NEG_LOGIT = -1e30
NEG_SCORE = -3e38
IDX_NONE = 1e9

TM = 512
FFN_CHUNK = 256
ATT_TQ = 512
ATT_TK = 512
ATT_HEADS = 4
DSA_TQ = 256
DSA_TK = 512
COUNT_ROWS = 32


def _cparams(sem):
    return pltpu.CompilerParams(dimension_semantics=sem, vmem_limit_bytes=VMEM_LIMIT)


def _const_spec(shape):
    nd = len(shape)
    return pl.BlockSpec(shape, lambda *_: (0,) * nd, pipeline_mode=pl.Buffered(1))


def _silu(x):
    return x * jax.nn.sigmoid(x)


def _modulated_norm(x, g, sc, sh):
    ms = jnp.mean(x * x, axis=-1, keepdims=True)
    y = x * lax.rsqrt(ms + RMS_EPS) * g
    return y * (1.0 + sc) + sh


def _mod_body(c_ref, w_ref, b_ref, o_ref):
    cond = _silu(c_ref[...]).astype(BF16)
    o_ref[0] = jnp.dot(cond, w_ref[0].astype(BF16), preferred_element_type=F32) + b_ref[0]


def _modulation(c, ada_w, ada_b):
    depth, d, n = ada_w.shape
    b = c.shape[0]
    tn = n // N_MOD
    return pl.pallas_call(
        _mod_body,
        grid=(depth, n // tn),
        in_specs=[pl.BlockSpec((b, d), lambda i, j: (0, 0)),
                  pl.BlockSpec((1, d, tn), lambda i, j: (i, 0, j)),
                  pl.BlockSpec((1, 1, tn), lambda i, j: (i, 0, j))],
        out_specs=pl.BlockSpec((1, b, tn), lambda i, j: (i, 0, j)),
        out_shape=jax.ShapeDtypeStruct((depth, b, n), F32),
        compiler_params=_cparams(("arbitrary", "arbitrary")),
        name="adaln_mod",
    )(c, ada_w, ada_b.reshape(depth, 1, n))


def _rope_table_body(pos_ref, invd_ref, sgnd_ref, invm_ref, sgnm_ref, cd_ref, sd_ref, cm_ref, sm_ref):
    pos = pos_ref[...].astype(F32)
    for inv_ref, sgn_ref, c_ref, s_ref in ((invd_ref, sgnd_ref, cd_ref, sd_ref),
                                           (invm_ref, sgnm_ref, cm_ref, sm_ref)):
        ang = pos * inv_ref[...]
        sgn = sgn_ref[...]
        c_ref[...] = jnp.where(sgn != 0.0, jnp.cos(ang), 1.0)
        s_ref[...] = sgn * jnp.sin(ang)


def _lane_pattern(rot, theta):
    half = rot // 2
    inv_freq = jnp.exp(-math.log(theta) * 2.0 * jnp.arange(half, dtype=F32) / rot)
    d = jnp.arange(LANES) % HEAD_DIM
    inv = jnp.where(d < rot, inv_freq[d % half], 0.0).astype(F32)
    sgn = jnp.where(d < half, -1.0, jnp.where(d < rot, 1.0, 0.0)).astype(F32)
    return inv.reshape(1, LANES), sgn.reshape(1, LANES)


def _rope_tables(positions):
    t = positions.size
    invd, sgnd = _lane_pattern(DSA_ROT, ROPE_THETA)
    invm, sgnm = _lane_pattern(MLA_ROPE, MLA_ROPE_THETA)
    tm = 1024
    row = pl.BlockSpec((tm, LANES), lambda i: (i, 0))
    vec = pl.BlockSpec((1, LANES), lambda i: (0, 0))
    tab = jax.ShapeDtypeStruct((t, LANES), F32)
    return pl.pallas_call(
        _rope_table_body,
        grid=(t // tm,),
        in_specs=[pl.BlockSpec((tm, 1), lambda i: (i, 0)), vec, vec, vec, vec],
        out_specs=[row, row, row, row],
        out_shape=[tab, tab, tab, tab],
        compiler_params=_cparams(("arbitrary",)),
        name="rope_tables",
    )(positions.reshape(t, 1), invd, sgnd, invm, sgnm)


def _rope_slab(y, cos, sin, half):
    lane = lax.broadcasted_iota(jnp.int32, (1, LANES), 1) % HEAD_DIM
    first = lane < half
    partner = jnp.where(first, pltpu.roll(y, LANES - half, 1), pltpu.roll(y, half, 1))
    return y * cos + partner * sin


def _ffn_body(h_ref, sh_ref, sc_ref, gt_ref, g_ref, wg_ref, wu_ref, wd_ref, *rest, nf, fc, final, n_mix):
    rest = list(rest)
    x = h_ref[...]
    if n_mix:
        mg_ref = rest.pop(0)
        mix = None
        for x_ref, w_ref in zip(rest[:n_mix], rest[n_mix:2 * n_mix]):
            part = jnp.dot(x_ref[...], w_ref[...], preferred_element_type=F32)
            mix = part if mix is None else mix + part
        x = x + mg_ref[0] * mix
        rest = rest[2 * n_mix:]
    if final:
        fg_ref, o_ref, acc_ref = rest
    else:
        o_ref, acc_ref = rest
    u = _modulated_norm(x, g_ref[...], sc_ref[0], sh_ref[0]).astype(BF16)
    for f in range(nf):
        sl = slice(f * fc, (f + 1) * fc)
        gp = jnp.dot(u, wg_ref[:, sl], preferred_element_type=F32)
        up = jnp.dot(u, wu_ref[:, sl], preferred_element_type=F32)
        a = (_silu(gp) * up).astype(BF16)
        d = jnp.dot(a, wd_ref[sl, :], preferred_element_type=F32)
        if f == 0:
            acc_ref[...] = d
        else:
            acc_ref[...] += d
    y = x + (MACARON_WEIGHT * gt_ref[0]) * acc_ref[...]
    if final:
        ms = jnp.mean(y * y, axis=-1, keepdims=True)
        y = y * lax.rsqrt(ms + RMS_EPS) * fg_ref[...]
    o_ref[...] = y


def _ffn(h, sh, sc, gt, g, wg, wu, wd, s, final_g=None, mix=None):
    t, d = h.shape
    f = wg.shape[1]
    tpb = s // TM
    row = pl.BlockSpec((TM, d), lambda i: (i, 0))
    mod = pl.BlockSpec((1, 1, d), lambda i: (i // tpb, 0, 0))
    in_specs = [row, mod, mod, mod, _const_spec((1, d)),
                _const_spec((d, f)), _const_spec((d, f)), _const_spec((f, d))]
    args = [h, sh, sc, gt, g, wg, wu, wd]
    n_mix = 0
    if mix is not None:
        mgate, xs, ws = mix
        n_mix = len(xs)
        in_specs += [mod] + [pl.BlockSpec((TM, x.shape[1]), lambda i: (i, 0)) for x in xs] + [_const_spec(w.shape) for w in ws]
        args += [mgate, *xs, *ws]
    if final_g is not None:
        in_specs.append(_const_spec((1, d)))
        args.append(final_g)
    return pl.pallas_call(
        functools.partial(_ffn_body, nf=f // FFN_CHUNK, fc=FFN_CHUNK, final=final_g is not None, n_mix=n_mix),
        grid=(t // TM,),
        in_specs=in_specs,
        out_specs=row,
        out_shape=jax.ShapeDtypeStruct((t, d), F32),
        scratch_shapes=[pltpu.VMEM((TM, d), F32)],
        compiler_params=_cparams(("arbitrary",)),
        name="swiglu_half_step",
    )(*args)


HYB_W = 512
HYB_COLS = 6 * HYB_W + 256 + 128 + 128


def _hyb_proj_body(h_ref, sh_ref, sc_ref, g_ref, w_ref, bf_ref, cos_ref, sin_ref,
                   fq_ref, fk_ref, fvt_ref, dq_ref, dk_ref, dvt_ref, iq_ref, ik_ref, cum_ref, iw_ref,
                   carry_ref, *, tpb):
    i = pl.program_id(0)
    x = h_ref[...]
    tm = x.shape[0]
    u = _modulated_norm(x, g_ref[...], sc_ref[0], sh_ref[0]).astype(BF16)
    cos = cos_ref[...]
    sin = sin_ref[...]
    half = DSA_ROT // 2

    def proj(c0, width):
        return jnp.dot(u, w_ref[:, c0:c0 + width], preferred_element_type=F32)

    def roped(y):
        return jnp.concatenate(
            [_rope_slab(y[:, j * LANES:(j + 1) * LANES], cos, sin, half) for j in range(y.shape[1] // LANES)],
            axis=1)

    fq_ref[0] = (proj(0 * HYB_W, HYB_W) * LOG2E).astype(BF16)
    fk_ref[0] = proj(1 * HYB_W, HYB_W).astype(BF16)
    fvt_ref[0] = proj(2 * HYB_W, HYB_W).T.astype(BF16)
    dq_ref[0] = (roped(proj(3 * HYB_W, HYB_W)) * LOG2E).astype(BF16)
    dk_ref[0] = roped(proj(4 * HYB_W, HYB_W)).astype(BF16)
    dvt_ref[0] = proj(5 * HYB_W, HYB_W).T.astype(BF16)
    tail = proj(6 * HYB_W, 512)
    iq_ref[0] = roped(tail[:, 0:256]).astype(BF16)
    ik_ref[0] = roped(tail[:, 256:384]).astype(BF16)
    gates = tail[:, 384:512]
    iw_ref[0] = gates.T[FOX_HEADS:2 * FOX_HEADS, :]

    z = gates + bf_ref[...]
    logf = jnp.minimum(z, 0.0) - jnp.log1p(jnp.exp(-jnp.abs(z)))
    rows = lax.broadcasted_iota(jnp.int32, (tm, LANES), 0)
    c = logf
    k = 1
    while k < tm:
        c = c + jnp.where(rows >= k, pltpu.roll(c, k, 0), 0.0)
        k *= 2

    @pl.when(i % tpb == 0)
    def _():
        carry_ref[...] = jnp.zeros_like(carry_ref)

    c = c + carry_ref[0:1, :]
    carry_ref[...] = jnp.broadcast_to(c[tm - 1:tm, :], carry_ref.shape)
    lane = lax.broadcasted_iota(jnp.int32, (1, LANES), 1)
    c = jnp.where(lane < FOX_HEADS, c * LOG2E, 0.0)
    hi = c.astype(BF16).astype(F32)
    mid = (c - hi).astype(BF16).astype(F32)
    lo = (c - hi - mid).astype(BF16).astype(F32)
    cum_ref[0] = (hi + pltpu.roll(mid, FOX_HEADS, 1) + pltpu.roll(lo, 2 * FOX_HEADS, 1)).astype(BF16)


def _hyb_proj(h, sh, sc, g, w, bf, cos, sin, b, s):
    t, d = h.shape
    tpb = s // TM
    row = pl.BlockSpec((TM, d), lambda i: (i, 0))
    mod = pl.BlockSpec((1, 1, d), lambda i: (i // tpb, 0, 0))
    tab = pl.BlockSpec((TM, LANES), lambda i: (i, 0))

    def out(width):
        return pl.BlockSpec((1, TM, width), lambda i: (i // tpb, i % tpb, 0))

    def shp(width, dt=BF16):
        return jax.ShapeDtypeStruct((b, s, width), dt)

    def out_t(width):
        return pl.BlockSpec((1, width, TM), lambda i: (i // tpb, 0, i % tpb))

    def shp_t(width, dt=BF16):
        return jax.ShapeDtypeStruct((b, width, s), dt)

    return pl.pallas_call(
        functools.partial(_hyb_proj_body, tpb=tpb),
        grid=(t // TM,),
        in_specs=[row, mod, mod, _const_spec((1, d)), _const_spec((d, HYB_COLS)), _const_spec((1, LANES)), tab, tab],
        out_specs=[out(HYB_W), out(HYB_W), out_t(HYB_W), out(HYB_W), out(HYB_W), out_t(HYB_W),
                   out(256), out(LANES), out(LANES), out_t(FOX_HEADS)],
        out_shape=[shp(HYB_W), shp(HYB_W), shp_t(HYB_W), shp(HYB_W), shp(HYB_W), shp_t(HYB_W),
                   shp(256), shp(LANES), shp(LANES), shp_t(FOX_HEADS, F32)],
        scratch_shapes=[pltpu.VMEM((8, LANES), F32)],
        compiler_params=_cparams(("arbitrary",)),
        name="hybrid_in_proj",
    )(h, sh, sc, g, w, bf, cos, sin)


def _attend_scratch(n_heads, tq, tk, n):
    return [pltpu.VMEM((2, n_heads, tk, tq), F32), pltpu.VMEM((2, n_heads, 1, tq), F32),
            pltpu.VMEM((n_heads, 1, tq), F32), pltpu.VMEM((n_heads, 1, tq), F32), pltpu.VMEM((n_heads, n, tq), F32)]


def _attend(n_heads, n_tiles, qk, values, scratch, mask_last=None):
    s_ref, smax_ref, m_ref, l_ref, acc_ref = scratch

    def stage(slot, kt):
        for hd in range(n_heads):
            s = qk(hd, kt)
            s_ref[slot, hd] = s
            smax_ref[slot, hd] = jnp.max(s, axis=0, keepdims=True)

    def step(slot, kt, mask=None):
        for hd in range(n_heads):
            s = s_ref[slot, hd]
            smax = smax_ref[slot, hd]
            if mask is not None:
                s = mask(s)
                smax = jnp.max(s, axis=0, keepdims=True)
            m = m_ref[hd]
            m_new = jnp.maximum(m, smax)
            alpha = jnp.exp2(m - m_new)
            p = jnp.exp2(s - m_new)
            m_ref[hd] = m_new
            l_ref[hd] = alpha * l_ref[hd] + jnp.sum(p, axis=0, keepdims=True)
            acc_ref[hd] = alpha * acc_ref[hd] + jnp.dot(values(hd, kt), p.astype(BF16), preferred_element_type=F32)

    m_ref[...] = jnp.full(m_ref.shape, NEG_LOGIT, F32)
    l_ref[...] = jnp.zeros(l_ref.shape, F32)
    acc_ref[...] = jnp.zeros(acc_ref.shape, F32)
    stage(0, 0)
    n_pairs = (n_tiles - 1) // 2

    def pair(i, _):
        stage(1, 2 * i + 1)
        step(0, 2 * i)
        stage(0, 2 * i + 2)
        step(1, 2 * i + 1)
        return 0

    lax.fori_loop(0, n_pairs, pair, 0)
    odd_tail = n_tiles - 2 * n_pairs == 2

    @pl.when(odd_tail)
    def _():
        stage(1, n_tiles - 1)
        step(0, n_tiles - 2)

    step(jnp.where(odd_tail, 1, 0), n_tiles - 1, mask=mask_last)
    return [acc_ref[hd] / l_ref[hd] for hd in range(n_heads)]


def _head_mask(j):
    lane = lax.broadcasted_iota(jnp.int32, (1, LANES), 1)
    return (lane // HEAD_DIM) == j


def _qk(q, k):
    return lax.dot_general(q, k, (((1,), (1,)), ((), ())), preferred_element_type=F32)


def _pair_out(outs):
    rows = lax.broadcasted_iota(jnp.int32, (LANES, 1), 0)
    return jnp.where(rows < HEAD_DIM, outs[0], outs[1]).T.astype(BF16)


def _fox_body(q_ref, k_ref, c_ref, vt_ref, o_ref, *scratch, tq, tk):
    n_heads = q_ref.shape[2] // HEAD_DIM
    hg = pl.program_id(1)
    qi = pl.program_id(2)
    krow = lax.broadcasted_iota(jnp.int32, (tk, tq), 0)
    qcol = lax.broadcasted_iota(jnp.int32, (tk, tq), 1)
    lane = lax.broadcasted_iota(jnp.int32, (tq, LANES), 1)
    qs = []
    for hd in range(n_heads):
        pick = (lane % FOX_HEADS == hg * n_heads + hd) & (lane < 3 * FOX_HEADS)
        sel = jnp.where(pick, -1.0, 0.0).astype(BF16)
        qfull = q_ref[0, :, (hd // 2) * LANES:(hd // 2 + 1) * LANES]
        qs.append(jnp.concatenate([jnp.where(_head_mask(hd % 2), qfull, jnp.zeros_like(qfull)), sel], axis=1))

    def qk(hd, kt):
        ks = pl.multiple_of(kt * tk, tk)
        lanes = slice((hd // 2) * LANES, (hd // 2 + 1) * LANES)
        k = jnp.concatenate([k_ref[0, pl.ds(ks, tk), lanes], c_ref[0, pl.ds(ks, tk), :]], axis=1)
        return _qk(k, qs[hd])

    def values(hd, kt):
        return vt_ref[0, (hd // 2) * LANES:(hd // 2 + 1) * LANES, pl.ds(pl.multiple_of(kt * tk, tk), tk)]

    def causal(s):
        return jnp.where(krow <= qcol, s, NEG_LOGIT)

    outs = _attend(n_heads, qi + 1, qk, values, scratch, mask_last=causal)
    for hp in range(n_heads // 2):
        o_ref[0, :, hp * LANES:(hp + 1) * LANES] = _pair_out(outs[2 * hp:2 * hp + 2])


def _fox_attention(fq, fk, cparts, fvt):
    b, s, w = fq.shape
    tq = tk = ATT_TQ
    gw = ATT_HEADS * HEAD_DIM
    qspec = pl.BlockSpec((1, tq, gw), lambda bi, hg, qi: (bi, qi, hg))
    kspec = pl.BlockSpec((1, s, gw), lambda bi, hg, qi: (bi, 0, hg))
    cspec = pl.BlockSpec((1, s, LANES), lambda bi, hg, qi: (bi, 0, 0))
    vspec = pl.BlockSpec((1, gw, s), lambda bi, hg, qi: (bi, hg, 0))
    return pl.pallas_call(
        functools.partial(_fox_body, tq=tq, tk=tk),
        grid=(b, w // gw, s // tq),
        in_specs=[qspec, kspec, cspec, vspec],
        out_specs=qspec,
        out_shape=jax.ShapeDtypeStruct((b, s, w), BF16),
        scratch_shapes=_attend_scratch(ATT_HEADS, tq, tk, LANES),
        compiler_params=_cparams(("arbitrary", "arbitrary", "arbitrary")),
        name="fox_attention",
    )(fq, fk, cparts, fvt)


def _hi_key_to_bits(h):
    return jnp.where(h >= 0x8000, h - 0x8000, (~h) & 0xFFFF)


def _tree_sum(parts):
    while len(parts) > 1:
        parts = [a + b for a, b in zip(parts[0::2], parts[1::2])] + ([parts[-1]] if len(parts) % 2 else [])
    return parts[0]


def _dsa_body(dq_ref, iq_ref, iw_ref, dk_ref, dvt_ref, ik_ref, o_ref, sc_ref, tie_ref, a1_ref, d1_ref, d0_ref, a2_ref,
              *scratch, tq, tk, top_k):
    qi = pl.program_id(1)
    idx_bits = max(1, (sc_ref.shape[0] - 1).bit_length())
    n_tiles = ((qi + 1) * tq + tk - 1) // tk
    qpos = lax.broadcasted_iota(jnp.int32, (1, tq), 1) + qi * tq
    limit = (qpos // CHUNK + 1) * CHUNK
    krow = lax.broadcasted_iota(jnp.int32, (tk, tq), 0)
    kf = jnp.float32(top_k)

    iqf = iq_ref[0]
    iw = iw_ref[0]
    iqs = [jnp.where(_head_mask(h % 2), iqf[:, (h // 2) * LANES:(h // 2 + 1) * LANES], jnp.zeros((tq, LANES), BF16))
           for h in range(IDX_HEADS)]
    ws = [iw[h:h + 1, :] for h in range(IDX_HEADS)]

    def score_tile(kt, _):
        ks = pl.multiple_of(kt * tk, tk)
        ik = ik_ref[0, pl.ds(ks, tk), :]
        sc = None
        for h in range(IDX_HEADS):
            term = jnp.maximum(_qk(ik, iqs[h]), 0.0) * ws[h]
            sc = term if sc is None else sc + term
        sc = jnp.where(krow + ks < limit, sc, NEG_SCORE)
        sc = jnp.where(sc == 0.0, 0.0, sc)
        sc_ref[pl.ds(ks, tk), :] = sc
        bits = pltpu.bitcast(sc, jnp.int32)
        a1_ref[pl.ds(ks, tk), :] = pltpu.bitcast(bits & jnp.int32(-65536), F32).astype(BF16)
        lo = bits & 0xFFFF
        klo = jnp.where(bits < 0, 0xFFFF - lo, lo)
        d1_ref[pl.ds(ks, tk), :] = lax.shift_right_logical(klo, 8).astype(F32).astype(BF16)
        d0_ref[pl.ds(ks, tk), :] = (klo & 0xFF).astype(F32).astype(BF16)
        return 0

    lax.fori_loop(0, n_tiles, score_tile, 0)

    def count(pred):
        def body(kt, cnt):
            ks = pl.multiple_of(kt * tk, tk)
            c = jnp.where(pred(ks), 1.0, 0.0)
            return cnt + jnp.sum(c.reshape(tk // COUNT_ROWS, COUNT_ROWS, tq), axis=0)
        cnt = lax.fori_loop(0, n_tiles, body, jnp.zeros((COUNT_ROWS, tq), F32))
        return jnp.sum(cnt, axis=0, keepdims=True)

    def count_packed(ref, op, ref_b):
        def body(kt, cnt):
            x = ref[pl.ds(pl.multiple_of(kt * tk, tk), tk), :]
            c = jnp.where(op(x, ref_b), jnp.ones((), BF16), jnp.zeros((), BF16))
            part = _tree_sum([c[i * COUNT_ROWS:(i + 1) * COUNT_ROWS, :] for i in range(tk // COUNT_ROWS)])
            return cnt + part.astype(F32)
        cnt = lax.fori_loop(0, n_tiles, body, jnp.zeros((COUNT_ROWS, tq), F32))
        return jnp.sum(cnt, axis=0, keepdims=True)

    def hi_float(h):
        return pltpu.bitcast(lax.shift_left(_hi_key_to_bits(h), 16), F32).astype(BF16)

    def hi_step(it, h):
        cand = h | lax.shift_left(jnp.int32(1), 15 - it)
        cnt = count_packed(a1_ref, lambda x, c: x >= c, hi_float(cand))
        return jnp.where(cnt >= kf, cand, h)

    h = lax.fori_loop(0, 16, hi_step, jnp.zeros((1, tq), jnp.int32))
    hbits = _hi_key_to_bits(h)
    hbits = jnp.where((hbits & 0x7F80) == 0, hbits & 0x8000, hbits)
    thr_hi = pltpu.bitcast(lax.shift_left(hbits, 16), F32).astype(BF16)
    need_hi = kf - count_packed(a1_ref, lambda x, c: x > c, thr_hi)

    def restrict(src_ref, match, digit_ref):
        def body(kt, _):
            rows = pl.ds(pl.multiple_of(kt * tk, tk), tk)
            a2_ref[rows, :] = jnp.where(src_ref[rows, :] == match, digit_ref[rows, :], -jnp.ones((), BF16))
            return 0
        lax.fori_loop(0, n_tiles, body, 0)

    def digit(need):
        def step(it, v):
            cand = v + lax.shift_left(jnp.int32(1), 7 - it).astype(F32)
            cnt = count_packed(a2_ref, lambda x, c: x >= c, cand.astype(BF16))
            return jnp.where(cnt >= need, cand, v)
        return lax.fori_loop(0, 8, step, jnp.zeros((1, tq), F32))

    restrict(a1_ref, thr_hi, d1_ref)
    v1 = digit(need_hi)
    need_lo = need_hi - count_packed(a2_ref, lambda x, c: x > c, v1.astype(BF16))
    restrict(a2_ref, v1.astype(BF16), d0_ref)
    v0 = digit(need_lo)
    klo = (v1 * 256.0 + v0).astype(jnp.int32)
    lo = jnp.where(hbits >= 0x8000, 0xFFFF - klo, klo)
    thr = pltpu.bitcast(lax.shift_left(hbits, 16) | lo, F32)

    n_gt = count(lambda ks: sc_ref[pl.ds(ks, tk), :] > thr)
    n_ge = count(lambda ks: sc_ref[pl.ds(ks, tk), :] >= thr)
    need = kf - n_gt

    def tie_tile(kt, _):
        ks = pl.multiple_of(kt * tk, tk)
        tie_ref[pl.ds(ks, tk), :] = jnp.where(sc_ref[pl.ds(ks, tk), :] == thr, (krow + ks).astype(F32), IDX_NONE)
        return 0

    lax.fori_loop(0, n_tiles, tie_tile, 0)
    surplus = jnp.max(n_ge - n_gt - need)

    def tie_search():
        def step(it, x):
            cand = x + lax.shift_left(jnp.int32(1), idx_bits - 1 - it).astype(F32)
            cnt = count(lambda ks: tie_ref[pl.ds(ks, tk), :] < cand)
            return jnp.where(cnt < need, cand, x)
        return lax.fori_loop(0, idx_bits, step, jnp.zeros((1, tq), F32))

    last_tie = lax.cond(surplus > 0.0, tie_search, lambda: jnp.full((1, tq), IDX_NONE / 2, F32))

    def bias_tile(kt, _):
        ks = pl.multiple_of(kt * tk, tk)
        keep = ((sc_ref[pl.ds(ks, tk), :] > thr) | (tie_ref[pl.ds(ks, tk), :] <= last_tie)) & (krow + ks < limit)
        sc_ref[pl.ds(ks, tk), :] = jnp.where(keep, 0.0, NEG_LOGIT)
        return 0

    lax.fori_loop(0, n_tiles, bias_tile, 0)

    n_heads = dq_ref.shape[2] // HEAD_DIM
    qs = []
    for hd in range(n_heads):
        qfull = dq_ref[0, :, (hd // 2) * LANES:(hd // 2 + 1) * LANES]
        qs.append(jnp.where(_head_mask(hd % 2), qfull, jnp.zeros_like(qfull)))

    def qk(hd, kt):
        ks = pl.multiple_of(kt * tk, tk)
        lanes = slice((hd // 2) * LANES, (hd // 2 + 1) * LANES)
        return _qk(dk_ref[0, pl.ds(ks, tk), lanes], qs[hd]) + sc_ref[pl.ds(ks, tk), :]

    def values(hd, kt):
        return dvt_ref[0, (hd // 2) * LANES:(hd // 2 + 1) * LANES, pl.ds(pl.multiple_of(kt * tk, tk), tk)]

    outs = _attend(n_heads, n_tiles, qk, values, scratch)
    for hp in range(n_heads // 2):
        o_ref[0, :, hp * LANES:(hp + 1) * LANES] = _pair_out(outs[2 * hp:2 * hp + 2])


def _dsa_attention(dq, dk, dvt, iq, ik, iw):
    b, s, w = dq.shape
    tq, tk = DSA_TQ, DSA_TK
    top_k = min(DSA_TOPK_MAX, s // 4)

    def qspec(width):
        return pl.BlockSpec((1, tq, width), lambda bi, qi: (bi, qi, 0))

    def kspec(width):
        return pl.BlockSpec((1, s, width), lambda bi, qi: (bi, 0, 0))

    return pl.pallas_call(
        functools.partial(_dsa_body, tq=tq, tk=tk, top_k=top_k),
        grid=(b, s // tq),
        in_specs=[qspec(w), qspec(iq.shape[2]), pl.BlockSpec((1, iw.shape[1], tq), lambda bi, qi: (bi, 0, qi)),
                  kspec(w), pl.BlockSpec((1, w, s), lambda bi, qi: (bi, 0, 0)), kspec(LANES)],
        out_specs=qspec(w),
        out_shape=jax.ShapeDtypeStruct((b, s, w), BF16),
        scratch_shapes=[pltpu.VMEM((s, tq), F32)] * 2 + [pltpu.VMEM((s, tq), BF16)] * 4
        + _attend_scratch(w // HEAD_DIM, tq, tk, LANES),
        compiler_params=_cparams(("arbitrary", "arbitrary")),
        name="dsa_attention",
    )(dq, iq, iw, dk, dvt, ik)


MLA_DOWN_COLS = MLA_Q_LORA + MLA_KV_LORA + 2 * MLA_ROPE


def _mla_proj_body(h_ref, sh_ref, sc_ref, g_ref, wd_ref, qn_ref, kvn_ref, wuq_ref, wukv_ref, cos_ref, sin_ref,
                   qnope_ref, qrope_ref, knope_ref, v_ref, krope_ref, *, scale):
    x = h_ref[...]
    u = _modulated_norm(x, g_ref[...], sc_ref[0], sh_ref[0]).astype(BF16)
    cos = cos_ref[...]
    sin = sin_ref[...]
    half = MLA_ROPE // 2
    down = jnp.dot(u, wd_ref[...], preferred_element_type=F32)

    def norm(z, g):
        ms = jnp.mean(z * z, axis=-1, keepdims=True)
        return (z * lax.rsqrt(ms + RMS_EPS) * g).astype(BF16)

    cq = norm(down[:, 0:MLA_Q_LORA], qn_ref[...])
    ckv = norm(down[:, MLA_Q_LORA:MLA_Q_LORA + MLA_KV_LORA], kvn_ref[...])
    krope_ref[0] = _rope_slab(down[:, MLA_Q_LORA + MLA_KV_LORA:], cos, sin, half).astype(BF16)

    n_nope = MLA_HEADS * MLA_NOPE
    qn = jnp.dot(cq, wuq_ref[:, 0:n_nope], preferred_element_type=F32)
    qnope_ref[0] = (qn * scale).astype(BF16)
    qr = jnp.dot(cq, wuq_ref[:, n_nope:], preferred_element_type=F32)
    qr = jnp.concatenate(
        [_rope_slab(qr[:, j * LANES:(j + 1) * LANES], cos, sin, half) for j in range(qr.shape[1] // LANES)], axis=1)
    qrope_ref[0] = (qr * scale).astype(BF16)
    knope_ref[0] = jnp.dot(ckv, wukv_ref[:, 0:n_nope], preferred_element_type=F32).astype(BF16)
    v_ref[0] = jnp.dot(ckv, wukv_ref[:, n_nope:], preferred_element_type=F32).T.astype(BF16)


def _mla_proj(h, sh, sc, g, wd, qn, kvn, wuq, wukv, cos, sin, b, s):
    t, d = h.shape
    tpb = s // TM
    row = pl.BlockSpec((TM, d), lambda i: (i, 0))
    mod = pl.BlockSpec((1, 1, d), lambda i: (i // tpb, 0, 0))
    tab = pl.BlockSpec((TM, LANES), lambda i: (i, 0))

    def out(width):
        return pl.BlockSpec((1, TM, width), lambda i: (i // tpb, i % tpb, 0))

    def shp(width):
        return jax.ShapeDtypeStruct((b, s, width), BF16)

    n_nope = MLA_HEADS * MLA_NOPE
    n_rope = MLA_HEADS * MLA_ROPE
    n_v = MLA_HEADS * MLA_V
    scale = (MLA_NOPE + MLA_ROPE) ** -0.5 * LOG2E
    return pl.pallas_call(
        functools.partial(_mla_proj_body, scale=scale),
        grid=(t // TM,),
        in_specs=[row, mod, mod, _const_spec((1, d)), _const_spec((d, MLA_DOWN_COLS)),
                  _const_spec((1, MLA_Q_LORA)), _const_spec((1, MLA_KV_LORA)),
                  _const_spec((MLA_Q_LORA, n_nope + n_rope)), _const_spec((MLA_KV_LORA, n_nope + n_v)), tab, tab],
        out_specs=[out(n_nope), out(n_rope), out(n_nope),
                   pl.BlockSpec((1, n_v, TM), lambda i: (i // tpb, 0, i % tpb)), out(LANES)],
        out_shape=[shp(n_nope), shp(n_rope), shp(n_nope), jax.ShapeDtypeStruct((b, n_v, s), BF16), shp(LANES)],
        compiler_params=_cparams(("arbitrary",)),
        name="mla_in_proj",
    )(h, sh, sc, g, wd, qn, kvn, wuq, wukv, cos, sin)


def _mla_body(qn_ref, qr_ref, kn_ref, kr_ref, vt_ref, o_ref, *scratch, tq, tk):
    n_heads = qn_ref.shape[2] // MLA_NOPE
    qi = pl.program_id(2)
    qs = []
    for hd in range(n_heads):
        qr = qr_ref[0, :, (hd // 2) * LANES:(hd // 2 + 1) * LANES]
        qs.append(jnp.concatenate([qn_ref[0, :, hd * LANES:(hd + 1) * LANES],
                                   jnp.where(_head_mask(hd % 2), qr, jnp.zeros_like(qr))], axis=1))
    krow = lax.broadcasted_iota(jnp.int32, (tk, tq), 0)
    qcol = lax.broadcasted_iota(jnp.int32, (tk, tq), 1)

    def qk(hd, kt):
        ks = pl.multiple_of(kt * tk, tk)
        k = jnp.concatenate([kn_ref[0, pl.ds(ks, tk), hd * LANES:(hd + 1) * LANES], kr_ref[0, pl.ds(ks, tk), :]],
                            axis=1)
        return _qk(k, qs[hd])

    def values(hd, kt):
        return vt_ref[0, hd * LANES:(hd + 1) * LANES, pl.ds(pl.multiple_of(kt * tk, tk), tk)]

    def chunk_causal(s):
        return jnp.where(krow // CHUNK <= qcol // CHUNK, s, NEG_LOGIT)

    outs = _attend(n_heads, qi + 1, qk, values, scratch, mask_last=chunk_causal)
    for hd in range(n_heads):
        o_ref[0, :, hd * LANES:(hd + 1) * LANES] = outs[hd].T.astype(BF16)


def _mla_attention(qnope, qrope, knope, krope, vt):
    b, s, w = qnope.shape
    tq = tk = ATT_TQ
    gw = ATT_HEADS * MLA_NOPE
    gr = ATT_HEADS * MLA_ROPE
    qspec = pl.BlockSpec((1, tq, gw), lambda bi, hg, qi: (bi, qi, hg))
    qrspec = pl.BlockSpec((1, tq, gr), lambda bi, hg, qi: (bi, qi, hg))
    kspec = pl.BlockSpec((1, s, gw), lambda bi, hg, qi: (bi, 0, hg))
    krspec = pl.BlockSpec((1, s, LANES), lambda bi, hg, qi: (bi, 0, 0))
    vspec = pl.BlockSpec((1, gw, s), lambda bi, hg, qi: (bi, hg, 0))
    return pl.pallas_call(
        functools.partial(_mla_body, tq=tq, tk=tk),
        grid=(b, w // gw, s // tq),
        in_specs=[qspec, qrspec, kspec, krspec, vspec],
        out_specs=qspec,
        out_shape=jax.ShapeDtypeStruct((b, s, w), BF16),
        scratch_shapes=_attend_scratch(ATT_HEADS, tq, tk, LANES),
        compiler_params=_cparams(("arbitrary", "arbitrary", "arbitrary")),
        name="mla_attention",
    )(qnope, qrope, knope, krope, vt)


def _hyb_weight(w_in, b_f):
    o = 0
    parts = {}
    for name, width in (("fq", 512), ("fk", 512), ("fv", 512), ("ff", FOX_HEADS), ("dq", 512), ("dk", 512),
                        ("dv", 512), ("iq", 256), ("iw", IDX_HEADS), ("ik", HEAD_DIM)):
        parts[name] = w_in[:, o:o + width]
        o += width
    qs = HEAD_DIM ** -0.5
    pad = jnp.zeros((w_in.shape[0], LANES - FOX_HEADS - IDX_HEADS), w_in.dtype)
    w = jnp.concatenate([parts["fq"] * qs, parts["fk"], parts["fv"], parts["dq"] * qs, parts["dk"], parts["dv"],
                         parts["iq"] * qs, parts["ik"], parts["ik"],
                         parts["ff"], parts["iw"] * IDX_HEADS ** -0.5, pad], axis=1)
    bf = jnp.concatenate([b_f, jnp.zeros((LANES - FOX_HEADS,), b_f.dtype)]).reshape(1, LANES)
    return w.astype(BF16), bf.astype(F32)


def _mla_weights(w_down, w_uq, w_ukv):
    kr = w_down[:, MLA_Q_LORA + MLA_KV_LORA:]
    wd = jnp.concatenate([w_down, kr], axis=1)
    uq = w_uq.reshape(MLA_Q_LORA, MLA_HEADS, MLA_NOPE + MLA_ROPE)
    wuq = jnp.concatenate([uq[:, :, :MLA_NOPE].reshape(MLA_Q_LORA, -1), uq[:, :, MLA_NOPE:].reshape(MLA_Q_LORA, -1)],
                          axis=1)
    ukv = w_ukv.reshape(MLA_KV_LORA, MLA_HEADS, MLA_NOPE + MLA_V)
    wukv = jnp.concatenate([ukv[:, :, :MLA_NOPE].reshape(MLA_KV_LORA, -1),
                            ukv[:, :, MLA_NOPE:].reshape(MLA_KV_LORA, -1)], axis=1)
    return wd.astype(BF16), wuq.astype(BF16), wukv.astype(BF16)


def kernel(x, c, positions, ada_w, ada_b, norm_g, ffn_w_gate, ffn_w_up, ffn_w_down, hyb_w_in, fox_b_f, hyb_w_out,
           mla_w_down, mla_q_norm, mla_kv_norm, mla_w_uq, mla_w_ukv, mla_w_out, final_g):
    b, s, d = x.shape
    depth = ada_w.shape[0]
    t = b * s
    assert s % TM == 0 and s % ATT_TQ == 0 and s % DSA_TK == 0 and ffn_w_gate.shape[-1] % FFN_CHUNK == 0

    mod = _modulation(c, ada_w, ada_b).reshape(depth, b, N_MOD, 1, d)
    cos_d, sin_d, cos_m, sin_m = _rope_tables(positions)
    wg = ffn_w_gate.astype(BF16)
    wu = ffn_w_up.astype(BF16)
    wdn = ffn_w_down.astype(BF16)

    h = x.reshape(t, d)
    for i in range(depth):
        sh1, sc1, g1, sh2, sc2, g2, sh3, sc3, g3 = [mod[i, :, j] for j in range(N_MOD)]
        ng = norm_g[i].reshape(3, 1, d)
        h = _ffn(h, sh1, sc1, g1, ng[0], wg[i, 0], wu[i, 0], wdn[i, 0], s)
        j = i // 2
        if i % 2 == 0:
            w_in, bf = _hyb_weight(hyb_w_in[j], fox_b_f[j])
            fq, fk, fvt, dq, dk, dvt, iq, ik, cparts, iw = _hyb_proj(h, sh2, sc2, ng[1], w_in, bf, cos_d, sin_d, b, s)
            out_a = _fox_attention(fq, fk, cparts, fvt)
            out_b = _dsa_attention(dq, dk, dvt, iq, ik, iw)
            w_out = hyb_w_out[j].astype(BF16)
            half = out_a.shape[2]
            mix = (g2, [out_a.reshape(t, half), out_b.reshape(t, half)], [w_out[:half], w_out[half:]])
        else:
            wd, wuq, wukv = _mla_weights(mla_w_down[j], mla_w_uq[j], mla_w_ukv[j])
            qn, qr, kn, vt, kr = _mla_proj(h, sh2, sc2, ng[1], wd, mla_q_norm[j].reshape(1, -1),
                                           mla_kv_norm[j].reshape(1, -1), wuq, wukv, cos_m, sin_m, b, s)
            out = _mla_attention(qn, qr, kn, kr, vt)
            mix = (g2, [out.reshape(t, -1)], [mla_w_out[j].astype(BF16)])
        last = i == depth - 1
        h = _ffn(h, sh3, sc3, g3, ng[2], wg[i, 1], wu[i, 1], wdn[i, 1], s,
                 final_g=final_g.reshape(1, d) if last else None, mix=mix)
    return h.reshape(b, s, d)
```

```python
import functools
import math

import jax
import jax.numpy as jnp
from jax import lax
from jax.experimental import pallas as pl
from jax.experimental.pallas import tpu as pltpu

F32 = jnp.float32
BF16 = jnp.bfloat16

CHUNK = 64
RMS_EPS = 1e-6
ROPE_THETA = 500000.0
MLA_ROPE_THETA = 10000.0
MACARON_WEIGHT = 0.5
N_MOD = 9
FOX_HEADS = 8
HEAD_DIM = 64
DSA_ROT = 16
IDX_HEADS = 4
DSA_TOPK_MAX = 256
MLA_HEADS = 8
MLA_NOPE = 128
MLA_ROPE = 64
MLA_V = 128
MLA_Q_LORA = 384
MLA_KV_LORA = 256

LANES = 128
VMEM_LIMIT = 56 * 1024 * 1024

LOG2E = math.log2(math.e)
NEG_LOGIT = -1e30
NEG_SCORE = -3e38
IDX_NONE = 1e9

TM = 512
FFN_CHUNK = 256
ATT_TQ = 512
ATT_TK = 512
ATT_HEADS = 4
FOX_STEP_HEADS = 8
DSA_TQ = 256
DSA_TK = 512
COUNT_ROWS = 32


def _cparams(sem):
    return pltpu.CompilerParams(dimension_semantics=sem, vmem_limit_bytes=VMEM_LIMIT)


def _const_spec(shape):
    nd = len(shape)
    return pl.BlockSpec(shape, lambda *_: (0,) * nd, pipeline_mode=pl.Buffered(1))


def _silu(x):
    return x * jax.nn.sigmoid(x)


def _modulated_norm(x, g, sc, sh):
    ms = jnp.mean(x * x, axis=-1, keepdims=True)
    y = x * lax.rsqrt(ms + RMS_EPS) * g
    return y * (1.0 + sc) + sh


def _mod_body(c_ref, w_ref, b_ref, o_ref):
    cond = _silu(c_ref[...]).astype(BF16)
    o_ref[0] = jnp.dot(cond, w_ref[0].astype(BF16), preferred_element_type=F32) + b_ref[0]


def _modulation(c, ada_w, ada_b):
    depth, d, n = ada_w.shape
    b = c.shape[0]
    tn = n // N_MOD
    return pl.pallas_call(
        _mod_body,
        grid=(depth, n // tn),
        in_specs=[pl.BlockSpec((b, d), lambda i, j: (0, 0)),
                  pl.BlockSpec((1, d, tn), lambda i, j: (i, 0, j)),
                  pl.BlockSpec((1, 1, tn), lambda i, j: (i, 0, j))],
        out_specs=pl.BlockSpec((1, b, tn), lambda i, j: (i, 0, j)),
        out_shape=jax.ShapeDtypeStruct((depth, b, n), F32),
        compiler_params=_cparams(("arbitrary", "arbitrary")),
        name="adaln_mod",
    )(c, ada_w, ada_b.reshape(depth, 1, n))


def _rope_table_body(pos_ref, invd_ref, sgnd_ref, invm_ref, sgnm_ref, cd_ref, sd_ref, cm_ref, sm_ref):
    pos = pos_ref[...].astype(F32)
    for inv_ref, sgn_ref, c_ref, s_ref in ((invd_ref, sgnd_ref, cd_ref, sd_ref),
                                           (invm_ref, sgnm_ref, cm_ref, sm_ref)):
        ang = pos * inv_ref[...]
        sgn = sgn_ref[...]
        c_ref[...] = jnp.where(sgn != 0.0, jnp.cos(ang), 1.0)
        s_ref[...] = sgn * jnp.sin(ang)


def _lane_pattern(rot, theta):
    half = rot // 2
    inv_freq = jnp.exp(-math.log(theta) * 2.0 * jnp.arange(half, dtype=F32) / rot)
    d = jnp.arange(LANES) % HEAD_DIM
    inv = jnp.where(d < rot, inv_freq[d % half], 0.0).astype(F32)
    sgn = jnp.where(d < half, -1.0, jnp.where(d < rot, 1.0, 0.0)).astype(F32)
    return inv.reshape(1, LANES), sgn.reshape(1, LANES)


def _rope_tables(positions):
    t = positions.size
    invd, sgnd = _lane_pattern(DSA_ROT, ROPE_THETA)
    invm, sgnm = _lane_pattern(MLA_ROPE, MLA_ROPE_THETA)
    tm = 1024
    row = pl.BlockSpec((tm, LANES), lambda i: (i, 0))
    vec = pl.BlockSpec((1, LANES), lambda i: (0, 0))
    tab = jax.ShapeDtypeStruct((t, LANES), F32)
    return pl.pallas_call(
        _rope_table_body,
        grid=(t // tm,),
        in_specs=[pl.BlockSpec((tm, 1), lambda i: (i, 0)), vec, vec, vec, vec],
        out_specs=[row, row, row, row],
        out_shape=[tab, tab, tab, tab],
        compiler_params=_cparams(("arbitrary",)),
        name="rope_tables",
    )(positions.reshape(t, 1), invd, sgnd, invm, sgnm)


def _rope_slab(y, cos, sin, half):
    lane = lax.broadcasted_iota(jnp.int32, (1, LANES), 1) % HEAD_DIM
    first = lane < half
    partner = jnp.where(first, pltpu.roll(y, LANES - half, 1), pltpu.roll(y, half, 1))
    return y * cos + partner * sin


def _ffn_body(h_ref, sh_ref, sc_ref, gt_ref, g_ref, wg_ref, wu_ref, wd_ref, *rest, nf, fc, final, n_mix):
    rest = list(rest)
    x = h_ref[...]
    if n_mix:
        mg_ref = rest.pop(0)
        mix = None
        for x_ref, w_ref in zip(rest[:n_mix], rest[n_mix:2 * n_mix]):
            part = jnp.dot(x_ref[...], w_ref[...], preferred_element_type=F32)
            mix = part if mix is None else mix + part
        x = x + mg_ref[0] * mix
        rest = rest[2 * n_mix:]
    if final:
        fg_ref, o_ref, acc_ref = rest
    else:
        o_ref, acc_ref = rest
    u = _modulated_norm(x, g_ref[...], sc_ref[0], sh_ref[0]).astype(BF16)
    for f in range(nf):
        sl = slice(f * fc, (f + 1) * fc)
        gp = jnp.dot(u, wg_ref[:, sl], preferred_element_type=F32)
        up = jnp.dot(u, wu_ref[:, sl], preferred_element_type=F32)
        a = (_silu(gp) * up).astype(BF16)
        d = jnp.dot(a, wd_ref[sl, :], preferred_element_type=F32)
        if f == 0:
            acc_ref[...] = d
        else:
            acc_ref[...] += d
    y = x + (MACARON_WEIGHT * gt_ref[0]) * acc_ref[...]
    if final:
        ms = jnp.mean(y * y, axis=-1, keepdims=True)
        y = y * lax.rsqrt(ms + RMS_EPS) * fg_ref[...]
    o_ref[...] = y


def _ffn(h, sh, sc, gt, g, wg, wu, wd, s, final_g=None, mix=None):
    t, d = h.shape
    f = wg.shape[1]
    tpb = s // TM
    row = pl.BlockSpec((TM, d), lambda i: (i, 0))
    mod = pl.BlockSpec((1, 1, d), lambda i: (i // tpb, 0, 0))
    in_specs = [row, mod, mod, mod, _const_spec((1, d)),
                _const_spec((d, f)), _const_spec((d, f)), _const_spec((f, d))]
    args = [h, sh, sc, gt, g, wg, wu, wd]
    n_mix = 0
    if mix is not None:
        mgate, xs, ws = mix
        n_mix = len(xs)
        in_specs += [mod] + [pl.BlockSpec((TM, x.shape[1]), lambda i: (i, 0)) for x in xs] + [_const_spec(w.shape) for w in ws]
        args += [mgate, *xs, *ws]
    if final_g is not None:
        in_specs.append(_const_spec((1, d)))
        args.append(final_g)
    return pl.pallas_call(
        functools.partial(_ffn_body, nf=f // FFN_CHUNK, fc=FFN_CHUNK, final=final_g is not None, n_mix=n_mix),
        grid=(t // TM,),
        in_specs=in_specs,
        out_specs=row,
        out_shape=jax.ShapeDtypeStruct((t, d), F32),
        scratch_shapes=[pltpu.VMEM((TM, d), F32)],
        compiler_params=_cparams(("arbitrary",)),
        name="swiglu_half_step",
    )(*args)


HYB_W = 512
HYB_COLS = 6 * HYB_W + 256 + 128 + 128


def _hyb_proj_body(h_ref, sh_ref, sc_ref, g_ref, w_ref, bf_ref, cos_ref, sin_ref,
                   fq_ref, fk_ref, fvt_ref, dq_ref, dk_ref, dvt_ref, iq_ref, ik_ref, cum_ref, iw_ref,
                   carry_ref, *, tpb):
    i = pl.program_id(0)
    x = h_ref[...]
    tm = x.shape[0]
    u = _modulated_norm(x, g_ref[...], sc_ref[0], sh_ref[0]).astype(BF16)
    cos = cos_ref[...]
    sin = sin_ref[...]
    half = DSA_ROT // 2

    def proj(c0, width):
        return jnp.dot(u, w_ref[:, c0:c0 + width], preferred_element_type=F32)

    def roped(y):
        return jnp.concatenate(
            [_rope_slab(y[:, j * LANES:(j + 1) * LANES], cos, sin, half) for j in range(y.shape[1] // LANES)],
            axis=1)

    fq_ref[0] = (proj(0 * HYB_W, HYB_W) * LOG2E).astype(BF16)
    fk_ref[0] = proj(1 * HYB_W, HYB_W).astype(BF16)
    fvt_ref[0] = proj(2 * HYB_W, HYB_W).T.astype(BF16)
    dq_ref[0] = (roped(proj(3 * HYB_W, HYB_W)) * LOG2E).astype(BF16)
    dk_ref[0] = roped(proj(4 * HYB_W, HYB_W)).astype(BF16)
    dvt_ref[0] = proj(5 * HYB_W, HYB_W).T.astype(BF16)
    tail = proj(6 * HYB_W, 512)
    iq_ref[0] = roped(tail[:, 0:256]).astype(BF16)
    ik_ref[0] = roped(tail[:, 256:384]).astype(BF16)
    gates = tail[:, 384:512]
    iw_ref[0] = gates.T[FOX_HEADS:2 * FOX_HEADS, :]

    z = gates + bf_ref[...]
    logf = jnp.minimum(z, 0.0) - jnp.log1p(jnp.exp(-jnp.abs(z)))
    rows = lax.broadcasted_iota(jnp.int32, (tm, LANES), 0)
    c = logf
    k = 1
    while k < tm:
        c = c + jnp.where(rows >= k, pltpu.roll(c, k, 0), 0.0)
        k *= 2

    @pl.when(i % tpb == 0)
    def _():
        carry_ref[...] = jnp.zeros_like(carry_ref)

    c = c + carry_ref[0:1, :]
    carry_ref[...] = jnp.broadcast_to(c[tm - 1:tm, :], carry_ref.shape)
    lane = lax.broadcasted_iota(jnp.int32, (1, LANES), 1)
    c = jnp.where(lane < FOX_HEADS, c * LOG2E, 0.0)
    hi = c.astype(BF16).astype(F32)
    mid = (c - hi).astype(BF16).astype(F32)
    lo = (c - hi - mid).astype(BF16).astype(F32)
    cum_ref[0] = (hi + pltpu.roll(mid, FOX_HEADS, 1) + pltpu.roll(lo, 2 * FOX_HEADS, 1)).astype(BF16)


def _hyb_proj(h, sh, sc, g, w, bf, cos, sin, b, s):
    t, d = h.shape
    tpb = s // TM
    row = pl.BlockSpec((TM, d), lambda i: (i, 0))
    mod = pl.BlockSpec((1, 1, d), lambda i: (i // tpb, 0, 0))
    tab = pl.BlockSpec((TM, LANES), lambda i: (i, 0))

    def out(width):
        return pl.BlockSpec((1, TM, width), lambda i: (i // tpb, i % tpb, 0))

    def shp(width, dt=BF16):
        return jax.ShapeDtypeStruct((b, s, width), dt)

    def out_t(width):
        return pl.BlockSpec((1, width, TM), lambda i: (i // tpb, 0, i % tpb))

    def shp_t(width, dt=BF16):
        return jax.ShapeDtypeStruct((b, width, s), dt)

    return pl.pallas_call(
        functools.partial(_hyb_proj_body, tpb=tpb),
        grid=(t // TM,),
        in_specs=[row, mod, mod, _const_spec((1, d)), _const_spec((d, HYB_COLS)), _const_spec((1, LANES)), tab, tab],
        out_specs=[out(HYB_W), out(HYB_W), out_t(HYB_W), out(HYB_W), out(HYB_W), out_t(HYB_W),
                   out(256), out(LANES), out(LANES), out_t(FOX_HEADS)],
        out_shape=[shp(HYB_W), shp(HYB_W), shp_t(HYB_W), shp(HYB_W), shp(HYB_W), shp_t(HYB_W),
                   shp(256), shp(LANES), shp(LANES), shp_t(FOX_HEADS, F32)],
        scratch_shapes=[pltpu.VMEM((8, LANES), F32)],
        compiler_params=_cparams(("arbitrary",)),
        name="hybrid_in_proj",
    )(h, sh, sc, g, w, bf, cos, sin)


def _attend_scratch(n_heads, tq, tk, n):
    return [pltpu.VMEM((2, n_heads, tk, tq), F32), pltpu.VMEM((2, n_heads, 1, tq), F32),
            pltpu.VMEM((n_heads, 1, tq), F32), pltpu.VMEM((n_heads, 1, tq), F32), pltpu.VMEM((n_heads, n, tq), F32)]


def _attend(n_heads, n_tiles, qk, values, scratch, mask_last=None):
    s_ref, smax_ref, m_ref, l_ref, acc_ref = scratch

    def stage(slot, kt):
        for hd in range(n_heads):
            s = qk(hd, kt)
            s_ref[slot, hd] = s
            smax_ref[slot, hd] = jnp.max(s, axis=0, keepdims=True)

    def step(slot, kt, mask=None):
        for hd in range(n_heads):
            s = s_ref[slot, hd]
            smax = smax_ref[slot, hd]
            if mask is not None:
                s = mask(s)
                smax = jnp.max(s, axis=0, keepdims=True)
            m = m_ref[hd]
            m_new = jnp.maximum(m, smax)
            alpha = jnp.exp2(m - m_new)
            p = jnp.exp2(s - m_new)
            m_ref[hd] = m_new
            l_ref[hd] = alpha * l_ref[hd] + jnp.sum(p, axis=0, keepdims=True)
            acc_ref[hd] = alpha * acc_ref[hd] + jnp.dot(values(hd, kt), p.astype(BF16), preferred_element_type=F32)

    m_ref[...] = jnp.full(m_ref.shape, NEG_LOGIT, F32)
    l_ref[...] = jnp.zeros(l_ref.shape, F32)
    acc_ref[...] = jnp.zeros(acc_ref.shape, F32)
    stage(0, 0)
    n_pairs = (n_tiles - 1) // 2

    def pair(i, _):
        stage(1, 2 * i + 1)
        step(0, 2 * i)
        stage(0, 2 * i + 2)
        step(1, 2 * i + 1)
        return 0

    lax.fori_loop(0, n_pairs, pair, 0)
    odd_tail = n_tiles - 2 * n_pairs == 2

    @pl.when(odd_tail)
    def _():
        stage(1, n_tiles - 1)
        step(0, n_tiles - 2)

    step(jnp.where(odd_tail, 1, 0), n_tiles - 1, mask=mask_last)
    return [acc_ref[hd] / l_ref[hd] for hd in range(n_heads)]


def _head_mask(j):
    lane = lax.broadcasted_iota(jnp.int32, (1, LANES), 1)
    return (lane // HEAD_DIM) == j


def _qk(q, k):
    return lax.dot_general(q, k, (((1,), (1,)), ((), ())), preferred_element_type=F32)


def _pair_out(outs):
    return jnp.concatenate(outs, axis=0).T.astype(BF16)


def _fox_body(q_ref, k_ref, c_ref, vt_ref, o_ref, *scratch, tq, tk):
    n_heads = q_ref.shape[2] // HEAD_DIM
    hg = pl.program_id(1)
    qi = pl.program_id(2)
    krow = lax.broadcasted_iota(jnp.int32, (tk, tq), 0)
    qcol = lax.broadcasted_iota(jnp.int32, (tk, tq), 1)
    lane = lax.broadcasted_iota(jnp.int32, (tq, LANES), 1)
    qs = []
    for hd in range(n_heads):
        pick = (lane % FOX_HEADS == hg * n_heads + hd) & (lane < 3 * FOX_HEADS)
        sel = jnp.where(pick, -1.0, 0.0).astype(BF16)
        qfull = q_ref[0, :, (hd // 2) * LANES:(hd // 2 + 1) * LANES]
        qs.append(jnp.concatenate([jnp.where(_head_mask(hd % 2), qfull, jnp.zeros_like(qfull)), sel], axis=1))

    def qk(hd, kt):
        ks = pl.multiple_of(kt * tk, tk)
        lanes = slice((hd // 2) * LANES, (hd // 2 + 1) * LANES)
        k = jnp.concatenate([k_ref[0, pl.ds(ks, tk), lanes], c_ref[0, pl.ds(ks, tk), :]], axis=1)
        return _qk(k, qs[hd])

    def values(hd, kt):
        return vt_ref[0, hd * HEAD_DIM:(hd + 1) * HEAD_DIM, pl.ds(pl.multiple_of(kt * tk, tk), tk)]

    def causal(s):
        return jnp.where(krow <= qcol, s, NEG_LOGIT)

    outs = _attend(n_heads, qi + 1, qk, values, scratch, mask_last=causal)
    for hp in range(n_heads // 2):
        o_ref[0, :, hp * LANES:(hp + 1) * LANES] = _pair_out(outs[2 * hp:2 * hp + 2])


def _fox_attention(fq, fk, cparts, fvt):
    b, s, w = fq.shape
    tq = tk = ATT_TQ
    gw = FOX_STEP_HEADS * HEAD_DIM
    qspec = pl.BlockSpec((1, tq, gw), lambda bi, hg, qi: (bi, qi, hg))
    kspec = pl.BlockSpec((1, s, gw), lambda bi, hg, qi: (bi, 0, hg))
    cspec = pl.BlockSpec((1, s, LANES), lambda bi, hg, qi: (bi, 0, 0))
    vspec = pl.BlockSpec((1, gw, s), lambda bi, hg, qi: (bi, hg, 0))
    return pl.pallas_call(
        functools.partial(_fox_body, tq=tq, tk=tk),
        grid=(b, w // gw, s // tq),
        in_specs=[qspec, kspec, cspec, vspec],
        out_specs=qspec,
        out_shape=jax.ShapeDtypeStruct((b, s, w), BF16),
        scratch_shapes=_attend_scratch(FOX_STEP_HEADS, tq, tk, HEAD_DIM),
        compiler_params=_cparams(("arbitrary", "arbitrary", "arbitrary")),
        name="fox_attention",
    )(fq, fk, cparts, fvt)


def _hi_key_to_bits(h):
    return jnp.where(h >= 0x8000, h - 0x8000, (~h) & 0xFFFF)


def _tree_sum(parts):
    while len(parts) > 1:
        parts = [a + b for a, b in zip(parts[0::2], parts[1::2])] + ([parts[-1]] if len(parts) % 2 else [])
    return parts[0]


def _dsa_body(dq_ref, iq_ref, iw_ref, dk_ref, dvt_ref, ik_ref, o_ref, sc_ref, tie_ref, a1_ref, d1_ref, d0_ref, a2_ref,
              *scratch, tq, tk, top_k):
    qi = pl.program_id(1)
    idx_bits = max(1, (sc_ref.shape[0] - 1).bit_length())
    n_tiles = ((qi + 1) * tq + tk - 1) // tk
    qpos = lax.broadcasted_iota(jnp.int32, (1, tq), 1) + qi * tq
    limit = (qpos // CHUNK + 1) * CHUNK
    krow = lax.broadcasted_iota(jnp.int32, (tk, tq), 0)
    kf = jnp.float32(top_k)

    iqf = iq_ref[0]
    iw = iw_ref[0]
    iqs = [jnp.where(_head_mask(h % 2), iqf[:, (h // 2) * LANES:(h // 2 + 1) * LANES], jnp.zeros((tq, LANES), BF16))
           for h in range(IDX_HEADS)]
    ws = [iw[h:h + 1, :] for h in range(IDX_HEADS)]

    def score_tile(kt, _):
        ks = pl.multiple_of(kt * tk, tk)
        ik = ik_ref[0, pl.ds(ks, tk), :]
        sc = None
        for h in range(IDX_HEADS):
            term = jnp.maximum(_qk(ik, iqs[h]), 0.0) * ws[h]
            sc = term if sc is None else sc + term
        sc = jnp.where(krow + ks < limit, sc, NEG_SCORE)
        sc = jnp.where(sc == 0.0, 0.0, sc)
        sc_ref[pl.ds(ks, tk), :] = sc
        bits = pltpu.bitcast(sc, jnp.int32)
        a1_ref[pl.ds(ks, tk), :] = pltpu.bitcast(bits & jnp.int32(-65536), F32).astype(BF16)
        lo = bits & 0xFFFF
        klo = jnp.where(bits < 0, 0xFFFF - lo, lo)
        d1_ref[pl.ds(ks, tk), :] = lax.shift_right_logical(klo, 8).astype(F32).astype(BF16)
        d0_ref[pl.ds(ks, tk), :] = (klo & 0xFF).astype(F32).astype(BF16)
        return 0

    lax.fori_loop(0, n_tiles, score_tile, 0)

    def count(pred):
        def body(kt, cnt):
            ks = pl.multiple_of(kt * tk, tk)
            c = jnp.where(pred(ks), 1.0, 0.0)
            return cnt + jnp.sum(c.reshape(tk // COUNT_ROWS, COUNT_ROWS, tq), axis=0)
        cnt = lax.fori_loop(0, n_tiles, body, jnp.zeros((COUNT_ROWS, tq), F32))
        return jnp.sum(cnt, axis=0, keepdims=True)

    def count_packed(ref, op, ref_b):
        def body(kt, cnt):
            x = ref[pl.ds(pl.multiple_of(kt * tk, tk), tk), :]
            c = jnp.where(op(x, ref_b), jnp.ones((), BF16), jnp.zeros((), BF16))
            part = _tree_sum([c[i * COUNT_ROWS:(i + 1) * COUNT_ROWS, :] for i in range(tk // COUNT_ROWS)])
            return cnt + part.astype(F32)
        cnt = lax.fori_loop(0, n_tiles, body, jnp.zeros((COUNT_ROWS, tq), F32))
        return jnp.sum(cnt, axis=0, keepdims=True)

    def hi_float(h):
        return pltpu.bitcast(lax.shift_left(_hi_key_to_bits(h), 16), F32).astype(BF16)

    def hi_step(it, h):
        cand = h | lax.shift_left(jnp.int32(1), 15 - it)
        cnt = count_packed(a1_ref, lambda x, c: x >= c, hi_float(cand))
        return jnp.where(cnt >= kf, cand, h)

    h = lax.fori_loop(0, 16, hi_step, jnp.zeros((1, tq), jnp.int32))
    hbits = _hi_key_to_bits(h)
    hbits = jnp.where((hbits & 0x7F80) == 0, hbits & 0x8000, hbits)
    thr_hi = pltpu.bitcast(lax.shift_left(hbits, 16), F32).astype(BF16)
    need_hi = kf - count_packed(a1_ref, lambda x, c: x > c, thr_hi)

    def restrict(src_ref, match, digit_ref):
        def body(kt, _):
            rows = pl.ds(pl.multiple_of(kt * tk, tk), tk)
            a2_ref[rows, :] = jnp.where(src_ref[rows, :] == match, digit_ref[rows, :], -jnp.ones((), BF16))
            return 0
        lax.fori_loop(0, n_tiles, body, 0)

    def digit(need):
        def step(it, v):
            cand = v + lax.shift_left(jnp.int32(1), 7 - it).astype(F32)
            cnt = count_packed(a2_ref, lambda x, c: x >= c, cand.astype(BF16))
            return jnp.where(cnt >= need, cand, v)
        return lax.fori_loop(0, 8, step, jnp.zeros((1, tq), F32))

    restrict(a1_ref, thr_hi, d1_ref)
    v1 = digit(need_hi)
    need_lo = need_hi - count_packed(a2_ref, lambda x, c: x > c, v1.astype(BF16))
    restrict(a2_ref, v1.astype(BF16), d0_ref)
    v0 = digit(need_lo)
    klo = (v1 * 256.0 + v0).astype(jnp.int32)
    lo = jnp.where(hbits >= 0x8000, 0xFFFF - klo, klo)
    thr = pltpu.bitcast(lax.shift_left(hbits, 16) | lo, F32)

    v0b = v0.astype(BF16)
    need = need_lo - count_packed(a2_ref, lambda x, c: x > c, v0b)
    n_tied = count_packed(a2_ref, lambda x, c: x == c, v0b)
    surplus = jnp.max(n_tied - need)

    def tie_search():
        def tie_tile(kt, _):
            ks = pl.multiple_of(kt * tk, tk)
            tie_ref[pl.ds(ks, tk), :] = jnp.where(sc_ref[pl.ds(ks, tk), :] == thr, (krow + ks).astype(F32), IDX_NONE)
            return 0

        lax.fori_loop(0, n_tiles, tie_tile, 0)

        def step(it, x):
            cand = x + lax.shift_left(jnp.int32(1), idx_bits - 1 - it).astype(F32)
            cnt = count(lambda ks: tie_ref[pl.ds(ks, tk), :] < cand)
            return jnp.where(cnt < need, cand, x)
        return lax.fori_loop(0, idx_bits, step, jnp.zeros((1, tq), F32)).astype(jnp.int32)

    last_tie = lax.cond(surplus > 0.0, tie_search, lambda: jnp.full((1, tq), sc_ref.shape[0], jnp.int32))

    def bias_tile(kt, _):
        ks = pl.multiple_of(kt * tk, tk)
        sc = sc_ref[pl.ds(ks, tk), :]
        key = krow + ks
        keep = ((sc > thr) | ((sc == thr) & (key <= last_tie))) & (key < limit)
        sc_ref[pl.ds(ks, tk), :] = jnp.where(keep, 0.0, NEG_LOGIT)
        return 0

    lax.fori_loop(0, n_tiles, bias_tile, 0)

    n_heads = dq_ref.shape[2] // HEAD_DIM
    qs = []
    for hd in range(n_heads):
        qfull = dq_ref[0, :, (hd // 2) * LANES:(hd // 2 + 1) * LANES]
        qs.append(jnp.where(_head_mask(hd % 2), qfull, jnp.zeros_like(qfull)))

    def qk(hd, kt):
        ks = pl.multiple_of(kt * tk, tk)
        lanes = slice((hd // 2) * LANES, (hd // 2 + 1) * LANES)
        return _qk(dk_ref[0, pl.ds(ks, tk), lanes], qs[hd]) + sc_ref[pl.ds(ks, tk), :]

    def values(hd, kt):
        return dvt_ref[0, hd * HEAD_DIM:(hd + 1) * HEAD_DIM, pl.ds(pl.multiple_of(kt * tk, tk), tk)]

    outs = _attend(n_heads, n_tiles, qk, values, scratch)
    for hp in range(n_heads // 2):
        o_ref[0, :, hp * LANES:(hp + 1) * LANES] = _pair_out(outs[2 * hp:2 * hp + 2])


def _dsa_attention(dq, dk, dvt, iq, ik, iw):
    b, s, w = dq.shape
    tq, tk = DSA_TQ, DSA_TK
    top_k = min(DSA_TOPK_MAX, s // 4)

    def qspec(width):
        return pl.BlockSpec((1, tq, width), lambda bi, qi: (bi, qi, 0))

    def kspec(width):
        return pl.BlockSpec((1, s, width), lambda bi, qi: (bi, 0, 0))

    return pl.pallas_call(
        functools.partial(_dsa_body, tq=tq, tk=tk, top_k=top_k),
        grid=(b, s // tq),
        in_specs=[qspec(w), qspec(iq.shape[2]), pl.BlockSpec((1, iw.shape[1], tq), lambda bi, qi: (bi, 0, qi)),
                  kspec(w), pl.BlockSpec((1, w, s), lambda bi, qi: (bi, 0, 0)), kspec(LANES)],
        out_specs=qspec(w),
        out_shape=jax.ShapeDtypeStruct((b, s, w), BF16),
        scratch_shapes=[pltpu.VMEM((s, tq), F32)] * 2 + [pltpu.VMEM((s, tq), BF16)] * 4
        + _attend_scratch(w // HEAD_DIM, tq, tk, HEAD_DIM),
        compiler_params=_cparams(("arbitrary", "arbitrary")),
        name="dsa_attention",
    )(dq, iq, iw, dk, dvt, ik)


MLA_DOWN_COLS = MLA_Q_LORA + MLA_KV_LORA + 2 * MLA_ROPE


def _mla_proj_body(h_ref, sh_ref, sc_ref, g_ref, wd_ref, qn_ref, kvn_ref, wuq_ref, wukv_ref, cos_ref, sin_ref,
                   qnope_ref, qrope_ref, knope_ref, v_ref, krope_ref, *, scale):
    x = h_ref[...]
    u = _modulated_norm(x, g_ref[...], sc_ref[0], sh_ref[0]).astype(BF16)
    cos = cos_ref[...]
    sin = sin_ref[...]
    half = MLA_ROPE // 2
    down = jnp.dot(u, wd_ref[...], preferred_element_type=F32)

    def norm(z, g):
        ms = jnp.mean(z * z, axis=-1, keepdims=True)
        return (z * lax.rsqrt(ms + RMS_EPS) * g).astype(BF16)

    cq = norm(down[:, 0:MLA_Q_LORA], qn_ref[...])
    ckv = norm(down[:, MLA_Q_LORA:MLA_Q_LORA + MLA_KV_LORA], kvn_ref[...])
    krope_ref[0] = _rope_slab(down[:, MLA_Q_LORA + MLA_KV_LORA:], cos, sin, half).astype(BF16)

    n_nope = MLA_HEADS * MLA_NOPE
    qn = jnp.dot(cq, wuq_ref[:, 0:n_nope], preferred_element_type=F32)
    qnope_ref[0] = (qn * scale).astype(BF16)
    qr = jnp.dot(cq, wuq_ref[:, n_nope:], preferred_element_type=F32)
    qr = jnp.concatenate(
        [_rope_slab(qr[:, j * LANES:(j + 1) * LANES], cos, sin, half) for j in range(qr.shape[1] // LANES)], axis=1)
    qrope_ref[0] = (qr * scale).astype(BF16)
    knope_ref[0] = jnp.dot(ckv, wukv_ref[:, 0:n_nope], preferred_element_type=F32).astype(BF16)
    v_ref[0] = jnp.dot(ckv, wukv_ref[:, n_nope:], preferred_element_type=F32).T.astype(BF16)


def _mla_proj(h, sh, sc, g, wd, qn, kvn, wuq, wukv, cos, sin, b, s):
    t, d = h.shape
    tpb = s // TM
    row = pl.BlockSpec((TM, d), lambda i: (i, 0))
    mod = pl.BlockSpec((1, 1, d), lambda i: (i // tpb, 0, 0))
    tab = pl.BlockSpec((TM, LANES), lambda i: (i, 0))

    def out(width):
        return pl.BlockSpec((1, TM, width), lambda i: (i // tpb, i % tpb, 0))

    def shp(width):
        return jax.ShapeDtypeStruct((b, s, width), BF16)

    n_nope = MLA_HEADS * MLA_NOPE
    n_rope = MLA_HEADS * MLA_ROPE
    n_v = MLA_HEADS * MLA_V
    scale = (MLA_NOPE + MLA_ROPE) ** -0.5 * LOG2E
    return pl.pallas_call(
        functools.partial(_mla_proj_body, scale=scale),
        grid=(t // TM,),
        in_specs=[row, mod, mod, _const_spec((1, d)), _const_spec((d, MLA_DOWN_COLS)),
                  _const_spec((1, MLA_Q_LORA)), _const_spec((1, MLA_KV_LORA)),
                  _const_spec((MLA_Q_LORA, n_nope + n_rope)), _const_spec((MLA_KV_LORA, n_nope + n_v)), tab, tab],
        out_specs=[out(n_nope), out(n_rope), out(n_nope),
                   pl.BlockSpec((1, n_v, TM), lambda i: (i // tpb, 0, i % tpb)), out(LANES)],
        out_shape=[shp(n_nope), shp(n_rope), shp(n_nope), jax.ShapeDtypeStruct((b, n_v, s), BF16), shp(LANES)],
        compiler_params=_cparams(("arbitrary",)),
        name="mla_in_proj",
    )(h, sh, sc, g, wd, qn, kvn, wuq, wukv, cos, sin)


def _mla_body(qn_ref, qr_ref, kn_ref, kr_ref, vt_ref, o_ref, *scratch, tq, tk):
    n_heads = qn_ref.shape[2] // MLA_NOPE
    qi = pl.program_id(2)
    qs = []
    for hd in range(n_heads):
        qr = qr_ref[0, :, (hd // 2) * LANES:(hd // 2 + 1) * LANES]
        qs.append(jnp.concatenate([qn_ref[0, :, hd * LANES:(hd + 1) * LANES],
                                   jnp.where(_head_mask(hd % 2), qr, jnp.zeros_like(qr))], axis=1))
    krow = lax.broadcasted_iota(jnp.int32, (tk, tq), 0)
    qcol = lax.broadcasted_iota(jnp.int32, (tk, tq), 1)

    def qk(hd, kt):
        ks = pl.multiple_of(kt * tk, tk)
        k = jnp.concatenate([kn_ref[0, pl.ds(ks, tk), hd * LANES:(hd + 1) * LANES], kr_ref[0, pl.ds(ks, tk), :]],
                            axis=1)
        return _qk(k, qs[hd])

    def values(hd, kt):
        return vt_ref[0, hd * LANES:(hd + 1) * LANES, pl.ds(pl.multiple_of(kt * tk, tk), tk)]

    def chunk_causal(s):
        return jnp.where(krow // CHUNK <= qcol // CHUNK, s, NEG_LOGIT)

    outs = _attend(n_heads, qi + 1, qk, values, scratch, mask_last=chunk_causal)
    for hd in range(n_heads):
        o_ref[0, :, hd * LANES:(hd + 1) * LANES] = outs[hd].T.astype(BF16)


def _mla_attention(qnope, qrope, knope, krope, vt):
    b, s, w = qnope.shape
    tq = tk = ATT_TQ
    gw = ATT_HEADS * MLA_NOPE
    gr = ATT_HEADS * MLA_ROPE
    qspec = pl.BlockSpec((1, tq, gw), lambda bi, hg, qi: (bi, qi, hg))
    qrspec = pl.BlockSpec((1, tq, gr), lambda bi, hg, qi: (bi, qi, hg))
    kspec = pl.BlockSpec((1, s, gw), lambda bi, hg, qi: (bi, 0, hg))
    krspec = pl.BlockSpec((1, s, LANES), lambda bi, hg, qi: (bi, 0, 0))
    vspec = pl.BlockSpec((1, gw, s), lambda bi, hg, qi: (bi, hg, 0))
    return pl.pallas_call(
        functools.partial(_mla_body, tq=tq, tk=tk),
        grid=(b, w // gw, s // tq),
        in_specs=[qspec, qrspec, kspec, krspec, vspec],
        out_specs=qspec,
        out_shape=jax.ShapeDtypeStruct((b, s, w), BF16),
        scratch_shapes=_attend_scratch(ATT_HEADS, tq, tk, LANES),
        compiler_params=_cparams(("arbitrary", "arbitrary", "arbitrary")),
        name="mla_attention",
    )(qnope, qrope, knope, krope, vt)


def _hyb_weight(w_in, b_f):
    o = 0
    parts = {}
    for name, width in (("fq", 512), ("fk", 512), ("fv", 512), ("ff", FOX_HEADS), ("dq", 512), ("dk", 512),
                        ("dv", 512), ("iq", 256), ("iw", IDX_HEADS), ("ik", HEAD_DIM)):
        parts[name] = w_in[:, o:o + width]
        o += width
    qs = HEAD_DIM ** -0.5
    pad = jnp.zeros((w_in.shape[0], LANES - FOX_HEADS - IDX_HEADS), w_in.dtype)
    w = jnp.concatenate([parts["fq"] * qs, parts["fk"], parts["fv"], parts["dq"] * qs, parts["dk"], parts["dv"],
                         parts["iq"] * qs, parts["ik"], parts["ik"],
                         parts["ff"], parts["iw"] * IDX_HEADS ** -0.5, pad], axis=1)
    bf = jnp.concatenate([b_f, jnp.zeros((LANES - FOX_HEADS,), b_f.dtype)]).reshape(1, LANES)
    return w.astype(BF16), bf.astype(F32)


def _mla_weights(w_down, w_uq, w_ukv):
    kr = w_down[:, MLA_Q_LORA + MLA_KV_LORA:]
    wd = jnp.concatenate([w_down, kr], axis=1)
    uq = w_uq.reshape(MLA_Q_LORA, MLA_HEADS, MLA_NOPE + MLA_ROPE)
    wuq = jnp.concatenate([uq[:, :, :MLA_NOPE].reshape(MLA_Q_LORA, -1), uq[:, :, MLA_NOPE:].reshape(MLA_Q_LORA, -1)],
                          axis=1)
    ukv = w_ukv.reshape(MLA_KV_LORA, MLA_HEADS, MLA_NOPE + MLA_V)
    wukv = jnp.concatenate([ukv[:, :, :MLA_NOPE].reshape(MLA_KV_LORA, -1),
                            ukv[:, :, MLA_NOPE:].reshape(MLA_KV_LORA, -1)], axis=1)
    return wd.astype(BF16), wuq.astype(BF16), wukv.astype(BF16)


def kernel(x, c, positions, ada_w, ada_b, norm_g, ffn_w_gate, ffn_w_up, ffn_w_down, hyb_w_in, fox_b_f, hyb_w_out,
           mla_w_down, mla_q_norm, mla_kv_norm, mla_w_uq, mla_w_ukv, mla_w_out, final_g):
    b, s, d = x.shape
    depth = ada_w.shape[0]
    t = b * s
    assert s % TM == 0 and s % ATT_TQ == 0 and s % DSA_TK == 0 and ffn_w_gate.shape[-1] % FFN_CHUNK == 0

    mod = _modulation(c, ada_w, ada_b).reshape(depth, b, N_MOD, 1, d)
    cos_d, sin_d, cos_m, sin_m = _rope_tables(positions)
    wg = ffn_w_gate.astype(BF16)
    wu = ffn_w_up.astype(BF16)
    wdn = ffn_w_down.astype(BF16)

    h = x.reshape(t, d)
    for i in range(depth):
        sh1, sc1, g1, sh2, sc2, g2, sh3, sc3, g3 = [mod[i, :, j] for j in range(N_MOD)]
        ng = norm_g[i].reshape(3, 1, d)
        h = _ffn(h, sh1, sc1, g1, ng[0], wg[i, 0], wu[i, 0], wdn[i, 0], s)
        j = i // 2
        if i % 2 == 0:
            w_in, bf = _hyb_weight(hyb_w_in[j], fox_b_f[j])
            fq, fk, fvt, dq, dk, dvt, iq, ik, cparts, iw = _hyb_proj(h, sh2, sc2, ng[1], w_in, bf, cos_d, sin_d, b, s)
            out_a = _fox_attention(fq, fk, cparts, fvt)
            out_b = _dsa_attention(dq, dk, dvt, iq, ik, iw)
            w_out = hyb_w_out[j].astype(BF16)
            half = out_a.shape[2]
            mix = (g2, [out_a.reshape(t, half), out_b.reshape(t, half)], [w_out[:half], w_out[half:]])
        else:
            wd, wuq, wukv = _mla_weights(mla_w_down[j], mla_w_uq[j], mla_w_ukv[j])
            qn, qr, kn, vt, kr = _mla_proj(h, sh2, sc2, ng[1], wd, mla_q_norm[j].reshape(1, -1),
                                           mla_kv_norm[j].reshape(1, -1), wuq, wukv, cos_m, sin_m, b, s)
            out = _mla_attention(qn, qr, kn, kr, vt)
            mix = (g2, [out.reshape(t, -1)], [mla_w_out[j].astype(BF16)])
        last = i == depth - 1
        h = _ffn(h, sh3, sc3, g3, ng[2], wg[i, 1], wu[i, 1], wdn[i, 1], s,
                 final_g=final_g.reshape(1, d) if last else None, mix=mix)
    return h.reshape(b, s, d)
```

```python
import functools
import math

import jax
import jax.numpy as jnp
from jax import lax
from jax.experimental import pallas as pl
from jax.experimental.pallas import tpu as pltpu

F32 = jnp.float32
BF16 = jnp.bfloat16

CHUNK = 64
RMS_EPS = 1e-6
ROPE_THETA = 500000.0
MLA_ROPE_THETA = 10000.0
MACARON_WEIGHT = 0.5
N_MOD = 9
FOX_HEADS = 8
HEAD_DIM = 64
DSA_ROT = 16
IDX_HEADS = 4
DSA_TOPK_MAX = 256
MLA_HEADS = 8
MLA_NOPE = 128
MLA_ROPE = 64
MLA_V = 128
MLA_Q_LORA = 384
MLA_KV_LORA = 256

LANES = 128
VMEM_LIMIT = 56 * 1024 * 1024

LOG2E = math.log2(math.e)
NEG_LOGIT = -1e30
NEG_SCORE = -3e38

TM = 512
FFN_CHUNK = 256
ATT_TQ = 512
ATT_TK = 512
ATT_HEADS = 4
FOX_STEP_HEADS = 8
DSA_TQ = 512
DSA_ATT_TQ = 256
DSA_TK = 512
COUNT_ROWS = 32


def _cparams(sem):
    return pltpu.CompilerParams(dimension_semantics=sem, vmem_limit_bytes=VMEM_LIMIT)


def _const_spec(shape):
    nd = len(shape)
    return pl.BlockSpec(shape, lambda *_: (0,) * nd, pipeline_mode=pl.Buffered(1))


def _silu(x):
    return x * jax.nn.sigmoid(x)


def _modulated_norm(x, g, sc, sh):
    ms = jnp.mean(x * x, axis=-1, keepdims=True)
    y = x * lax.rsqrt(ms + RMS_EPS) * g
    return y * (1.0 + sc) + sh


def _mod_body(c_ref, w_ref, b_ref, o_ref):
    cond = _silu(c_ref[...]).astype(BF16)
    o_ref[0] = jnp.dot(cond, w_ref[0].astype(BF16), preferred_element_type=F32) + b_ref[0]


def _modulation(c, ada_w, ada_b):
    depth, d, n = ada_w.shape
    b = c.shape[0]
    tn = n // N_MOD
    return pl.pallas_call(
        _mod_body,
        grid=(depth, n // tn),
        in_specs=[pl.BlockSpec((b, d), lambda i, j: (0, 0)),
                  pl.BlockSpec((1, d, tn), lambda i, j: (i, 0, j)),
                  pl.BlockSpec((1, 1, tn), lambda i, j: (i, 0, j))],
        out_specs=pl.BlockSpec((1, b, tn), lambda i, j: (i, 0, j)),
        out_shape=jax.ShapeDtypeStruct((depth, b, n), F32),
        compiler_params=_cparams(("arbitrary", "arbitrary")),
        name="adaln_mod",
    )(c, ada_w, ada_b.reshape(depth, 1, n))


def _rope_table_body(pos_ref, invd_ref, sgnd_ref, invm_ref, sgnm_ref, cd_ref, sd_ref, cm_ref, sm_ref):
    pos = pos_ref[...].astype(F32)
    for inv_ref, sgn_ref, c_ref, s_ref in ((invd_ref, sgnd_ref, cd_ref, sd_ref),
                                           (invm_ref, sgnm_ref, cm_ref, sm_ref)):
        ang = pos * inv_ref[...]
        sgn = sgn_ref[...]
        c_ref[...] = jnp.where(sgn != 0.0, jnp.cos(ang), 1.0)
        s_ref[...] = sgn * jnp.sin(ang)


def _lane_pattern(rot, theta):
    half = rot // 2
    inv_freq = jnp.exp(-math.log(theta) * 2.0 * jnp.arange(half, dtype=F32) / rot)
    d = jnp.arange(LANES) % HEAD_DIM
    inv = jnp.where(d < rot, inv_freq[d % half], 0.0).astype(F32)
    sgn = jnp.where(d < half, -1.0, jnp.where(d < rot, 1.0, 0.0)).astype(F32)
    return inv.reshape(1, LANES), sgn.reshape(1, LANES)


def _rope_tables(positions):
    t = positions.size
    invd, sgnd = _lane_pattern(DSA_ROT, ROPE_THETA)
    invm, sgnm = _lane_pattern(MLA_ROPE, MLA_ROPE_THETA)
    tm = 1024
    row = pl.BlockSpec((tm, LANES), lambda i: (i, 0))
    vec = pl.BlockSpec((1, LANES), lambda i: (0, 0))
    tab = jax.ShapeDtypeStruct((t, LANES), F32)
    return pl.pallas_call(
        _rope_table_body,
        grid=(t // tm,),
        in_specs=[pl.BlockSpec((tm, 1), lambda i: (i, 0)), vec, vec, vec, vec],
        out_specs=[row, row, row, row],
        out_shape=[tab, tab, tab, tab],
        compiler_params=_cparams(("arbitrary",)),
        name="rope_tables",
    )(positions.reshape(t, 1), invd, sgnd, invm, sgnm)


def _rope_slab(y, cos, sin, half):
    lane = lax.broadcasted_iota(jnp.int32, (1, LANES), 1) % HEAD_DIM
    first = lane < half
    partner = jnp.where(first, pltpu.roll(y, LANES - half, 1), pltpu.roll(y, half, 1))
    return y * cos + partner * sin


def _ffn_body(h_ref, sh_ref, sc_ref, gt_ref, g_ref, wg_ref, wu_ref, wd_ref, *rest, nf, fc, final, n_mix):
    rest = list(rest)
    x = h_ref[...]
    if n_mix:
        mg_ref = rest.pop(0)
        mix = None
        for x_ref, w_ref in zip(rest[:n_mix], rest[n_mix:2 * n_mix]):
            part = jnp.dot(x_ref[...], w_ref[...], preferred_element_type=F32)
            mix = part if mix is None else mix + part
        x = x + mg_ref[0] * mix
        rest = rest[2 * n_mix:]
    if final:
        fg_ref, o_ref, acc_ref = rest
    else:
        o_ref, acc_ref = rest
    u = _modulated_norm(x, g_ref[...], sc_ref[0], sh_ref[0]).astype(BF16)
    for f in range(nf):
        sl = slice(f * fc, (f + 1) * fc)
        gp = jnp.dot(u, wg_ref[:, sl], preferred_element_type=F32)
        up = jnp.dot(u, wu_ref[:, sl], preferred_element_type=F32)
        a = (_silu(gp) * up).astype(BF16)
        d = jnp.dot(a, wd_ref[sl, :], preferred_element_type=F32)
        if f == 0:
            acc_ref[...] = d
        else:
            acc_ref[...] += d
    y = x + (MACARON_WEIGHT * gt_ref[0]) * acc_ref[...]
    if final:
        ms = jnp.mean(y * y, axis=-1, keepdims=True)
        y = y * lax.rsqrt(ms + RMS_EPS) * fg_ref[...]
    o_ref[...] = y


def _ffn(h, sh, sc, gt, g, wg, wu, wd, s, final_g=None, mix=None):
    t, d = h.shape
    f = wg.shape[1]
    tpb = s // TM
    row = pl.BlockSpec((TM, d), lambda i: (i, 0))
    mod = pl.BlockSpec((1, 1, d), lambda i: (i // tpb, 0, 0))
    in_specs = [row, mod, mod, mod, _const_spec((1, d)),
                _const_spec((d, f)), _const_spec((d, f)), _const_spec((f, d))]
    args = [h, sh, sc, gt, g, wg, wu, wd]
    n_mix = 0
    if mix is not None:
        mgate, xs, ws = mix
        n_mix = len(xs)
        in_specs += [mod] + [pl.BlockSpec((TM, x.shape[1]), lambda i: (i, 0)) for x in xs] + [_const_spec(w.shape) for w in ws]
        args += [mgate, *xs, *ws]
    if final_g is not None:
        in_specs.append(_const_spec((1, d)))
        args.append(final_g)
    return pl.pallas_call(
        functools.partial(_ffn_body, nf=f // FFN_CHUNK, fc=FFN_CHUNK, final=final_g is not None, n_mix=n_mix),
        grid=(t // TM,),
        in_specs=in_specs,
        out_specs=row,
        out_shape=jax.ShapeDtypeStruct((t, d), F32),
        scratch_shapes=[pltpu.VMEM((TM, d), F32)],
        compiler_params=_cparams(("arbitrary",)),
        name="swiglu_half_step",
    )(*args)


HYB_W = 512
HYB_COLS = 6 * HYB_W + 256 + 128 + 128


def _hyb_proj_body(h_ref, sh_ref, sc_ref, g_ref, w_ref, bf_ref, cos_ref, sin_ref,
                   fq_ref, fk_ref, fvt_ref, dq_ref, dk_ref, dvt_ref, iq_ref, ik_ref, cum_ref, iw_ref,
                   carry_ref, *, tpb):
    i = pl.program_id(0)
    x = h_ref[...]
    tm = x.shape[0]
    u = _modulated_norm(x, g_ref[...], sc_ref[0], sh_ref[0]).astype(BF16)
    cos = cos_ref[...]
    sin = sin_ref[...]
    half = DSA_ROT // 2

    def proj(c0, width):
        return jnp.dot(u, w_ref[:, c0:c0 + width], preferred_element_type=F32)

    def roped(y):
        return jnp.concatenate(
            [_rope_slab(y[:, j * LANES:(j + 1) * LANES], cos, sin, half) for j in range(y.shape[1] // LANES)],
            axis=1)

    fq_ref[0] = (proj(0 * HYB_W, HYB_W) * LOG2E).astype(BF16)
    fk_ref[0] = proj(1 * HYB_W, HYB_W).astype(BF16)
    fvt_ref[0] = proj(2 * HYB_W, HYB_W).T.astype(BF16)
    dq_ref[0] = (roped(proj(3 * HYB_W, HYB_W)) * LOG2E).astype(BF16)
    dk_ref[0] = roped(proj(4 * HYB_W, HYB_W)).astype(BF16)
    dvt_ref[0] = proj(5 * HYB_W, HYB_W).T.astype(BF16)
    tail = proj(6 * HYB_W, 512)
    iq_ref[0] = roped(tail[:, 0:256]).astype(BF16)
    ik_ref[0] = roped(tail[:, 256:384]).astype(BF16)
    gates = tail[:, 384:512]
    iw_ref[0] = gates.T[FOX_HEADS:2 * FOX_HEADS, :]

    z = gates + bf_ref[...]
    logf = jnp.minimum(z, 0.0) - jnp.log1p(jnp.exp(-jnp.abs(z)))
    rows = lax.broadcasted_iota(jnp.int32, (tm, LANES), 0)
    c = logf
    k = 1
    while k < tm:
        c = c + jnp.where(rows >= k, pltpu.roll(c, k, 0), 0.0)
        k *= 2

    @pl.when(i % tpb == 0)
    def _():
        carry_ref[...] = jnp.zeros_like(carry_ref)

    c = c + carry_ref[0:1, :]
    carry_ref[...] = jnp.broadcast_to(c[tm - 1:tm, :], carry_ref.shape)
    lane = lax.broadcasted_iota(jnp.int32, (1, LANES), 1)
    c = jnp.where(lane < FOX_HEADS, c * LOG2E, 0.0)
    hi = c.astype(BF16).astype(F32)
    mid = (c - hi).astype(BF16).astype(F32)
    lo = (c - hi - mid).astype(BF16).astype(F32)
    cum_ref[0] = (hi + pltpu.roll(mid, FOX_HEADS, 1) + pltpu.roll(lo, 2 * FOX_HEADS, 1)).astype(BF16)


def _hyb_proj(h, sh, sc, g, w, bf, cos, sin, b, s):
    t, d = h.shape
    tpb = s // TM
    row = pl.BlockSpec((TM, d), lambda i: (i, 0))
    mod = pl.BlockSpec((1, 1, d), lambda i: (i // tpb, 0, 0))
    tab = pl.BlockSpec((TM, LANES), lambda i: (i, 0))

    def out(width):
        return pl.BlockSpec((1, TM, width), lambda i: (i // tpb, i % tpb, 0))

    def shp(width, dt=BF16):
        return jax.ShapeDtypeStruct((b, s, width), dt)

    def out_t(width):
        return pl.BlockSpec((1, width, TM), lambda i: (i // tpb, 0, i % tpb))

    def shp_t(width, dt=BF16):
        return jax.ShapeDtypeStruct((b, width, s), dt)

    return pl.pallas_call(
        functools.partial(_hyb_proj_body, tpb=tpb),
        grid=(t // TM,),
        in_specs=[row, mod, mod, _const_spec((1, d)), _const_spec((d, HYB_COLS)), _const_spec((1, LANES)), tab, tab],
        out_specs=[out(HYB_W), out(HYB_W), out_t(HYB_W), out(HYB_W), out(HYB_W), out_t(HYB_W),
                   out(256), out(LANES), out(LANES), out_t(FOX_HEADS)],
        out_shape=[shp(HYB_W), shp(HYB_W), shp_t(HYB_W), shp(HYB_W), shp(HYB_W), shp_t(HYB_W),
                   shp(256), shp(LANES), shp(LANES), shp_t(FOX_HEADS, F32)],
        scratch_shapes=[pltpu.VMEM((8, LANES), F32)],
        compiler_params=_cparams(("arbitrary",)),
        name="hybrid_in_proj",
    )(h, sh, sc, g, w, bf, cos, sin)


def _attend_scratch(n_heads, tq, tk, n):
    return [pltpu.VMEM((2, n_heads, tk, tq), F32), pltpu.VMEM((2, n_heads, 1, tq), F32),
            pltpu.VMEM((n_heads, 1, tq), F32), pltpu.VMEM((n_heads, 1, tq), F32), pltpu.VMEM((n_heads, n, tq), F32)]


def _attend(n_heads, n_tiles, qk, values, scratch, mask_last=None):
    s_ref, smax_ref, m_ref, l_ref, acc_ref = scratch

    def stage(slot, kt):
        for hd in range(n_heads):
            s = qk(hd, kt)
            s_ref[slot, hd] = s
            smax_ref[slot, hd] = jnp.max(s, axis=0, keepdims=True)

    def step(slot, kt, mask=None):
        for hd in range(n_heads):
            s = s_ref[slot, hd]
            smax = smax_ref[slot, hd]
            if mask is not None:
                s = mask(s)
                smax = jnp.max(s, axis=0, keepdims=True)
            m = m_ref[hd]
            m_new = jnp.maximum(m, smax)
            alpha = jnp.exp2(m - m_new)
            p = jnp.exp2(s - m_new)
            m_ref[hd] = m_new
            l_ref[hd] = alpha * l_ref[hd] + jnp.sum(p, axis=0, keepdims=True)
            acc_ref[hd] = alpha * acc_ref[hd] + jnp.dot(values(hd, kt), p.astype(BF16), preferred_element_type=F32)

    m_ref[...] = jnp.full(m_ref.shape, NEG_LOGIT, F32)
    l_ref[...] = jnp.zeros(l_ref.shape, F32)
    acc_ref[...] = jnp.zeros(acc_ref.shape, F32)
    stage(0, 0)
    n_pairs = (n_tiles - 1) // 2

    def pair(i, _):
        stage(1, 2 * i + 1)
        step(0, 2 * i)
        stage(0, 2 * i + 2)
        step(1, 2 * i + 1)
        return 0

    lax.fori_loop(0, n_pairs, pair, 0)
    odd_tail = n_tiles - 2 * n_pairs == 2

    @pl.when(odd_tail)
    def _():
        stage(1, n_tiles - 1)
        step(0, n_tiles - 2)

    step(jnp.where(odd_tail, 1, 0), n_tiles - 1, mask=mask_last)
    return [acc_ref[hd] / l_ref[hd] for hd in range(n_heads)]


def _head_mask(j):
    lane = lax.broadcasted_iota(jnp.int32, (1, LANES), 1)
    return (lane // HEAD_DIM) == j


def _qk(q, k):
    return lax.dot_general(q, k, (((1,), (1,)), ((), ())), preferred_element_type=F32)


def _pair_out(outs):
    return jnp.concatenate(outs, axis=0).T.astype(BF16)


def _fox_body(q_ref, k_ref, c_ref, vt_ref, o_ref, *scratch, tq, tk):
    n_heads = q_ref.shape[2] // HEAD_DIM
    hg = pl.program_id(1)
    qi = pl.program_id(2)
    krow = lax.broadcasted_iota(jnp.int32, (tk, tq), 0)
    qcol = lax.broadcasted_iota(jnp.int32, (tk, tq), 1)
    lane = lax.broadcasted_iota(jnp.int32, (tq, LANES), 1)
    qs = []
    for hd in range(n_heads):
        pick = (lane % FOX_HEADS == hg * n_heads + hd) & (lane < 3 * FOX_HEADS)
        sel = jnp.where(pick, -1.0, 0.0).astype(BF16)
        qfull = q_ref[0, :, (hd // 2) * LANES:(hd // 2 + 1) * LANES]
        qs.append(jnp.concatenate([jnp.where(_head_mask(hd % 2), qfull, jnp.zeros_like(qfull)), sel], axis=1))

    def qk(hd, kt):
        ks = pl.multiple_of(kt * tk, tk)
        lanes = slice((hd // 2) * LANES, (hd // 2 + 1) * LANES)
        k = jnp.concatenate([k_ref[0, pl.ds(ks, tk), lanes], c_ref[0, pl.ds(ks, tk), :]], axis=1)
        return _qk(k, qs[hd])

    def values(hd, kt):
        return vt_ref[0, hd * HEAD_DIM:(hd + 1) * HEAD_DIM, pl.ds(pl.multiple_of(kt * tk, tk), tk)]

    def causal(s):
        return jnp.where(krow <= qcol, s, NEG_LOGIT)

    outs = _attend(n_heads, qi + 1, qk, values, scratch, mask_last=causal)
    for hp in range(n_heads // 2):
        o_ref[0, :, hp * LANES:(hp + 1) * LANES] = _pair_out(outs[2 * hp:2 * hp + 2])


def _fox_attention(fq, fk, cparts, fvt):
    b, s, w = fq.shape
    tq = tk = ATT_TQ
    gw = FOX_STEP_HEADS * HEAD_DIM
    qspec = pl.BlockSpec((1, tq, gw), lambda bi, hg, qi: (bi, qi, hg))
    kspec = pl.BlockSpec((1, s, gw), lambda bi, hg, qi: (bi, 0, hg))
    cspec = pl.BlockSpec((1, s, LANES), lambda bi, hg, qi: (bi, 0, 0))
    vspec = pl.BlockSpec((1, gw, s), lambda bi, hg, qi: (bi, hg, 0))
    return pl.pallas_call(
        functools.partial(_fox_body, tq=tq, tk=tk),
        grid=(b, w // gw, s // tq),
        in_specs=[qspec, kspec, cspec, vspec],
        out_specs=qspec,
        out_shape=jax.ShapeDtypeStruct((b, s, w), BF16),
        scratch_shapes=_attend_scratch(FOX_STEP_HEADS, tq, tk, HEAD_DIM),
        compiler_params=_cparams(("arbitrary", "arbitrary", "arbitrary")),
        name="fox_attention",
    )(fq, fk, cparts, fvt)


def _hi_key_to_bits(h):
    return jnp.where(h >= 0x8000, h - 0x8000, (~h) & 0xFFFF)


def _tree_sum(parts):
    while len(parts) > 1:
        parts = [a + b for a, b in zip(parts[0::2], parts[1::2])] + ([parts[-1]] if len(parts) % 2 else [])
    return parts[0]


def _dsa_body(dq_ref, iq_ref, iw_ref, dk_ref, dvt_ref, ik_ref, o_ref, sc_ref, a1_ref, d1_ref, d0_ref, a2_ref,
              *scratch, tq, tqa, tk, top_k):
    qi = pl.program_id(1)
    idx_bits = max(1, (sc_ref.shape[0] - 1).bit_length())
    n_tiles = ((qi + 1) * tq + tk - 1) // tk
    qpos = lax.broadcasted_iota(jnp.int32, (1, tq), 1) + qi * tq
    limit = (qpos // CHUNK + 1) * CHUNK
    krow = lax.broadcasted_iota(jnp.int32, (tk, tq), 0)
    kf = jnp.float32(top_k)

    iqf = iq_ref[0]
    iw = iw_ref[0]
    iqs = [jnp.where(_head_mask(h % 2), iqf[:, (h // 2) * LANES:(h // 2 + 1) * LANES], jnp.zeros((tq, LANES), BF16))
           for h in range(IDX_HEADS)]
    ws = [iw[h:h + 1, :] for h in range(IDX_HEADS)]

    def score_tile(kt, _):
        ks = pl.multiple_of(kt * tk, tk)
        ik = ik_ref[0, pl.ds(ks, tk), :]
        sc = None
        for h in range(IDX_HEADS):
            term = jnp.maximum(_qk(ik, iqs[h]), 0.0) * ws[h]
            sc = term if sc is None else sc + term
        sc = jnp.where(krow + ks < limit, sc, NEG_SCORE)
        sc = jnp.where(sc == 0.0, 0.0, sc)
        sc_ref[pl.ds(ks, tk), :] = sc
        bits = pltpu.bitcast(sc, jnp.int32)
        a1_ref[pl.ds(ks, tk), :] = pltpu.bitcast(bits & jnp.int32(-65536), F32).astype(BF16)
        lo = bits & 0xFFFF
        klo = jnp.where(bits < 0, 0xFFFF - lo, lo)
        d1_ref[pl.ds(ks, tk), :] = lax.shift_right_logical(klo, 8).astype(F32).astype(BF16)
        d0_ref[pl.ds(ks, tk), :] = (klo & 0xFF).astype(F32).astype(BF16)
        return 0

    lax.fori_loop(0, n_tiles, score_tile, 0)

    def count(pred):
        def body(kt, cnt):
            ks = pl.multiple_of(kt * tk, tk)
            c = jnp.where(pred(ks), 1.0, 0.0)
            return cnt + jnp.sum(c.reshape(tk // COUNT_ROWS, COUNT_ROWS, tq), axis=0)
        cnt = lax.fori_loop(0, n_tiles, body, jnp.zeros((COUNT_ROWS, tq), F32))
        return jnp.sum(cnt, axis=0, keepdims=True)

    def count_packed(ref, op, ref_b):
        def body(kt, cnt):
            x = ref[pl.ds(pl.multiple_of(kt * tk, tk), tk), :]
            c = jnp.where(op(x, ref_b), jnp.ones((), BF16), jnp.zeros((), BF16))
            part = _tree_sum([c[i * COUNT_ROWS:(i + 1) * COUNT_ROWS, :] for i in range(tk // COUNT_ROWS)])
            return cnt + part.astype(F32)
        cnt = lax.fori_loop(0, n_tiles, body, jnp.zeros((COUNT_ROWS, tq), F32))
        return jnp.sum(cnt, axis=0, keepdims=True)

    def hi_float(h):
        return pltpu.bitcast(lax.shift_left(_hi_key_to_bits(h), 16), F32).astype(BF16)

    def hi_step(it, h):
        cand = h | lax.shift_left(jnp.int32(1), 15 - it)
        cnt = count_packed(a1_ref, lambda x, c: x >= c, hi_float(cand))
        return jnp.where(cnt >= kf, cand, h)

    h = lax.fori_loop(0, 16, hi_step, jnp.zeros((1, tq), jnp.int32))
    hbits = _hi_key_to_bits(h)
    hbits = jnp.where((hbits & 0x7F80) == 0, hbits & 0x8000, hbits)
    thr_hi = pltpu.bitcast(lax.shift_left(hbits, 16), F32).astype(BF16)
    need_hi = kf - count_packed(a1_ref, lambda x, c: x > c, thr_hi)

    def restrict(src_ref, match, digit_ref):
        def body(kt, _):
            rows = pl.ds(pl.multiple_of(kt * tk, tk), tk)
            a2_ref[rows, :] = jnp.where(src_ref[rows, :] == match, digit_ref[rows, :], -jnp.ones((), BF16))
            return 0
        lax.fori_loop(0, n_tiles, body, 0)

    def digit(need):
        def step(it, v):
            cand = v + lax.shift_left(jnp.int32(1), 7 - it).astype(F32)
            cnt = count_packed(a2_ref, lambda x, c: x >= c, cand.astype(BF16))
            return jnp.where(cnt >= need, cand, v)
        return lax.fori_loop(0, 8, step, jnp.zeros((1, tq), F32))

    restrict(a1_ref, thr_hi, d1_ref)
    v1 = digit(need_hi)
    need_lo = need_hi - count_packed(a2_ref, lambda x, c: x > c, v1.astype(BF16))
    restrict(a2_ref, v1.astype(BF16), d0_ref)
    v0 = digit(need_lo)
    klo = (v1 * 256.0 + v0).astype(jnp.int32)
    lo = jnp.where(hbits >= 0x8000, 0xFFFF - klo, klo)
    thr = pltpu.bitcast(lax.shift_left(hbits, 16) | lo, F32)

    v0b = v0.astype(BF16)
    need = need_lo - count_packed(a2_ref, lambda x, c: x > c, v0b)
    n_tied = count_packed(a2_ref, lambda x, c: x == c, v0b)
    surplus = jnp.max(n_tied - need)

    def tie_search():
        def step(it, x):
            cand = x + lax.shift_left(jnp.int32(1), idx_bits - 1 - it)
            cnt = count(lambda ks: (sc_ref[pl.ds(ks, tk), :] == thr) & (krow + ks < cand))
            return jnp.where(cnt < need, cand, x)
        return lax.fori_loop(0, idx_bits, step, jnp.zeros((1, tq), jnp.int32))

    last_tie = lax.cond(surplus > 0.0, tie_search, lambda: jnp.full((1, tq), sc_ref.shape[0], jnp.int32))

    def bias_tile(kt, _):
        ks = pl.multiple_of(kt * tk, tk)
        sc = sc_ref[pl.ds(ks, tk), :]
        key = krow + ks
        keep = ((sc > thr) | ((sc == thr) & (key <= last_tie))) & (key < limit)
        sc_ref[pl.ds(ks, tk), :] = jnp.where(keep, 0.0, NEG_LOGIT)
        return 0

    lax.fori_loop(0, n_tiles, bias_tile, 0)

    n_heads = dq_ref.shape[2] // HEAD_DIM

    def values(hd, kt):
        return dvt_ref[0, hd * HEAD_DIM:(hd + 1) * HEAD_DIM, pl.ds(pl.multiple_of(kt * tk, tk), tk)]

    for sb in range(tq // tqa):
        qrows = slice(sb * tqa, (sb + 1) * tqa)
        qs = []
        for hd in range(n_heads):
            qfull = dq_ref[0, qrows, (hd // 2) * LANES:(hd // 2 + 1) * LANES]
            qs.append(jnp.where(_head_mask(hd % 2), qfull, jnp.zeros_like(qfull)))

        def qk(hd, kt, qs=qs, qrows=qrows):
            ks = pl.multiple_of(kt * tk, tk)
            lanes = slice((hd // 2) * LANES, (hd // 2 + 1) * LANES)
            return _qk(dk_ref[0, pl.ds(ks, tk), lanes], qs[hd]) + sc_ref[pl.ds(ks, tk), qrows]

        sub_tiles = (qi * tq + (sb + 1) * tqa + tk - 1) // tk
        outs = _attend(n_heads, sub_tiles, qk, values, scratch)
        for hp in range(n_heads // 2):
            o_ref[0, qrows, hp * LANES:(hp + 1) * LANES] = _pair_out(outs[2 * hp:2 * hp + 2])


def _dsa_attention(dq, dk, dvt, iq, ik, iw):
    b, s, w = dq.shape
    tq, tqa, tk = DSA_TQ, DSA_ATT_TQ, DSA_TK
    top_k = min(DSA_TOPK_MAX, s // 4)
    once = pl.Buffered(1)

    def qspec(width):
        return pl.BlockSpec((1, tq, width), lambda bi, qi: (bi, qi, 0))

    def kspec(width):
        return pl.BlockSpec((1, s, width), lambda bi, qi: (bi, 0, 0), pipeline_mode=once)

    return pl.pallas_call(
        functools.partial(_dsa_body, tq=tq, tqa=tqa, tk=tk, top_k=top_k),
        grid=(b, s // tq),
        in_specs=[qspec(w), qspec(iq.shape[2]), pl.BlockSpec((1, iw.shape[1], tq), lambda bi, qi: (bi, 0, qi)),
                  kspec(w), pl.BlockSpec((1, w, s), lambda bi, qi: (bi, 0, 0), pipeline_mode=once), kspec(LANES)],
        out_specs=qspec(w),
        out_shape=jax.ShapeDtypeStruct((b, s, w), BF16),
        scratch_shapes=[pltpu.VMEM((s, tq), F32)] + [pltpu.VMEM((s, tq), BF16)] * 4
        + _attend_scratch(w // HEAD_DIM, tqa, tk, HEAD_DIM),
        compiler_params=_cparams(("arbitrary", "arbitrary")),
        name="dsa_attention",
    )(dq, iq, iw, dk, dvt, ik)


MLA_DOWN_COLS = MLA_Q_LORA + MLA_KV_LORA + 2 * MLA_ROPE


def _mla_proj_body(h_ref, sh_ref, sc_ref, g_ref, wd_ref, qn_ref, kvn_ref, wuq_ref, wukv_ref, cos_ref, sin_ref,
                   qnope_ref, qrope_ref, knope_ref, v_ref, krope_ref, *, scale):
    x = h_ref[...]
    u = _modulated_norm(x, g_ref[...], sc_ref[0], sh_ref[0]).astype(BF16)
    cos = cos_ref[...]
    sin = sin_ref[...]
    half = MLA_ROPE // 2
    down = jnp.dot(u, wd_ref[...], preferred_element_type=F32)

    def norm(z, g):
        ms = jnp.mean(z * z, axis=-1, keepdims=True)
        return (z * lax.rsqrt(ms + RMS_EPS) * g).astype(BF16)

    cq = norm(down[:, 0:MLA_Q_LORA], qn_ref[...])
    ckv = norm(down[:, MLA_Q_LORA:MLA_Q_LORA + MLA_KV_LORA], kvn_ref[...])
    krope_ref[0] = _rope_slab(down[:, MLA_Q_LORA + MLA_KV_LORA:], cos, sin, half).astype(BF16)

    n_nope = MLA_HEADS * MLA_NOPE
    qn = jnp.dot(cq, wuq_ref[:, 0:n_nope], preferred_element_type=F32)
    qnope_ref[0] = (qn * scale).astype(BF16)
    qr = jnp.dot(cq, wuq_ref[:, n_nope:], preferred_element_type=F32)
    qr = jnp.concatenate(
        [_rope_slab(qr[:, j * LANES:(j + 1) * LANES], cos, sin, half) for j in range(qr.shape[1] // LANES)], axis=1)
    qrope_ref[0] = (qr * scale).astype(BF16)
    knope_ref[0] = jnp.dot(ckv, wukv_ref[:, 0:n_nope], preferred_element_type=F32).astype(BF16)
    v_ref[0] = jnp.dot(ckv, wukv_ref[:, n_nope:], preferred_element_type=F32).T.astype(BF16)


def _mla_proj(h, sh, sc, g, wd, qn, kvn, wuq, wukv, cos, sin, b, s):
    t, d = h.shape
    tpb = s // TM
    row = pl.BlockSpec((TM, d), lambda i: (i, 0))
    mod = pl.BlockSpec((1, 1, d), lambda i: (i // tpb, 0, 0))
    tab = pl.BlockSpec((TM, LANES), lambda i: (i, 0))

    def out(width):
        return pl.BlockSpec((1, TM, width), lambda i: (i // tpb, i % tpb, 0))

    def shp(width):
        return jax.ShapeDtypeStruct((b, s, width), BF16)

    n_nope = MLA_HEADS * MLA_NOPE
    n_rope = MLA_HEADS * MLA_ROPE
    n_v = MLA_HEADS * MLA_V
    scale = (MLA_NOPE + MLA_ROPE) ** -0.5 * LOG2E
    return pl.pallas_call(
        functools.partial(_mla_proj_body, scale=scale),
        grid=(t // TM,),
        in_specs=[row, mod, mod, _const_spec((1, d)), _const_spec((d, MLA_DOWN_COLS)),
                  _const_spec((1, MLA_Q_LORA)), _const_spec((1, MLA_KV_LORA)),
                  _const_spec((MLA_Q_LORA, n_nope + n_rope)), _const_spec((MLA_KV_LORA, n_nope + n_v)), tab, tab],
        out_specs=[out(n_nope), out(n_rope), out(n_nope),
                   pl.BlockSpec((1, n_v, TM), lambda i: (i // tpb, 0, i % tpb)), out(LANES)],
        out_shape=[shp(n_nope), shp(n_rope), shp(n_nope), jax.ShapeDtypeStruct((b, n_v, s), BF16), shp(LANES)],
        compiler_params=_cparams(("arbitrary",)),
        name="mla_in_proj",
    )(h, sh, sc, g, wd, qn, kvn, wuq, wukv, cos, sin)


def _mla_body(qn_ref, qr_ref, kn_ref, kr_ref, vt_ref, o_ref, *scratch, tq, tk):
    n_heads = qn_ref.shape[2] // MLA_NOPE
    qi = pl.program_id(2)
    qs = []
    for hd in range(n_heads):
        qr = qr_ref[0, :, (hd // 2) * LANES:(hd // 2 + 1) * LANES]
        qs.append(jnp.concatenate([qn_ref[0, :, hd * LANES:(hd + 1) * LANES],
                                   jnp.where(_head_mask(hd % 2), qr, jnp.zeros_like(qr))], axis=1))
    krow = lax.broadcasted_iota(jnp.int32, (tk, tq), 0)
    qcol = lax.broadcasted_iota(jnp.int32, (tk, tq), 1)

    def qk(hd, kt):
        ks = pl.multiple_of(kt * tk, tk)
        k = jnp.concatenate([kn_ref[0, pl.ds(ks, tk), hd * LANES:(hd + 1) * LANES], kr_ref[0, pl.ds(ks, tk), :]],
                            axis=1)
        return _qk(k, qs[hd])

    def values(hd, kt):
        return vt_ref[0, hd * LANES:(hd + 1) * LANES, pl.ds(pl.multiple_of(kt * tk, tk), tk)]

    def chunk_causal(s):
        return jnp.where(krow // CHUNK <= qcol // CHUNK, s, NEG_LOGIT)

    outs = _attend(n_heads, qi + 1, qk, values, scratch, mask_last=chunk_causal)
    for hd in range(n_heads):
        o_ref[0, :, hd * LANES:(hd + 1) * LANES] = outs[hd].T.astype(BF16)


def _mla_attention(qnope, qrope, knope, krope, vt):
    b, s, w = qnope.shape
    tq = tk = ATT_TQ
    gw = ATT_HEADS * MLA_NOPE
    gr = ATT_HEADS * MLA_ROPE
    qspec = pl.BlockSpec((1, tq, gw), lambda bi, hg, qi: (bi, qi, hg))
    qrspec = pl.BlockSpec((1, tq, gr), lambda bi, hg, qi: (bi, qi, hg))
    kspec = pl.BlockSpec((1, s, gw), lambda bi, hg, qi: (bi, 0, hg))
    krspec = pl.BlockSpec((1, s, LANES), lambda bi, hg, qi: (bi, 0, 0))
    vspec = pl.BlockSpec((1, gw, s), lambda bi, hg, qi: (bi, hg, 0))
    return pl.pallas_call(
        functools.partial(_mla_body, tq=tq, tk=tk),
        grid=(b, w // gw, s // tq),
        in_specs=[qspec, qrspec, kspec, krspec, vspec],
        out_specs=qspec,
        out_shape=jax.ShapeDtypeStruct((b, s, w), BF16),
        scratch_shapes=_attend_scratch(ATT_HEADS, tq, tk, LANES),
        compiler_params=_cparams(("arbitrary", "arbitrary", "arbitrary")),
        name="mla_attention",
    )(qnope, qrope, knope, krope, vt)


def _hyb_weight(w_in, b_f):
    o = 0
    parts = {}
    for name, width in (("fq", 512), ("fk", 512), ("fv", 512), ("ff", FOX_HEADS), ("dq", 512), ("dk", 512),
                        ("dv", 512), ("iq", 256), ("iw", IDX_HEADS), ("ik", HEAD_DIM)):
        parts[name] = w_in[:, o:o + width]
        o += width
    qs = HEAD_DIM ** -0.5
    pad = jnp.zeros((w_in.shape[0], LANES - FOX_HEADS - IDX_HEADS), w_in.dtype)
    w = jnp.concatenate([parts["fq"] * qs, parts["fk"], parts["fv"], parts["dq"] * qs, parts["dk"], parts["dv"],
                         parts["iq"] * qs, parts["ik"], parts["ik"],
                         parts["ff"], parts["iw"] * IDX_HEADS ** -0.5, pad], axis=1)
    bf = jnp.concatenate([b_f, jnp.zeros((LANES - FOX_HEADS,), b_f.dtype)]).reshape(1, LANES)
    return w.astype(BF16), bf.astype(F32)


def _mla_weights(w_down, w_uq, w_ukv):
    kr = w_down[:, MLA_Q_LORA + MLA_KV_LORA:]
    wd = jnp.concatenate([w_down, kr], axis=1)
    uq = w_uq.reshape(MLA_Q_LORA, MLA_HEADS, MLA_NOPE + MLA_ROPE)
    wuq = jnp.concatenate([uq[:, :, :MLA_NOPE].reshape(MLA_Q_LORA, -1), uq[:, :, MLA_NOPE:].reshape(MLA_Q_LORA, -1)],
                          axis=1)
    ukv = w_ukv.reshape(MLA_KV_LORA, MLA_HEADS, MLA_NOPE + MLA_V)
    wukv = jnp.concatenate([ukv[:, :, :MLA_NOPE].reshape(MLA_KV_LORA, -1),
                            ukv[:, :, MLA_NOPE:].reshape(MLA_KV_LORA, -1)], axis=1)
    return wd.astype(BF16), wuq.astype(BF16), wukv.astype(BF16)


def kernel(x, c, positions, ada_w, ada_b, norm_g, ffn_w_gate, ffn_w_up, ffn_w_down, hyb_w_in, fox_b_f, hyb_w_out,
           mla_w_down, mla_q_norm, mla_kv_norm, mla_w_uq, mla_w_ukv, mla_w_out, final_g):
    b, s, d = x.shape
    depth = ada_w.shape[0]
    t = b * s
    assert s % TM == 0 and s % ATT_TQ == 0 and s % DSA_TQ == 0 and s % DSA_TK == 0
    assert DSA_TQ % DSA_ATT_TQ == 0 and ffn_w_gate.shape[-1] % FFN_CHUNK == 0

    mod = _modulation(c, ada_w, ada_b).reshape(depth, b, N_MOD, 1, d)
    cos_d, sin_d, cos_m, sin_m = _rope_tables(positions)
    wg = ffn_w_gate.astype(BF16)
    wu = ffn_w_up.astype(BF16)
    wdn = ffn_w_down.astype(BF16)

    h = x.reshape(t, d)
    for i in range(depth):
        sh1, sc1, g1, sh2, sc2, g2, sh3, sc3, g3 = [mod[i, :, j] for j in range(N_MOD)]
        ng = norm_g[i].reshape(3, 1, d)
        h = _ffn(h, sh1, sc1, g1, ng[0], wg[i, 0], wu[i, 0], wdn[i, 0], s)
        j = i // 2
        if i % 2 == 0:
            w_in, bf = _hyb_weight(hyb_w_in[j], fox_b_f[j])
            fq, fk, fvt, dq, dk, dvt, iq, ik, cparts, iw = _hyb_proj(h, sh2, sc2, ng[1], w_in, bf, cos_d, sin_d, b, s)
            out_a = _fox_attention(fq, fk, cparts, fvt)
            out_b = _dsa_attention(dq, dk, dvt, iq, ik, iw)
            w_out = hyb_w_out[j].astype(BF16)
            half = out_a.shape[2]
            mix = (g2, [out_a.reshape(t, half), out_b.reshape(t, half)], [w_out[:half], w_out[half:]])
        else:
            wd, wuq, wukv = _mla_weights(mla_w_down[j], mla_w_uq[j], mla_w_ukv[j])
            qn, qr, kn, vt, kr = _mla_proj(h, sh2, sc2, ng[1], wd, mla_q_norm[j].reshape(1, -1),
                                           mla_kv_norm[j].reshape(1, -1), wuq, wukv, cos_m, sin_m, b, s)
            out = _mla_attention(qn, qr, kn, kr, vt)
            mix = (g2, [out.reshape(t, -1)], [mla_w_out[j].astype(BF16)])
        last = i == depth - 1
        h = _ffn(h, sh3, sc3, g3, ng[2], wg[i, 1], wu[i, 1], wdn[i, 1], s,
                 final_g=final_g.reshape(1, d) if last else None, mix=mix)
    return h.reshape(b, s, d)
```

```python
import functools
import math

import jax
import jax.numpy as jnp
from jax import lax
from jax.experimental import pallas as pl
from jax.experimental.pallas import tpu as pltpu

F32 = jnp.float32
BF16 = jnp.bfloat16

CHUNK = 64
RMS_EPS = 1e-6
ROPE_THETA = 500000.0
MLA_ROPE_THETA = 10000.0
MACARON_WEIGHT = 0.5
N_MOD = 9
FOX_HEADS = 8
HEAD_DIM = 64
DSA_ROT = 16
IDX_HEADS = 4
DSA_TOPK_MAX = 256
MLA_HEADS = 8
MLA_NOPE = 128
MLA_ROPE = 64
MLA_V = 128
MLA_Q_LORA = 384
MLA_KV_LORA = 256

LANES = 128
VMEM_LIMIT = 56 * 1024 * 1024

LOG2E = math.log2(math.e)
NEG_LOGIT = -1e30
NEG_SCORE = -3e38

TM = 512
FFN_CHUNK = 256
ATT_TQ = 512
ATT_TK = 512
ATT_HEADS = 4
FOX_STEP_HEADS = 8
DSA_TQ = 256
DSA_ATT_TQ = 256
DSA_TK = 512
COUNT_ROWS = 32


def _cparams(sem):
    return pltpu.CompilerParams(dimension_semantics=sem, vmem_limit_bytes=VMEM_LIMIT)


def _const_spec(shape):
    nd = len(shape)
    return pl.BlockSpec(shape, lambda *_: (0,) * nd, pipeline_mode=pl.Buffered(1))


def _silu(x):
    return x * jax.nn.sigmoid(x)


def _modulated_norm(x, g, sc, sh):
    ms = jnp.mean(x * x, axis=-1, keepdims=True)
    y = x * lax.rsqrt(ms + RMS_EPS) * g
    return y * (1.0 + sc) + sh


def _mod_body(c_ref, w_ref, b_ref, o_ref):
    cond = _silu(c_ref[...]).astype(BF16)
    o_ref[0] = jnp.dot(cond, w_ref[0].astype(BF16), preferred_element_type=F32) + b_ref[0]


def _modulation(c, ada_w, ada_b):
    depth, d, n = ada_w.shape
    b = c.shape[0]
    tn = n // N_MOD
    return pl.pallas_call(
        _mod_body,
        grid=(depth, n // tn),
        in_specs=[pl.BlockSpec((b, d), lambda i, j: (0, 0)),
                  pl.BlockSpec((1, d, tn), lambda i, j: (i, 0, j)),
                  pl.BlockSpec((1, 1, tn), lambda i, j: (i, 0, j))],
        out_specs=pl.BlockSpec((1, b, tn), lambda i, j: (i, 0, j)),
        out_shape=jax.ShapeDtypeStruct((depth, b, n), F32),
        compiler_params=_cparams(("arbitrary", "arbitrary")),
        name="adaln_mod",
    )(c, ada_w, ada_b.reshape(depth, 1, n))


def _rope_table_body(pos_ref, invd_ref, sgnd_ref, invm_ref, sgnm_ref, cd_ref, sd_ref, cm_ref, sm_ref):
    pos = pos_ref[...].astype(F32)
    for inv_ref, sgn_ref, c_ref, s_ref in ((invd_ref, sgnd_ref, cd_ref, sd_ref),
                                           (invm_ref, sgnm_ref, cm_ref, sm_ref)):
        ang = pos * inv_ref[...]
        sgn = sgn_ref[...]
        c_ref[...] = jnp.where(sgn != 0.0, jnp.cos(ang), 1.0)
        s_ref[...] = sgn * jnp.sin(ang)


def _lane_pattern(rot, theta):
    half = rot // 2
    inv_freq = jnp.exp(-math.log(theta) * 2.0 * jnp.arange(half, dtype=F32) / rot)
    d = jnp.arange(LANES) % HEAD_DIM
    inv = jnp.where(d < rot, inv_freq[d % half], 0.0).astype(F32)
    sgn = jnp.where(d < half, -1.0, jnp.where(d < rot, 1.0, 0.0)).astype(F32)
    return inv.reshape(1, LANES), sgn.reshape(1, LANES)


def _rope_tables(positions):
    t = positions.size
    invd, sgnd = _lane_pattern(DSA_ROT, ROPE_THETA)
    invm, sgnm = _lane_pattern(MLA_ROPE, MLA_ROPE_THETA)
    tm = 1024
    row = pl.BlockSpec((tm, LANES), lambda i: (i, 0))
    vec = pl.BlockSpec((1, LANES), lambda i: (0, 0))
    tab = jax.ShapeDtypeStruct((t, LANES), F32)
    return pl.pallas_call(
        _rope_table_body,
        grid=(t // tm,),
        in_specs=[pl.BlockSpec((tm, 1), lambda i: (i, 0)), vec, vec, vec, vec],
        out_specs=[row, row, row, row],
        out_shape=[tab, tab, tab, tab],
        compiler_params=_cparams(("arbitrary",)),
        name="rope_tables",
    )(positions.reshape(t, 1), invd, sgnd, invm, sgnm)


def _rope_slab(y, cos, sin, half):
    lane = lax.broadcasted_iota(jnp.int32, (1, LANES), 1) % HEAD_DIM
    first = lane < half
    partner = jnp.where(first, pltpu.roll(y, LANES - half, 1), pltpu.roll(y, half, 1))
    return y * cos + partner * sin


def _ffn_body(h_ref, sh_ref, sc_ref, gt_ref, g_ref, wg_ref, wu_ref, wd_ref, *rest, nf, fc, final, n_mix):
    rest = list(rest)
    x = h_ref[...]
    if n_mix:
        mg_ref = rest.pop(0)
        mix = None
        for x_ref, w_ref in zip(rest[:n_mix], rest[n_mix:2 * n_mix]):
            part = jnp.dot(x_ref[...], w_ref[...], preferred_element_type=F32)
            mix = part if mix is None else mix + part
        x = x + mg_ref[0] * mix
        rest = rest[2 * n_mix:]
    if final:
        fg_ref, o_ref, acc_ref = rest
    else:
        o_ref, acc_ref = rest
    u = _modulated_norm(x, g_ref[...], sc_ref[0], sh_ref[0]).astype(BF16)
    for f in range(nf):
        sl = slice(f * fc, (f + 1) * fc)
        gp = jnp.dot(u, wg_ref[:, sl], preferred_element_type=F32)
        up = jnp.dot(u, wu_ref[:, sl], preferred_element_type=F32)
        a = (_silu(gp) * up).astype(BF16)
        d = jnp.dot(a, wd_ref[sl, :], preferred_element_type=F32)
        if f == 0:
            acc_ref[...] = d
        else:
            acc_ref[...] += d
    y = x + (MACARON_WEIGHT * gt_ref[0]) * acc_ref[...]
    if final:
        ms = jnp.mean(y * y, axis=-1, keepdims=True)
        y = y * lax.rsqrt(ms + RMS_EPS) * fg_ref[...]
    o_ref[...] = y


def _ffn(h, sh, sc, gt, g, wg, wu, wd, s, final_g=None, mix=None):
    t, d = h.shape
    f = wg.shape[1]
    tpb = s // TM
    row = pl.BlockSpec((TM, d), lambda i: (i, 0))
    mod = pl.BlockSpec((1, 1, d), lambda i: (i // tpb, 0, 0))
    in_specs = [row, mod, mod, mod, _const_spec((1, d)),
                _const_spec((d, f)), _const_spec((d, f)), _const_spec((f, d))]
    args = [h, sh, sc, gt, g, wg, wu, wd]
    n_mix = 0
    if mix is not None:
        mgate, xs, ws = mix
        n_mix = len(xs)
        in_specs += [mod] + [pl.BlockSpec((TM, x.shape[1]), lambda i: (i, 0)) for x in xs] + [_const_spec(w.shape) for w in ws]
        args += [mgate, *xs, *ws]
    if final_g is not None:
        in_specs.append(_const_spec((1, d)))
        args.append(final_g)
    return pl.pallas_call(
        functools.partial(_ffn_body, nf=f // FFN_CHUNK, fc=FFN_CHUNK, final=final_g is not None, n_mix=n_mix),
        grid=(t // TM,),
        in_specs=in_specs,
        out_specs=row,
        out_shape=jax.ShapeDtypeStruct((t, d), F32),
        scratch_shapes=[pltpu.VMEM((TM, d), F32)],
        compiler_params=_cparams(("arbitrary",)),
        name="swiglu_half_step",
    )(*args)


HYB_W = 512
HYB_COLS = 6 * HYB_W + 256 + 128 + 128


def _hyb_proj_body(h_ref, sh_ref, sc_ref, g_ref, w_ref, bf_ref, cos_ref, sin_ref,
                   fq_ref, fk_ref, fvt_ref, dq_ref, dk_ref, dvt_ref, iq_ref, ik_ref, cum_ref, iw_ref,
                   carry_ref, *, tpb):
    i = pl.program_id(0)
    x = h_ref[...]
    tm = x.shape[0]
    u = _modulated_norm(x, g_ref[...], sc_ref[0], sh_ref[0]).astype(BF16)
    cos = cos_ref[...]
    sin = sin_ref[...]
    half = DSA_ROT // 2

    def proj(c0, width):
        return jnp.dot(u, w_ref[:, c0:c0 + width], preferred_element_type=F32)

    def roped(y):
        return jnp.concatenate(
            [_rope_slab(y[:, j * LANES:(j + 1) * LANES], cos, sin, half) for j in range(y.shape[1] // LANES)],
            axis=1)

    fq_ref[0] = (proj(0 * HYB_W, HYB_W) * LOG2E).astype(BF16)
    fk_ref[0] = proj(1 * HYB_W, HYB_W).astype(BF16)
    fvt_ref[0] = proj(2 * HYB_W, HYB_W).T.astype(BF16)
    dq_ref[0] = (roped(proj(3 * HYB_W, HYB_W)) * LOG2E).astype(BF16)
    dk_ref[0] = roped(proj(4 * HYB_W, HYB_W)).astype(BF16)
    dvt_ref[0] = proj(5 * HYB_W, HYB_W).T.astype(BF16)
    tail = proj(6 * HYB_W, 512)
    iq_ref[0] = roped(tail[:, 0:256]).astype(BF16)
    ik_ref[0] = roped(tail[:, 256:384]).astype(BF16)
    gates = tail[:, 384:512]
    iw_ref[0] = gates.T[FOX_HEADS:2 * FOX_HEADS, :]

    z = gates + bf_ref[...]
    logf = jnp.minimum(z, 0.0) - jnp.log1p(jnp.exp(-jnp.abs(z)))
    rows = lax.broadcasted_iota(jnp.int32, (tm, LANES), 0)
    c = logf
    k = 1
    while k < tm:
        c = c + jnp.where(rows >= k, pltpu.roll(c, k, 0), 0.0)
        k *= 2

    @pl.when(i % tpb == 0)
    def _():
        carry_ref[...] = jnp.zeros_like(carry_ref)

    c = c + carry_ref[0:1, :]
    carry_ref[...] = jnp.broadcast_to(c[tm - 1:tm, :], carry_ref.shape)
    lane = lax.broadcasted_iota(jnp.int32, (1, LANES), 1)
    c = jnp.where(lane < FOX_HEADS, c * LOG2E, 0.0)
    hi = c.astype(BF16).astype(F32)
    mid = (c - hi).astype(BF16).astype(F32)
    lo = (c - hi - mid).astype(BF16).astype(F32)
    cum_ref[0] = (hi + pltpu.roll(mid, FOX_HEADS, 1) + pltpu.roll(lo, 2 * FOX_HEADS, 1)).astype(BF16)


def _hyb_proj(h, sh, sc, g, w, bf, cos, sin, b, s):
    t, d = h.shape
    tpb = s // TM
    row = pl.BlockSpec((TM, d), lambda i: (i, 0))
    mod = pl.BlockSpec((1, 1, d), lambda i: (i // tpb, 0, 0))
    tab = pl.BlockSpec((TM, LANES), lambda i: (i, 0))

    def out(width):
        return pl.BlockSpec((1, TM, width), lambda i: (i // tpb, i % tpb, 0))

    def shp(width, dt=BF16):
        return jax.ShapeDtypeStruct((b, s, width), dt)

    def out_t(width):
        return pl.BlockSpec((1, width, TM), lambda i: (i // tpb, 0, i % tpb))

    def shp_t(width, dt=BF16):
        return jax.ShapeDtypeStruct((b, width, s), dt)

    return pl.pallas_call(
        functools.partial(_hyb_proj_body, tpb=tpb),
        grid=(t // TM,),
        in_specs=[row, mod, mod, _const_spec((1, d)), _const_spec((d, HYB_COLS)), _const_spec((1, LANES)), tab, tab],
        out_specs=[out(HYB_W), out(HYB_W), out_t(HYB_W), out(HYB_W), out(HYB_W), out_t(HYB_W),
                   out(256), out(LANES), out(LANES), out_t(FOX_HEADS)],
        out_shape=[shp(HYB_W), shp(HYB_W), shp_t(HYB_W), shp(HYB_W), shp(HYB_W), shp_t(HYB_W),
                   shp(256), shp(LANES), shp(LANES), shp_t(FOX_HEADS, F32)],
        scratch_shapes=[pltpu.VMEM((8, LANES), F32)],
        compiler_params=_cparams(("arbitrary",)),
        name="hybrid_in_proj",
    )(h, sh, sc, g, w, bf, cos, sin)


def _attend_scratch(n_heads, tq, tk, n):
    return [pltpu.VMEM((2, n_heads, tk, tq), F32), pltpu.VMEM((2, n_heads, 1, tq), F32),
            pltpu.VMEM((n_heads, 1, tq), F32), pltpu.VMEM((n_heads, 1, tq), F32), pltpu.VMEM((n_heads, n, tq), F32)]


def _attend(n_heads, n_tiles, qk, values, scratch, mask_last=None):
    s_ref, smax_ref, m_ref, l_ref, acc_ref = scratch

    def stage(slot, kt):
        for hd in range(n_heads):
            s = qk(hd, kt)
            s_ref[slot, hd] = s
            smax_ref[slot, hd] = jnp.max(s, axis=0, keepdims=True)

    def step(slot, kt, mask=None):
        for hd in range(n_heads):
            s = s_ref[slot, hd]
            smax = smax_ref[slot, hd]
            if mask is not None:
                s = mask(s)
                smax = jnp.max(s, axis=0, keepdims=True)
            m = m_ref[hd]
            m_new = jnp.maximum(m, smax)
            alpha = jnp.exp2(m - m_new)
            p = jnp.exp2(s - m_new)
            m_ref[hd] = m_new
            l_ref[hd] = alpha * l_ref[hd] + jnp.sum(p, axis=0, keepdims=True)
            acc_ref[hd] = alpha * acc_ref[hd] + jnp.dot(values(hd, kt), p.astype(BF16), preferred_element_type=F32)

    m_ref[...] = jnp.full(m_ref.shape, NEG_LOGIT, F32)
    l_ref[...] = jnp.zeros(l_ref.shape, F32)
    acc_ref[...] = jnp.zeros(acc_ref.shape, F32)
    stage(0, 0)
    n_pairs = (n_tiles - 1) // 2

    def pair(i, _):
        stage(1, 2 * i + 1)
        step(0, 2 * i)
        stage(0, 2 * i + 2)
        step(1, 2 * i + 1)
        return 0

    lax.fori_loop(0, n_pairs, pair, 0)
    odd_tail = n_tiles - 2 * n_pairs == 2

    @pl.when(odd_tail)
    def _():
        stage(1, n_tiles - 1)
        step(0, n_tiles - 2)

    step(jnp.where(odd_tail, 1, 0), n_tiles - 1, mask=mask_last)
    return [acc_ref[hd] / l_ref[hd] for hd in range(n_heads)]


def _head_mask(j):
    lane = lax.broadcasted_iota(jnp.int32, (1, LANES), 1)
    return (lane // HEAD_DIM) == j


def _qk(q, k):
    return lax.dot_general(q, k, (((1,), (1,)), ((), ())), preferred_element_type=F32)


def _pair_out(outs):
    return jnp.concatenate(outs, axis=0).T.astype(BF16)


def _fox_body(q_ref, k_ref, c_ref, vt_ref, o_ref, *scratch, tq, tk):
    n_heads = q_ref.shape[2] // HEAD_DIM
    hg = pl.program_id(1)
    qi = pl.program_id(2)
    krow = lax.broadcasted_iota(jnp.int32, (tk, tq), 0)
    qcol = lax.broadcasted_iota(jnp.int32, (tk, tq), 1)
    lane = lax.broadcasted_iota(jnp.int32, (tq, LANES), 1)
    qs = []
    for hd in range(n_heads):
        pick = (lane % FOX_HEADS == hg * n_heads + hd) & (lane < 3 * FOX_HEADS)
        sel = jnp.where(pick, -1.0, 0.0).astype(BF16)
        qfull = q_ref[0, :, (hd // 2) * LANES:(hd // 2 + 1) * LANES]
        qs.append(jnp.concatenate([jnp.where(_head_mask(hd % 2), qfull, jnp.zeros_like(qfull)), sel], axis=1))

    def qk(hd, kt):
        ks = pl.multiple_of(kt * tk, tk)
        lanes = slice((hd // 2) * LANES, (hd // 2 + 1) * LANES)
        k = jnp.concatenate([k_ref[0, pl.ds(ks, tk), lanes], c_ref[0, pl.ds(ks, tk), :]], axis=1)
        return _qk(k, qs[hd])

    def values(hd, kt):
        return vt_ref[0, hd * HEAD_DIM:(hd + 1) * HEAD_DIM, pl.ds(pl.multiple_of(kt * tk, tk), tk)]

    def causal(s):
        return jnp.where(krow <= qcol, s, NEG_LOGIT)

    outs = _attend(n_heads, qi + 1, qk, values, scratch, mask_last=causal)
    for hp in range(n_heads // 2):
        o_ref[0, :, hp * LANES:(hp + 1) * LANES] = _pair_out(outs[2 * hp:2 * hp + 2])


def _fox_attention(fq, fk, cparts, fvt):
    b, s, w = fq.shape
    tq = tk = ATT_TQ
    gw = FOX_STEP_HEADS * HEAD_DIM
    qspec = pl.BlockSpec((1, tq, gw), lambda bi, hg, qi: (bi, qi, hg))
    kspec = pl.BlockSpec((1, s, gw), lambda bi, hg, qi: (bi, 0, hg))
    cspec = pl.BlockSpec((1, s, LANES), lambda bi, hg, qi: (bi, 0, 0))
    vspec = pl.BlockSpec((1, gw, s), lambda bi, hg, qi: (bi, hg, 0))
    return pl.pallas_call(
        functools.partial(_fox_body, tq=tq, tk=tk),
        grid=(b, w // gw, s // tq),
        in_specs=[qspec, kspec, cspec, vspec],
        out_specs=qspec,
        out_shape=jax.ShapeDtypeStruct((b, s, w), BF16),
        scratch_shapes=_attend_scratch(FOX_STEP_HEADS, tq, tk, HEAD_DIM),
        compiler_params=_cparams(("arbitrary", "arbitrary", "arbitrary")),
        name="fox_attention",
    )(fq, fk, cparts, fvt)


def _hi_key_to_bits(h):
    return jnp.where(h >= 0x8000, h - 0x8000, (~h) & 0xFFFF)


def _tree_sum(parts):
    while len(parts) > 1:
        parts = [a + b for a, b in zip(parts[0::2], parts[1::2])] + ([parts[-1]] if len(parts) % 2 else [])
    return parts[0]


def _dsa_body(dq_ref, iq_ref, iw_ref, dk_ref, dvt_ref, ik_ref, o_ref, sc_ref, a1_ref, d1_ref, d0_ref, a2_ref,
              *scratch, tq, tqa, tk, top_k):
    qi = pl.program_id(1)
    idx_bits = max(1, (sc_ref.shape[0] - 1).bit_length())
    n_tiles = ((qi + 1) * tq + tk - 1) // tk
    qpos = lax.broadcasted_iota(jnp.int32, (1, tq), 1) + qi * tq
    limit = (qpos // CHUNK + 1) * CHUNK
    krow = lax.broadcasted_iota(jnp.int32, (tk, tq), 0)
    kf = jnp.float32(top_k)

    iqf = iq_ref[0]
    iw = iw_ref[0]
    iqs = [jnp.where(_head_mask(h % 2), iqf[:, (h // 2) * LANES:(h // 2 + 1) * LANES], jnp.zeros((tq, LANES), BF16))
           for h in range(IDX_HEADS)]
    ws = [iw[h:h + 1, :] for h in range(IDX_HEADS)]

    def score_tile(kt, _):
        ks = pl.multiple_of(kt * tk, tk)
        ik = ik_ref[0, pl.ds(ks, tk), :]
        sc = None
        for h in range(IDX_HEADS):
            term = jnp.maximum(_qk(ik, iqs[h]), 0.0) * ws[h]
            sc = term if sc is None else sc + term
        sc = jnp.where(krow + ks < limit, sc, NEG_SCORE)
        sc = jnp.where(sc == 0.0, 0.0, sc)
        sc_ref[pl.ds(ks, tk), :] = sc
        bits = pltpu.bitcast(sc, jnp.int32)
        a1_ref[pl.ds(ks, tk), :] = pltpu.bitcast(bits & jnp.int32(-65536), F32).astype(BF16)
        lo = bits & 0xFFFF
        klo = jnp.where(bits < 0, 0xFFFF - lo, lo)
        d1_ref[pl.ds(ks, tk), :] = lax.shift_right_logical(klo, 8).astype(F32).astype(BF16)
        d0_ref[pl.ds(ks, tk), :] = (klo & 0xFF).astype(F32).astype(BF16)
        return 0

    lax.fori_loop(0, n_tiles, score_tile, 0)

    def count(pred):
        def body(kt, cnt):
            ks = pl.multiple_of(kt * tk, tk)
            c = jnp.where(pred(ks), 1.0, 0.0)
            return cnt + jnp.sum(c.reshape(tk // COUNT_ROWS, COUNT_ROWS, tq), axis=0)
        cnt = lax.fori_loop(0, n_tiles, body, jnp.zeros((COUNT_ROWS, tq), F32))
        return jnp.sum(cnt, axis=0, keepdims=True)

    def tile_count(mask):
        c = jnp.where(mask, jnp.ones((), BF16), jnp.zeros((), BF16))
        return _tree_sum([c[i * COUNT_ROWS:(i + 1) * COUNT_ROWS, :] for i in range(tk // COUNT_ROWS)]).astype(F32)

    zero_cnt = jnp.zeros((COUNT_ROWS, tq), F32)

    def count_packed(ref, op, ref_b):
        def body(kt, cnt):
            return cnt + tile_count(op(ref[pl.ds(pl.multiple_of(kt * tk, tk), tk), :], ref_b))
        return jnp.sum(lax.fori_loop(0, n_tiles, body, zero_cnt), axis=0, keepdims=True)

    def hi_float(h):
        return pltpu.bitcast(lax.shift_left(_hi_key_to_bits(h), 16), F32).astype(BF16)

    def hi_step(it, h):
        cand = h | lax.shift_left(jnp.int32(1), 15 - it)
        cnt = count_packed(a1_ref, lambda x, c: x >= c, hi_float(cand))
        return jnp.where(cnt >= kf, cand, h)

    h = lax.fori_loop(0, 16, hi_step, jnp.zeros((1, tq), jnp.int32))
    hbits = _hi_key_to_bits(h)
    hbits = jnp.where((hbits & 0x7F80) == 0, hbits & 0x8000, hbits)
    thr_hi = pltpu.bitcast(lax.shift_left(hbits, 16), F32).astype(BF16)

    def restrict(src_ref, match, digit_ref):
        def body(kt, cnt):
            rows = pl.ds(pl.multiple_of(kt * tk, tk), tk)
            x = src_ref[rows, :]
            a2_ref[rows, :] = jnp.where(x == match, digit_ref[rows, :], -jnp.ones((), BF16))
            return cnt + tile_count(x > match)
        return jnp.sum(lax.fori_loop(0, n_tiles, body, zero_cnt), axis=0, keepdims=True)

    def digit(need):
        def step(it, v):
            cand = v + lax.shift_left(jnp.int32(1), 7 - it).astype(F32)
            cnt = count_packed(a2_ref, lambda x, c: x >= c, cand.astype(BF16))
            return jnp.where(cnt >= need, cand, v)
        return lax.fori_loop(0, 8, step, jnp.zeros((1, tq), F32))

    need_hi = kf - restrict(a1_ref, thr_hi, d1_ref)
    v1 = digit(need_hi)
    need_lo = need_hi - restrict(a2_ref, v1.astype(BF16), d0_ref)
    v0 = digit(need_lo)
    klo = (v1 * 256.0 + v0).astype(jnp.int32)
    lo = jnp.where(hbits >= 0x8000, 0xFFFF - klo, klo)
    thr = pltpu.bitcast(lax.shift_left(hbits, 16) | lo, F32)

    v0b = v0.astype(BF16)

    def tie_counts(kt, cnt):
        x = a2_ref[pl.ds(pl.multiple_of(kt * tk, tk), tk), :]
        return cnt[0] + tile_count(x > v0b), cnt[1] + tile_count(x == v0b)

    above, tied = lax.fori_loop(0, n_tiles, tie_counts, (zero_cnt, zero_cnt))
    need = need_lo - jnp.sum(above, axis=0, keepdims=True)
    surplus = jnp.max(jnp.sum(tied, axis=0, keepdims=True) - need)

    def tie_search():
        def step(it, x):
            cand = x + lax.shift_left(jnp.int32(1), idx_bits - 1 - it)
            cnt = count(lambda ks: (sc_ref[pl.ds(ks, tk), :] == thr) & (krow + ks < cand))
            return jnp.where(cnt < need, cand, x)
        return lax.fori_loop(0, idx_bits, step, jnp.zeros((1, tq), jnp.int32))

    last_tie = lax.cond(surplus > 0.0, tie_search, lambda: jnp.full((1, tq), sc_ref.shape[0], jnp.int32))
    last_tie = jnp.minimum(last_tie, limit - 1)

    def bias_tile(kt, _):
        ks = pl.multiple_of(kt * tk, tk)
        sc = sc_ref[pl.ds(ks, tk), :]
        tie_bias = jnp.where(krow + ks <= last_tie, 0.0, NEG_LOGIT)
        sc_ref[pl.ds(ks, tk), :] = jnp.where(sc > thr, 0.0, jnp.where(sc == thr, tie_bias, NEG_LOGIT))
        return 0

    lax.fori_loop(0, n_tiles, bias_tile, 0)

    n_heads = dq_ref.shape[2] // HEAD_DIM

    def values(hd, kt):
        return dvt_ref[0, hd * HEAD_DIM:(hd + 1) * HEAD_DIM, pl.ds(pl.multiple_of(kt * tk, tk), tk)]

    for sb in range(tq // tqa):
        qrows = slice(sb * tqa, (sb + 1) * tqa)
        qs = []
        for hd in range(n_heads):
            qfull = dq_ref[0, qrows, (hd // 2) * LANES:(hd // 2 + 1) * LANES]
            qs.append(jnp.where(_head_mask(hd % 2), qfull, jnp.zeros_like(qfull)))

        def qk(hd, kt, qs=qs, qrows=qrows):
            ks = pl.multiple_of(kt * tk, tk)
            lanes = slice((hd // 2) * LANES, (hd // 2 + 1) * LANES)
            return _qk(dk_ref[0, pl.ds(ks, tk), lanes], qs[hd]) + sc_ref[pl.ds(ks, tk), qrows]

        sub_tiles = (qi * tq + (sb + 1) * tqa + tk - 1) // tk
        outs = _attend(n_heads, sub_tiles, qk, values, scratch)
        for hp in range(n_heads // 2):
            o_ref[0, qrows, hp * LANES:(hp + 1) * LANES] = _pair_out(outs[2 * hp:2 * hp + 2])


def _dsa_attention(dq, dk, dvt, iq, ik, iw):
    b, s, w = dq.shape
    tq, tqa, tk = DSA_TQ, DSA_ATT_TQ, DSA_TK
    top_k = min(DSA_TOPK_MAX, s // 4)

    def qspec(width):
        return pl.BlockSpec((1, tq, width), lambda bi, qi: (bi, qi, 0))

    def kspec(width):
        return pl.BlockSpec((1, s, width), lambda bi, qi: (bi, 0, 0))

    return pl.pallas_call(
        functools.partial(_dsa_body, tq=tq, tqa=tqa, tk=tk, top_k=top_k),
        grid=(b, s // tq),
        in_specs=[qspec(w), qspec(iq.shape[2]), pl.BlockSpec((1, iw.shape[1], tq), lambda bi, qi: (bi, 0, qi)),
                  kspec(w), pl.BlockSpec((1, w, s), lambda bi, qi: (bi, 0, 0)), kspec(LANES)],
        out_specs=qspec(w),
        out_shape=jax.ShapeDtypeStruct((b, s, w), BF16),
        scratch_shapes=[pltpu.VMEM((s, tq), F32)] + [pltpu.VMEM((s, tq), BF16)] * 4
        + _attend_scratch(w // HEAD_DIM, tqa, tk, HEAD_DIM),
        compiler_params=_cparams(("arbitrary", "arbitrary")),
        name="dsa_attention",
    )(dq, iq, iw, dk, dvt, ik)


MLA_DOWN_COLS = MLA_Q_LORA + MLA_KV_LORA + 2 * MLA_ROPE


def _mla_proj_body(h_ref, sh_ref, sc_ref, g_ref, wd_ref, qn_ref, kvn_ref, wuq_ref, wukv_ref, cos_ref, sin_ref,
                   qnope_ref, qrope_ref, knope_ref, v_ref, krope_ref, *, scale):
    x = h_ref[...]
    u = _modulated_norm(x, g_ref[...], sc_ref[0], sh_ref[0]).astype(BF16)
    cos = cos_ref[...]
    sin = sin_ref[...]
    half = MLA_ROPE // 2
    down = jnp.dot(u, wd_ref[...], preferred_element_type=F32)

    def norm(z, g):
        ms = jnp.mean(z * z, axis=-1, keepdims=True)
        return (z * lax.rsqrt(ms + RMS_EPS) * g).astype(BF16)

    cq = norm(down[:, 0:MLA_Q_LORA], qn_ref[...])
    ckv = norm(down[:, MLA_Q_LORA:MLA_Q_LORA + MLA_KV_LORA], kvn_ref[...])
    krope_ref[0] = _rope_slab(down[:, MLA_Q_LORA + MLA_KV_LORA:], cos, sin, half).astype(BF16)

    n_nope = MLA_HEADS * MLA_NOPE
    qn = jnp.dot(cq, wuq_ref[:, 0:n_nope], preferred_element_type=F32)
    qnope_ref[0] = (qn * scale).astype(BF16)
    qr = jnp.dot(cq, wuq_ref[:, n_nope:], preferred_element_type=F32)
    qr = jnp.concatenate(
        [_rope_slab(qr[:, j * LANES:(j + 1) * LANES], cos, sin, half) for j in range(qr.shape[1] // LANES)], axis=1)
    qrope_ref[0] = (qr * scale).astype(BF16)
    knope_ref[0] = jnp.dot(ckv, wukv_ref[:, 0:n_nope], preferred_element_type=F32).astype(BF16)
    v_ref[0] = jnp.dot(ckv, wukv_ref[:, n_nope:], preferred_element_type=F32).T.astype(BF16)


def _mla_proj(h, sh, sc, g, wd, qn, kvn, wuq, wukv, cos, sin, b, s):
    t, d = h.shape
    tpb = s // TM
    row = pl.BlockSpec((TM, d), lambda i: (i, 0))
    mod = pl.BlockSpec((1, 1, d), lambda i: (i // tpb, 0, 0))
    tab = pl.BlockSpec((TM, LANES), lambda i: (i, 0))

    def out(width):
        return pl.BlockSpec((1, TM, width), lambda i: (i // tpb, i % tpb, 0))

    def shp(width):
        return jax.ShapeDtypeStruct((b, s, width), BF16)

    n_nope = MLA_HEADS * MLA_NOPE
    n_rope = MLA_HEADS * MLA_ROPE
    n_v = MLA_HEADS * MLA_V
    scale = (MLA_NOPE + MLA_ROPE) ** -0.5 * LOG2E
    return pl.pallas_call(
        functools.partial(_mla_proj_body, scale=scale),
        grid=(t // TM,),
        in_specs=[row, mod, mod, _const_spec((1, d)), _const_spec((d, MLA_DOWN_COLS)),
                  _const_spec((1, MLA_Q_LORA)), _const_spec((1, MLA_KV_LORA)),
                  _const_spec((MLA_Q_LORA, n_nope + n_rope)), _const_spec((MLA_KV_LORA, n_nope + n_v)), tab, tab],
        out_specs=[out(n_nope), out(n_rope), out(n_nope),
                   pl.BlockSpec((1, n_v, TM), lambda i: (i // tpb, 0, i % tpb)), out(LANES)],
        out_shape=[shp(n_nope), shp(n_rope), shp(n_nope), jax.ShapeDtypeStruct((b, n_v, s), BF16), shp(LANES)],
        compiler_params=_cparams(("arbitrary",)),
        name="mla_in_proj",
    )(h, sh, sc, g, wd, qn, kvn, wuq, wukv, cos, sin)


def _mla_body(qn_ref, qr_ref, kn_ref, kr_ref, vt_ref, o_ref, *scratch, tq, tk):
    n_heads = qn_ref.shape[2] // MLA_NOPE
    qi = pl.program_id(2)
    qs = []
    for hd in range(n_heads):
        qr = qr_ref[0, :, (hd // 2) * LANES:(hd // 2 + 1) * LANES]
        qs.append(jnp.concatenate([qn_ref[0, :, hd * LANES:(hd + 1) * LANES],
                                   jnp.where(_head_mask(hd % 2), qr, jnp.zeros_like(qr))], axis=1))
    krow = lax.broadcasted_iota(jnp.int32, (tk, tq), 0)
    qcol = lax.broadcasted_iota(jnp.int32, (tk, tq), 1)

    def qk(hd, kt):
        ks = pl.multiple_of(kt * tk, tk)
        k = jnp.concatenate([kn_ref[0, pl.ds(ks, tk), hd * LANES:(hd + 1) * LANES], kr_ref[0, pl.ds(ks, tk), :]],
                            axis=1)
        return _qk(k, qs[hd])

    def values(hd, kt):
        return vt_ref[0, hd * LANES:(hd + 1) * LANES, pl.ds(pl.multiple_of(kt * tk, tk), tk)]

    def chunk_causal(s):
        return jnp.where(krow // CHUNK <= qcol // CHUNK, s, NEG_LOGIT)

    outs = _attend(n_heads, qi + 1, qk, values, scratch, mask_last=chunk_causal)
    for hd in range(n_heads):
        o_ref[0, :, hd * LANES:(hd + 1) * LANES] = outs[hd].T.astype(BF16)


def _mla_attention(qnope, qrope, knope, krope, vt):
    b, s, w = qnope.shape
    tq = tk = ATT_TQ
    gw = ATT_HEADS * MLA_NOPE
    gr = ATT_HEADS * MLA_ROPE
    qspec = pl.BlockSpec((1, tq, gw), lambda bi, hg, qi: (bi, qi, hg))
    qrspec = pl.BlockSpec((1, tq, gr), lambda bi, hg, qi: (bi, qi, hg))
    kspec = pl.BlockSpec((1, s, gw), lambda bi, hg, qi: (bi, 0, hg))
    krspec = pl.BlockSpec((1, s, LANES), lambda bi, hg, qi: (bi, 0, 0))
    vspec = pl.BlockSpec((1, gw, s), lambda bi, hg, qi: (bi, hg, 0))
    return pl.pallas_call(
        functools.partial(_mla_body, tq=tq, tk=tk),
        grid=(b, w // gw, s // tq),
        in_specs=[qspec, qrspec, kspec, krspec, vspec],
        out_specs=qspec,
        out_shape=jax.ShapeDtypeStruct((b, s, w), BF16),
        scratch_shapes=_attend_scratch(ATT_HEADS, tq, tk, LANES),
        compiler_params=_cparams(("arbitrary", "arbitrary", "arbitrary")),
        name="mla_attention",
    )(qnope, qrope, knope, krope, vt)


def _hyb_weight(w_in, b_f):
    o = 0
    parts = {}
    for name, width in (("fq", 512), ("fk", 512), ("fv", 512), ("ff", FOX_HEADS), ("dq", 512), ("dk", 512),
                        ("dv", 512), ("iq", 256), ("iw", IDX_HEADS), ("ik", HEAD_DIM)):
        parts[name] = w_in[:, o:o + width]
        o += width
    qs = HEAD_DIM ** -0.5
    pad = jnp.zeros((w_in.shape[0], LANES - FOX_HEADS - IDX_HEADS), w_in.dtype)
    w = jnp.concatenate([parts["fq"] * qs, parts["fk"], parts["fv"], parts["dq"] * qs, parts["dk"], parts["dv"],
                         parts["iq"] * qs, parts["ik"], parts["ik"],
                         parts["ff"], parts["iw"] * IDX_HEADS ** -0.5, pad], axis=1)
    bf = jnp.concatenate([b_f, jnp.zeros((LANES - FOX_HEADS,), b_f.dtype)]).reshape(1, LANES)
    return w.astype(BF16), bf.astype(F32)


def _mla_weights(w_down, w_uq, w_ukv):
    kr = w_down[:, MLA_Q_LORA + MLA_KV_LORA:]
    wd = jnp.concatenate([w_down, kr], axis=1)
    uq = w_uq.reshape(MLA_Q_LORA, MLA_HEADS, MLA_NOPE + MLA_ROPE)
    wuq = jnp.concatenate([uq[:, :, :MLA_NOPE].reshape(MLA_Q_LORA, -1), uq[:, :, MLA_NOPE:].reshape(MLA_Q_LORA, -1)],
                          axis=1)
    ukv = w_ukv.reshape(MLA_KV_LORA, MLA_HEADS, MLA_NOPE + MLA_V)
    wukv = jnp.concatenate([ukv[:, :, :MLA_NOPE].reshape(MLA_KV_LORA, -1),
                            ukv[:, :, MLA_NOPE:].reshape(MLA_KV_LORA, -1)], axis=1)
    return wd.astype(BF16), wuq.astype(BF16), wukv.astype(BF16)


def kernel(x, c, positions, ada_w, ada_b, norm_g, ffn_w_gate, ffn_w_up, ffn_w_down, hyb_w_in, fox_b_f, hyb_w_out,
           mla_w_down, mla_q_norm, mla_kv_norm, mla_w_uq, mla_w_ukv, mla_w_out, final_g):
    b, s, d = x.shape
    depth = ada_w.shape[0]
    t = b * s
    assert s % TM == 0 and s % ATT_TQ == 0 and s % DSA_TQ == 0 and s % DSA_TK == 0
    assert DSA_TQ % DSA_ATT_TQ == 0 and ffn_w_gate.shape[-1] % FFN_CHUNK == 0

    mod = _modulation(c, ada_w, ada_b).reshape(depth, b, N_MOD, 1, d)
    cos_d, sin_d, cos_m, sin_m = _rope_tables(positions)
    wg = ffn_w_gate.astype(BF16)
    wu = ffn_w_up.astype(BF16)
    wdn = ffn_w_down.astype(BF16)

    h = x.reshape(t, d)
    for i in range(depth):
        sh1, sc1, g1, sh2, sc2, g2, sh3, sc3, g3 = [mod[i, :, j] for j in range(N_MOD)]
        ng = norm_g[i].reshape(3, 1, d)
        h = _ffn(h, sh1, sc1, g1, ng[0], wg[i, 0], wu[i, 0], wdn[i, 0], s)
        j = i // 2
        if i % 2 == 0:
            w_in, bf = _hyb_weight(hyb_w_in[j], fox_b_f[j])
            fq, fk, fvt, dq, dk, dvt, iq, ik, cparts, iw = _hyb_proj(h, sh2, sc2, ng[1], w_in, bf, cos_d, sin_d, b, s)
            out_a = _fox_attention(fq, fk, cparts, fvt)
            out_b = _dsa_attention(dq, dk, dvt, iq, ik, iw)
            w_out = hyb_w_out[j].astype(BF16)
            half = out_a.shape[2]
            mix = (g2, [out_a.reshape(t, half), out_b.reshape(t, half)], [w_out[:half], w_out[half:]])
        else:
            wd, wuq, wukv = _mla_weights(mla_w_down[j], mla_w_uq[j], mla_w_ukv[j])
            qn, qr, kn, vt, kr = _mla_proj(h, sh2, sc2, ng[1], wd, mla_q_norm[j].reshape(1, -1),
                                           mla_kv_norm[j].reshape(1, -1), wuq, wukv, cos_m, sin_m, b, s)
            out = _mla_attention(qn, qr, kn, kr, vt)
            mix = (g2, [out.reshape(t, -1)], [mla_w_out[j].astype(BF16)])
        last = i == depth - 1
        h = _ffn(h, sh3, sc3, g3, ng[2], wg[i, 1], wu[i, 1], wdn[i, 1], s,
                 final_g=final_g.reshape(1, d) if last else None, mix=mix)
    return h.reshape(b, s, d)
```

```python
import functools
import math

import jax
import jax.numpy as jnp
from jax import lax
from jax.experimental import pallas as pl
from jax.experimental.pallas import tpu as pltpu

F32 = jnp.float32
BF16 = jnp.bfloat16

CHUNK = 64
RMS_EPS = 1e-6
ROPE_THETA = 500000.0
MLA_ROPE_THETA = 10000.0
MACARON_WEIGHT = 0.5
N_MOD = 9
FOX_HEADS = 8
HEAD_DIM = 64
DSA_ROT = 16
IDX_HEADS = 4
DSA_TOPK_MAX = 256
MLA_HEADS = 8
MLA_NOPE = 128
MLA_ROPE = 64
MLA_V = 128
MLA_Q_LORA = 384
MLA_KV_LORA = 256

LANES = 128
VMEM_LIMIT = 56 * 1024 * 1024

LOG2E = math.log2(math.e)
NEG_LOGIT = -1e30
NEG_SCORE = -3e38
IDX_NONE = 1e9

TM = 512
FFN_CHUNK = 256
ATT_TQ = 512
ATT_TK = 512
ATT_HEADS = 4
FOX_STEP_HEADS = 8
DSA_TQ = 256
DSA_ATT_TQ = 256
DSA_TK = 512
COUNT_ROWS = 32


def _cparams(sem):
    return pltpu.CompilerParams(dimension_semantics=sem, vmem_limit_bytes=VMEM_LIMIT)


def _const_spec(shape):
    nd = len(shape)
    return pl.BlockSpec(shape, lambda *_: (0,) * nd, pipeline_mode=pl.Buffered(1))


def _silu(x):
    return x * jax.nn.sigmoid(x)


def _modulated_norm(x, g, sc, sh):
    ms = jnp.mean(x * x, axis=-1, keepdims=True)
    y = x * lax.rsqrt(ms + RMS_EPS) * g
    return y * (1.0 + sc) + sh


def _mod_body(c_ref, w_ref, b_ref, o_ref):
    cond = _silu(c_ref[...]).astype(BF16)
    o_ref[0] = jnp.dot(cond, w_ref[0].astype(BF16), preferred_element_type=F32) + b_ref[0]


def _modulation(c, ada_w, ada_b):
    depth, d, n = ada_w.shape
    b = c.shape[0]
    tn = n // N_MOD
    return pl.pallas_call(
        _mod_body,
        grid=(depth, n // tn),
        in_specs=[pl.BlockSpec((b, d), lambda i, j: (0, 0)),
                  pl.BlockSpec((1, d, tn), lambda i, j: (i, 0, j)),
                  pl.BlockSpec((1, 1, tn), lambda i, j: (i, 0, j))],
        out_specs=pl.BlockSpec((1, b, tn), lambda i, j: (i, 0, j)),
        out_shape=jax.ShapeDtypeStruct((depth, b, n), F32),
        compiler_params=_cparams(("arbitrary", "arbitrary")),
        name="adaln_mod",
    )(c, ada_w, ada_b.reshape(depth, 1, n))


def _rope_table_body(pos_ref, invd_ref, sgnd_ref, invm_ref, sgnm_ref, cd_ref, sd_ref, cm_ref, sm_ref):
    pos = pos_ref[...].astype(F32)
    for inv_ref, sgn_ref, c_ref, s_ref in ((invd_ref, sgnd_ref, cd_ref, sd_ref),
                                           (invm_ref, sgnm_ref, cm_ref, sm_ref)):
        ang = pos * inv_ref[...]
        sgn = sgn_ref[...]
        c_ref[...] = jnp.where(sgn != 0.0, jnp.cos(ang), 1.0)
        s_ref[...] = sgn * jnp.sin(ang)


def _lane_pattern(rot, theta):
    half = rot // 2
    inv_freq = jnp.exp(-math.log(theta) * 2.0 * jnp.arange(half, dtype=F32) / rot)
    d = jnp.arange(LANES) % HEAD_DIM
    inv = jnp.where(d < rot, inv_freq[d % half], 0.0).astype(F32)
    sgn = jnp.where(d < half, -1.0, jnp.where(d < rot, 1.0, 0.0)).astype(F32)
    return inv.reshape(1, LANES), sgn.reshape(1, LANES)


def _rope_tables(positions):
    t = positions.size
    invd, sgnd = _lane_pattern(DSA_ROT, ROPE_THETA)
    invm, sgnm = _lane_pattern(MLA_ROPE, MLA_ROPE_THETA)
    tm = 1024
    row = pl.BlockSpec((tm, LANES), lambda i: (i, 0))
    vec = pl.BlockSpec((1, LANES), lambda i: (0, 0))
    tab = jax.ShapeDtypeStruct((t, LANES), F32)
    return pl.pallas_call(
        _rope_table_body,
        grid=(t // tm,),
        in_specs=[pl.BlockSpec((tm, 1), lambda i: (i, 0)), vec, vec, vec, vec],
        out_specs=[row, row, row, row],
        out_shape=[tab, tab, tab, tab],
        compiler_params=_cparams(("arbitrary",)),
        name="rope_tables",
    )(positions.reshape(t, 1), invd, sgnd, invm, sgnm)


def _rope_slab(y, cos, sin, half):
    lane = lax.broadcasted_iota(jnp.int32, (1, LANES), 1) % HEAD_DIM
    first = lane < half
    partner = jnp.where(first, pltpu.roll(y, LANES - half, 1), pltpu.roll(y, half, 1))
    return y * cos + partner * sin


def _ffn_body(h_ref, sh_ref, sc_ref, gt_ref, g_ref, wg_ref, wu_ref, wd_ref, *rest, nf, fc, final, n_mix):
    rest = list(rest)
    x = h_ref[...]
    if n_mix:
        mg_ref = rest.pop(0)
        mix = None
        for x_ref, w_ref in zip(rest[:n_mix], rest[n_mix:2 * n_mix]):
            part = jnp.dot(x_ref[...], w_ref[...], preferred_element_type=F32)
            mix = part if mix is None else mix + part
        x = x + mg_ref[0] * mix
        rest = rest[2 * n_mix:]
    if final:
        fg_ref, o_ref, acc_ref = rest
    else:
        o_ref, acc_ref = rest
    u = _modulated_norm(x, g_ref[...], sc_ref[0], sh_ref[0]).astype(BF16)
    for f in range(nf):
        sl = slice(f * fc, (f + 1) * fc)
        gp = jnp.dot(u, wg_ref[:, sl], preferred_element_type=F32)
        up = jnp.dot(u, wu_ref[:, sl], preferred_element_type=F32)
        a = (_silu(gp) * up).astype(BF16)
        d = jnp.dot(a, wd_ref[sl, :], preferred_element_type=F32)
        if f == 0:
            acc_ref[...] = d
        else:
            acc_ref[...] += d
    y = x + (MACARON_WEIGHT * gt_ref[0]) * acc_ref[...]
    if final:
        ms = jnp.mean(y * y, axis=-1, keepdims=True)
        y = y * lax.rsqrt(ms + RMS_EPS) * fg_ref[...]
    o_ref[...] = y


def _ffn(h, sh, sc, gt, g, wg, wu, wd, s, final_g=None, mix=None):
    t, d = h.shape
    f = wg.shape[1]
    tpb = s // TM
    row = pl.BlockSpec((TM, d), lambda i: (i, 0))
    mod = pl.BlockSpec((1, 1, d), lambda i: (i // tpb, 0, 0))
    in_specs = [row, mod, mod, mod, _const_spec((1, d)),
                _const_spec((d, f)), _const_spec((d, f)), _const_spec((f, d))]
    args = [h, sh, sc, gt, g, wg, wu, wd]
    n_mix = 0
    if mix is not None:
        mgate, xs, ws = mix
        n_mix = len(xs)
        in_specs += [mod] + [pl.BlockSpec((TM, x.shape[1]), lambda i: (i, 0)) for x in xs] + [_const_spec(w.shape) for w in ws]
        args += [mgate, *xs, *ws]
    if final_g is not None:
        in_specs.append(_const_spec((1, d)))
        args.append(final_g)
    return pl.pallas_call(
        functools.partial(_ffn_body, nf=f // FFN_CHUNK, fc=FFN_CHUNK, final=final_g is not None, n_mix=n_mix),
        grid=(t // TM,),
        in_specs=in_specs,
        out_specs=row,
        out_shape=jax.ShapeDtypeStruct((t, d), F32),
        scratch_shapes=[pltpu.VMEM((TM, d), F32)],
        compiler_params=_cparams(("arbitrary",)),
        name="swiglu_half_step",
    )(*args)


HYB_W = 512
HYB_COLS = 6 * HYB_W + 256 + 128 + 128


def _hyb_proj_body(h_ref, sh_ref, sc_ref, g_ref, w_ref, bf_ref, cos_ref, sin_ref,
                   fq_ref, fk_ref, fvt_ref, dq_ref, dk_ref, dvt_ref, iq_ref, ik_ref, cum_ref, iw_ref,
                   carry_ref, *, tpb):
    i = pl.program_id(0)
    x = h_ref[...]
    tm = x.shape[0]
    u = _modulated_norm(x, g_ref[...], sc_ref[0], sh_ref[0]).astype(BF16)
    cos = cos_ref[...]
    sin = sin_ref[...]
    half = DSA_ROT // 2

    def proj(c0, width):
        return jnp.dot(u, w_ref[:, c0:c0 + width], preferred_element_type=F32)

    def roped(y):
        return jnp.concatenate(
            [_rope_slab(y[:, j * LANES:(j + 1) * LANES], cos, sin, half) for j in range(y.shape[1] // LANES)],
            axis=1)

    fq_ref[0] = (proj(0 * HYB_W, HYB_W) * LOG2E).astype(BF16)
    fk_ref[0] = proj(1 * HYB_W, HYB_W).astype(BF16)
    fvt_ref[0] = proj(2 * HYB_W, HYB_W).T.astype(BF16)
    dq_ref[0] = (roped(proj(3 * HYB_W, HYB_W)) * LOG2E).astype(BF16)
    dk_ref[0] = roped(proj(4 * HYB_W, HYB_W)).astype(BF16)
    dvt_ref[0] = proj(5 * HYB_W, HYB_W).T.astype(BF16)
    tail = proj(6 * HYB_W, 512)
    iq_ref[0] = roped(tail[:, 0:256]).astype(BF16)
    ik_ref[0] = roped(tail[:, 256:384]).astype(BF16)
    gates = tail[:, 384:512]
    iw_ref[0] = gates.T[FOX_HEADS:2 * FOX_HEADS, :]

    z = gates + bf_ref[...]
    logf = jnp.minimum(z, 0.0) - jnp.log1p(jnp.exp(-jnp.abs(z)))
    rows = lax.broadcasted_iota(jnp.int32, (tm, LANES), 0)
    c = logf
    k = 1
    while k < tm:
        c = c + jnp.where(rows >= k, pltpu.roll(c, k, 0), 0.0)
        k *= 2

    @pl.when(i % tpb == 0)
    def _():
        carry_ref[...] = jnp.zeros_like(carry_ref)

    c = c + carry_ref[0:1, :]
    carry_ref[...] = jnp.broadcast_to(c[tm - 1:tm, :], carry_ref.shape)
    lane = lax.broadcasted_iota(jnp.int32, (1, LANES), 1)
    c = jnp.where(lane < FOX_HEADS, c * LOG2E, 0.0)
    hi = c.astype(BF16).astype(F32)
    mid = (c - hi).astype(BF16).astype(F32)
    lo = (c - hi - mid).astype(BF16).astype(F32)
    cum_ref[0] = (hi + pltpu.roll(mid, FOX_HEADS, 1) + pltpu.roll(lo, 2 * FOX_HEADS, 1)).astype(BF16)


def _hyb_proj(h, sh, sc, g, w, bf, cos, sin, b, s):
    t, d = h.shape
    tpb = s // TM
    row = pl.BlockSpec((TM, d), lambda i: (i, 0))
    mod = pl.BlockSpec((1, 1, d), lambda i: (i // tpb, 0, 0))
    tab = pl.BlockSpec((TM, LANES), lambda i: (i, 0))

    def out(width):
        return pl.BlockSpec((1, TM, width), lambda i: (i // tpb, i % tpb, 0))

    def shp(width, dt=BF16):
        return jax.ShapeDtypeStruct((b, s, width), dt)

    def out_t(width):
        return pl.BlockSpec((1, width, TM), lambda i: (i // tpb, 0, i % tpb))

    def shp_t(width, dt=BF16):
        return jax.ShapeDtypeStruct((b, width, s), dt)

    return pl.pallas_call(
        functools.partial(_hyb_proj_body, tpb=tpb),
        grid=(t // TM,),
        in_specs=[row, mod, mod, _const_spec((1, d)), _const_spec((d, HYB_COLS)), _const_spec((1, LANES)), tab, tab],
        out_specs=[out(HYB_W), out(HYB_W), out_t(HYB_W), out(HYB_W), out(HYB_W), out_t(HYB_W),
                   out(256), out(LANES), out(LANES), out_t(FOX_HEADS)],
        out_shape=[shp(HYB_W), shp(HYB_W), shp_t(HYB_W), shp(HYB_W), shp(HYB_W), shp_t(HYB_W),
                   shp(256), shp(LANES), shp(LANES), shp_t(FOX_HEADS, F32)],
        scratch_shapes=[pltpu.VMEM((8, LANES), F32)],
        compiler_params=_cparams(("arbitrary",)),
        name="hybrid_in_proj",
    )(h, sh, sc, g, w, bf, cos, sin)


def _attend_scratch(n_heads, tq, tk, n):
    return [pltpu.VMEM((2, n_heads, tk, tq), F32), pltpu.VMEM((2, n_heads, 1, tq), F32),
            pltpu.VMEM((n_heads, 1, tq), F32), pltpu.VMEM((n_heads, 1, tq), F32), pltpu.VMEM((n_heads, n, tq), F32)]


def _attend(n_heads, n_tiles, qk, values, scratch, mask_last=None):
    s_ref, smax_ref, m_ref, l_ref, acc_ref = scratch

    def stage(slot, kt):
        for hd in range(n_heads):
            s = qk(hd, kt)
            s_ref[slot, hd] = s
            smax_ref[slot, hd] = jnp.max(s, axis=0, keepdims=True)

    def step(slot, kt, mask=None):
        for hd in range(n_heads):
            s = s_ref[slot, hd]
            smax = smax_ref[slot, hd]
            if mask is not None:
                s = mask(s)
                smax = jnp.max(s, axis=0, keepdims=True)
            m = m_ref[hd]
            m_new = jnp.maximum(m, smax)
            alpha = jnp.exp2(m - m_new)
            p = jnp.exp2(s - m_new)
            m_ref[hd] = m_new
            l_ref[hd] = alpha * l_ref[hd] + jnp.sum(p, axis=0, keepdims=True)
            acc_ref[hd] = alpha * acc_ref[hd] + jnp.dot(values(hd, kt), p.astype(BF16), preferred_element_type=F32)

    m_ref[...] = jnp.full(m_ref.shape, NEG_LOGIT, F32)
    l_ref[...] = jnp.zeros(l_ref.shape, F32)
    acc_ref[...] = jnp.zeros(acc_ref.shape, F32)
    stage(0, 0)
    n_pairs = (n_tiles - 1) // 2

    def pair(i, _):
        stage(1, 2 * i + 1)
        step(0, 2 * i)
        stage(0, 2 * i + 2)
        step(1, 2 * i + 1)
        return 0

    lax.fori_loop(0, n_pairs, pair, 0)
    odd_tail = n_tiles - 2 * n_pairs == 2

    @pl.when(odd_tail)
    def _():
        stage(1, n_tiles - 1)
        step(0, n_tiles - 2)

    step(jnp.where(odd_tail, 1, 0), n_tiles - 1, mask=mask_last)
    return [acc_ref[hd] / l_ref[hd] for hd in range(n_heads)]


def _head_mask(j):
    lane = lax.broadcasted_iota(jnp.int32, (1, LANES), 1)
    return (lane // HEAD_DIM) == j


def _qk(q, k):
    return lax.dot_general(q, k, (((1,), (1,)), ((), ())), preferred_element_type=F32)


def _pair_out(outs):
    return jnp.concatenate(outs, axis=0).T.astype(BF16)


def _fox_body(q_ref, k_ref, c_ref, vt_ref, o_ref, *scratch, tq, tk):
    n_heads = q_ref.shape[2] // HEAD_DIM
    hg = pl.program_id(1)
    qi = pl.program_id(2)
    krow = lax.broadcasted_iota(jnp.int32, (tk, tq), 0)
    qcol = lax.broadcasted_iota(jnp.int32, (tk, tq), 1)
    lane = lax.broadcasted_iota(jnp.int32, (tq, LANES), 1)
    qs = []
    for hd in range(n_heads):
        pick = (lane % FOX_HEADS == hg * n_heads + hd) & (lane < 3 * FOX_HEADS)
        sel = jnp.where(pick, -1.0, 0.0).astype(BF16)
        qfull = q_ref[0, :, (hd // 2) * LANES:(hd // 2 + 1) * LANES]
        qs.append(jnp.concatenate([jnp.where(_head_mask(hd % 2), qfull, jnp.zeros_like(qfull)), sel], axis=1))

    def qk(hd, kt):
        ks = pl.multiple_of(kt * tk, tk)
        lanes = slice((hd // 2) * LANES, (hd // 2 + 1) * LANES)
        k = jnp.concatenate([k_ref[0, pl.ds(ks, tk), lanes], c_ref[0, pl.ds(ks, tk), :]], axis=1)
        return _qk(k, qs[hd])

    def values(hd, kt):
        return vt_ref[0, hd * HEAD_DIM:(hd + 1) * HEAD_DIM, pl.ds(pl.multiple_of(kt * tk, tk), tk)]

    def causal(s):
        return jnp.where(krow <= qcol, s, NEG_LOGIT)

    outs = _attend(n_heads, qi + 1, qk, values, scratch, mask_last=causal)
    for hp in range(n_heads // 2):
        o_ref[0, :, hp * LANES:(hp + 1) * LANES] = _pair_out(outs[2 * hp:2 * hp + 2])


def _fox_attention(fq, fk, cparts, fvt):
    b, s, w = fq.shape
    tq = tk = ATT_TQ
    gw = FOX_STEP_HEADS * HEAD_DIM
    qspec = pl.BlockSpec((1, tq, gw), lambda bi, hg, qi: (bi, qi, hg))
    kspec = pl.BlockSpec((1, s, gw), lambda bi, hg, qi: (bi, 0, hg))
    cspec = pl.BlockSpec((1, s, LANES), lambda bi, hg, qi: (bi, 0, 0))
    vspec = pl.BlockSpec((1, gw, s), lambda bi, hg, qi: (bi, hg, 0))
    return pl.pallas_call(
        functools.partial(_fox_body, tq=tq, tk=tk),
        grid=(b, w // gw, s // tq),
        in_specs=[qspec, kspec, cspec, vspec],
        out_specs=qspec,
        out_shape=jax.ShapeDtypeStruct((b, s, w), BF16),
        scratch_shapes=_attend_scratch(FOX_STEP_HEADS, tq, tk, HEAD_DIM),
        compiler_params=_cparams(("arbitrary", "arbitrary", "arbitrary")),
        name="fox_attention",
    )(fq, fk, cparts, fvt)


def _hi_key_to_bits(h):
    return jnp.where(h >= 0x8000, h - 0x8000, (~h) & 0xFFFF)


def _tree_sum(parts):
    while len(parts) > 1:
        parts = [a + b for a, b in zip(parts[0::2], parts[1::2])] + ([parts[-1]] if len(parts) % 2 else [])
    return parts[0]


def _dsa_body(dq_ref, iq_ref, iw_ref, dk_ref, dvt_ref, ik_ref, o_ref, sc_ref, tie_ref, a1_ref, d1_ref, d0_ref, a2_ref,
              *scratch, tq, tqa, tk, top_k):
    qi = pl.program_id(1)
    idx_bits = max(1, (sc_ref.shape[0] - 1).bit_length())
    n_tiles = ((qi + 1) * tq + tk - 1) // tk
    qpos = lax.broadcasted_iota(jnp.int32, (1, tq), 1) + qi * tq
    limit = (qpos // CHUNK + 1) * CHUNK
    krow = lax.broadcasted_iota(jnp.int32, (tk, tq), 0)
    kf = jnp.float32(top_k)

    iqf = iq_ref[0]
    iw = iw_ref[0]
    iqs = [jnp.where(_head_mask(h % 2), iqf[:, (h // 2) * LANES:(h // 2 + 1) * LANES], jnp.zeros((tq, LANES), BF16))
           for h in range(IDX_HEADS)]
    ws = [iw[h:h + 1, :] for h in range(IDX_HEADS)]

    def score_tile(kt, _):
        ks = pl.multiple_of(kt * tk, tk)
        ik = ik_ref[0, pl.ds(ks, tk), :]
        sc = None
        for h in range(IDX_HEADS):
            term = jnp.maximum(_qk(ik, iqs[h]), 0.0) * ws[h]
            sc = term if sc is None else sc + term
        sc = jnp.where(krow + ks < limit, sc, NEG_SCORE)
        sc = jnp.where(sc == 0.0, 0.0, sc)
        sc_ref[pl.ds(ks, tk), :] = sc
        bits = pltpu.bitcast(sc, jnp.int32)
        a1_ref[pl.ds(ks, tk), :] = pltpu.bitcast(bits & jnp.int32(-65536), F32).astype(BF16)
        lo = bits & 0xFFFF
        klo = jnp.where(bits < 0, 0xFFFF - lo, lo)
        d1_ref[pl.ds(ks, tk), :] = lax.shift_right_logical(klo, 8).astype(F32).astype(BF16)
        d0_ref[pl.ds(ks, tk), :] = (klo & 0xFF).astype(F32).astype(BF16)
        return 0

    lax.fori_loop(0, n_tiles, score_tile, 0)

    def count(pred):
        def body(kt, cnt):
            ks = pl.multiple_of(kt * tk, tk)
            c = jnp.where(pred(ks), 1.0, 0.0)
            return cnt + jnp.sum(c.reshape(tk // COUNT_ROWS, COUNT_ROWS, tq), axis=0)
        cnt = lax.fori_loop(0, n_tiles, body, jnp.zeros((COUNT_ROWS, tq), F32))
        return jnp.sum(cnt, axis=0, keepdims=True)

    def tile_count(mask):
        c = jnp.where(mask, jnp.ones((), BF16), jnp.zeros((), BF16))
        return _tree_sum([c[i * COUNT_ROWS:(i + 1) * COUNT_ROWS, :] for i in range(tk // COUNT_ROWS)]).astype(F32)

    zero_cnt = jnp.zeros((COUNT_ROWS, tq), F32)

    def count_packed(ref, op, ref_b):
        def body(kt, cnt):
            return cnt + tile_count(op(ref[pl.ds(pl.multiple_of(kt * tk, tk), tk), :], ref_b))
        return jnp.sum(lax.fori_loop(0, n_tiles, body, zero_cnt), axis=0, keepdims=True)

    def hi_float(h):
        return pltpu.bitcast(lax.shift_left(_hi_key_to_bits(h), 16), F32).astype(BF16)

    def hi_step(it, h):
        cand = h | lax.shift_left(jnp.int32(1), 15 - it)
        cnt = count_packed(a1_ref, lambda x, c: x >= c, hi_float(cand))
        return jnp.where(cnt >= kf, cand, h)

    h = lax.fori_loop(0, 16, hi_step, jnp.zeros((1, tq), jnp.int32))
    hbits = _hi_key_to_bits(h)
    hbits = jnp.where((hbits & 0x7F80) == 0, hbits & 0x8000, hbits)
    thr_hi = pltpu.bitcast(lax.shift_left(hbits, 16), F32).astype(BF16)

    def restrict(src_ref, match, digit_ref):
        def body(kt, cnt):
            rows = pl.ds(pl.multiple_of(kt * tk, tk), tk)
            x = src_ref[rows, :]
            a2_ref[rows, :] = jnp.where(x == match, digit_ref[rows, :], -jnp.ones((), BF16))
            return cnt + tile_count(x > match)
        return jnp.sum(lax.fori_loop(0, n_tiles, body, zero_cnt), axis=0, keepdims=True)

    def digit(need):
        def step(it, v):
            cand = v + lax.shift_left(jnp.int32(1), 7 - it).astype(F32)
            cnt = count_packed(a2_ref, lambda x, c: x >= c, cand.astype(BF16))
            return jnp.where(cnt >= need, cand, v)
        return lax.fori_loop(0, 8, step, jnp.zeros((1, tq), F32))

    need_hi = kf - restrict(a1_ref, thr_hi, d1_ref)
    v1 = digit(need_hi)
    need_lo = need_hi - restrict(a2_ref, v1.astype(BF16), d0_ref)
    v0 = digit(need_lo)
    klo = (v1 * 256.0 + v0).astype(jnp.int32)
    lo = jnp.where(hbits >= 0x8000, 0xFFFF - klo, klo)
    thr = pltpu.bitcast(lax.shift_left(hbits, 16) | lo, F32)

    v0b = v0.astype(BF16)

    def tie_counts(kt, cnt):
        x = a2_ref[pl.ds(pl.multiple_of(kt * tk, tk), tk), :]
        return cnt[0] + tile_count(x > v0b), cnt[1] + tile_count(x == v0b)

    above, tied = lax.fori_loop(0, n_tiles, tie_counts, (zero_cnt, zero_cnt))
    need = need_lo - jnp.sum(above, axis=0, keepdims=True)
    surplus = jnp.max(jnp.sum(tied, axis=0, keepdims=True) - need)

    def tie_search():
        def tie_tile(kt, _):
            ks = pl.multiple_of(kt * tk, tk)
            tie_ref[pl.ds(ks, tk), :] = jnp.where(sc_ref[pl.ds(ks, tk), :] == thr, (krow + ks).astype(F32), IDX_NONE)
            return 0

        lax.fori_loop(0, n_tiles, tie_tile, 0)

        def step(it, x):
            cand = x + lax.shift_left(jnp.int32(1), idx_bits - 1 - it).astype(F32)
            cnt = count(lambda ks: tie_ref[pl.ds(ks, tk), :] < cand)
            return jnp.where(cnt < need, cand, x)
        return lax.fori_loop(0, idx_bits, step, jnp.zeros((1, tq), F32)).astype(jnp.int32)

    last_tie = lax.cond(surplus > 0.0, tie_search, lambda: jnp.full((1, tq), sc_ref.shape[0], jnp.int32))
    last_tie = jnp.minimum(last_tie, limit - 1)

    def bias_tile(kt, _):
        ks = pl.multiple_of(kt * tk, tk)
        sc = sc_ref[pl.ds(ks, tk), :]
        tie_bias = jnp.where(krow + ks <= last_tie, 0.0, NEG_LOGIT)
        sc_ref[pl.ds(ks, tk), :] = jnp.where(sc > thr, 0.0, jnp.where(sc == thr, tie_bias, NEG_LOGIT))
        return 0

    lax.fori_loop(0, n_tiles, bias_tile, 0)

    n_heads = dq_ref.shape[2] // HEAD_DIM

    def values(hd, kt):
        return dvt_ref[0, hd * HEAD_DIM:(hd + 1) * HEAD_DIM, pl.ds(pl.multiple_of(kt * tk, tk), tk)]

    for sb in range(tq // tqa):
        qrows = slice(sb * tqa, (sb + 1) * tqa)
        qs = []
        for hd in range(n_heads):
            qfull = dq_ref[0, qrows, (hd // 2) * LANES:(hd // 2 + 1) * LANES]
            qs.append(jnp.where(_head_mask(hd % 2), qfull, jnp.zeros_like(qfull)))

        def qk(hd, kt, qs=qs, qrows=qrows):
            ks = pl.multiple_of(kt * tk, tk)
            lanes = slice((hd // 2) * LANES, (hd // 2 + 1) * LANES)
            return _qk(dk_ref[0, pl.ds(ks, tk), lanes], qs[hd]) + sc_ref[pl.ds(ks, tk), qrows]

        sub_tiles = (qi * tq + (sb + 1) * tqa + tk - 1) // tk
        outs = _attend(n_heads, sub_tiles, qk, values, scratch)
        for hp in range(n_heads // 2):
            o_ref[0, qrows, hp * LANES:(hp + 1) * LANES] = _pair_out(outs[2 * hp:2 * hp + 2])


def _dsa_attention(dq, dk, dvt, iq, ik, iw):
    b, s, w = dq.shape
    tq, tqa, tk = DSA_TQ, DSA_ATT_TQ, DSA_TK
    top_k = min(DSA_TOPK_MAX, s // 4)

    def qspec(width):
        return pl.BlockSpec((1, tq, width), lambda bi, qi: (bi, qi, 0))

    def kspec(width):
        return pl.BlockSpec((1, s, width), lambda bi, qi: (bi, 0, 0))

    return pl.pallas_call(
        functools.partial(_dsa_body, tq=tq, tqa=tqa, tk=tk, top_k=top_k),
        grid=(b, s // tq),
        in_specs=[qspec(w), qspec(iq.shape[2]), pl.BlockSpec((1, iw.shape[1], tq), lambda bi, qi: (bi, 0, qi)),
                  kspec(w), pl.BlockSpec((1, w, s), lambda bi, qi: (bi, 0, 0)), kspec(LANES)],
        out_specs=qspec(w),
        out_shape=jax.ShapeDtypeStruct((b, s, w), BF16),
        scratch_shapes=[pltpu.VMEM((s, tq), F32)] * 2 + [pltpu.VMEM((s, tq), BF16)] * 4
        + _attend_scratch(w // HEAD_DIM, tqa, tk, HEAD_DIM),
        compiler_params=_cparams(("arbitrary", "arbitrary")),
        name="dsa_attention",
    )(dq, iq, iw, dk, dvt, ik)


MLA_DOWN_COLS = MLA_Q_LORA + MLA_KV_LORA + 2 * MLA_ROPE


def _mla_proj_body(h_ref, sh_ref, sc_ref, g_ref, wd_ref, qn_ref, kvn_ref, wuq_ref, wukv_ref, cos_ref, sin_ref,
                   qnope_ref, qrope_ref, knope_ref, v_ref, krope_ref, *, scale):
    x = h_ref[...]
    u = _modulated_norm(x, g_ref[...], sc_ref[0], sh_ref[0]).astype(BF16)
    cos = cos_ref[...]
    sin = sin_ref[...]
    half = MLA_ROPE // 2
    down = jnp.dot(u, wd_ref[...], preferred_element_type=F32)

    def norm(z, g):
        ms = jnp.mean(z * z, axis=-1, keepdims=True)
        return (z * lax.rsqrt(ms + RMS_EPS) * g).astype(BF16)

    cq = norm(down[:, 0:MLA_Q_LORA], qn_ref[...])
    ckv = norm(down[:, MLA_Q_LORA:MLA_Q_LORA + MLA_KV_LORA], kvn_ref[...])
    krope_ref[0] = _rope_slab(down[:, MLA_Q_LORA + MLA_KV_LORA:], cos, sin, half).astype(BF16)

    n_nope = MLA_HEADS * MLA_NOPE
    qn = jnp.dot(cq, wuq_ref[:, 0:n_nope], preferred_element_type=F32)
    qnope_ref[0] = (qn * scale).astype(BF16)
    qr = jnp.dot(cq, wuq_ref[:, n_nope:], preferred_element_type=F32)
    qr = jnp.concatenate(
        [_rope_slab(qr[:, j * LANES:(j + 1) * LANES], cos, sin, half) for j in range(qr.shape[1] // LANES)], axis=1)
    qrope_ref[0] = (qr * scale).astype(BF16)
    knope_ref[0] = jnp.dot(ckv, wukv_ref[:, 0:n_nope], preferred_element_type=F32).astype(BF16)
    v_ref[0] = jnp.dot(ckv, wukv_ref[:, n_nope:], preferred_element_type=F32).T.astype(BF16)


def _mla_proj(h, sh, sc, g, wd, qn, kvn, wuq, wukv, cos, sin, b, s):
    t, d = h.shape
    tpb = s // TM
    row = pl.BlockSpec((TM, d), lambda i: (i, 0))
    mod = pl.BlockSpec((1, 1, d), lambda i: (i // tpb, 0, 0))
    tab = pl.BlockSpec((TM, LANES), lambda i: (i, 0))

    def out(width):
        return pl.BlockSpec((1, TM, width), lambda i: (i // tpb, i % tpb, 0))

    def shp(width):
        return jax.ShapeDtypeStruct((b, s, width), BF16)

    n_nope = MLA_HEADS * MLA_NOPE
    n_rope = MLA_HEADS * MLA_ROPE
    n_v = MLA_HEADS * MLA_V
    scale = (MLA_NOPE + MLA_ROPE) ** -0.5 * LOG2E
    return pl.pallas_call(
        functools.partial(_mla_proj_body, scale=scale),
        grid=(t // TM,),
        in_specs=[row, mod, mod, _const_spec((1, d)), _const_spec((d, MLA_DOWN_COLS)),
                  _const_spec((1, MLA_Q_LORA)), _const_spec((1, MLA_KV_LORA)),
                  _const_spec((MLA_Q_LORA, n_nope + n_rope)), _const_spec((MLA_KV_LORA, n_nope + n_v)), tab, tab],
        out_specs=[out(n_nope), out(n_rope), out(n_nope),
                   pl.BlockSpec((1, n_v, TM), lambda i: (i // tpb, 0, i % tpb)), out(LANES)],
        out_shape=[shp(n_nope), shp(n_rope), shp(n_nope), jax.ShapeDtypeStruct((b, n_v, s), BF16), shp(LANES)],
        compiler_params=_cparams(("arbitrary",)),
        name="mla_in_proj",
    )(h, sh, sc, g, wd, qn, kvn, wuq, wukv, cos, sin)


def _mla_body(qn_ref, qr_ref, kn_ref, kr_ref, vt_ref, o_ref, *scratch, tq, tk):
    n_heads = qn_ref.shape[2] // MLA_NOPE
    qi = pl.program_id(2)
    qs = []
    for hd in range(n_heads):
        qr = qr_ref[0, :, (hd // 2) * LANES:(hd // 2 + 1) * LANES]
        qs.append(jnp.concatenate([qn_ref[0, :, hd * LANES:(hd + 1) * LANES],
                                   jnp.where(_head_mask(hd % 2), qr, jnp.zeros_like(qr))], axis=1))
    krow = lax.broadcasted_iota(jnp.int32, (tk, tq), 0)
    qcol = lax.broadcasted_iota(jnp.int32, (tk, tq), 1)

    def qk(hd, kt):
        ks = pl.multiple_of(kt * tk, tk)
        k = jnp.concatenate([kn_ref[0, pl.ds(ks, tk), hd * LANES:(hd + 1) * LANES], kr_ref[0, pl.ds(ks, tk), :]],
                            axis=1)
        return _qk(k, qs[hd])

    def values(hd, kt):
        return vt_ref[0, hd * LANES:(hd + 1) * LANES, pl.ds(pl.multiple_of(kt * tk, tk), tk)]

    def chunk_causal(s):
        return jnp.where(krow // CHUNK <= qcol // CHUNK, s, NEG_LOGIT)

    outs = _attend(n_heads, qi + 1, qk, values, scratch, mask_last=chunk_causal)
    for hd in range(n_heads):
        o_ref[0, :, hd * LANES:(hd + 1) * LANES] = outs[hd].T.astype(BF16)


def _mla_attention(qnope, qrope, knope, krope, vt):
    b, s, w = qnope.shape
    tq = tk = ATT_TQ
    gw = ATT_HEADS * MLA_NOPE
    gr = ATT_HEADS * MLA_ROPE
    qspec = pl.BlockSpec((1, tq, gw), lambda bi, hg, qi: (bi, qi, hg))
    qrspec = pl.BlockSpec((1, tq, gr), lambda bi, hg, qi: (bi, qi, hg))
    kspec = pl.BlockSpec((1, s, gw), lambda bi, hg, qi: (bi, 0, hg))
    krspec = pl.BlockSpec((1, s, LANES), lambda bi, hg, qi: (bi, 0, 0))
    vspec = pl.BlockSpec((1, gw, s), lambda bi, hg, qi: (bi, hg, 0))
    return pl.pallas_call(
        functools.partial(_mla_body, tq=tq, tk=tk),
        grid=(b, w // gw, s // tq),
        in_specs=[qspec, qrspec, kspec, krspec, vspec],
        out_specs=qspec,
        out_shape=jax.ShapeDtypeStruct((b, s, w), BF16),
        scratch_shapes=_attend_scratch(ATT_HEADS, tq, tk, LANES),
        compiler_params=_cparams(("arbitrary", "arbitrary", "arbitrary")),
        name="mla_attention",
    )(qnope, qrope, knope, krope, vt)


def _hyb_weight(w_in, b_f):
    o = 0
    parts = {}
    for name, width in (("fq", 512), ("fk", 512), ("fv", 512), ("ff", FOX_HEADS), ("dq", 512), ("dk", 512),
                        ("dv", 512), ("iq", 256), ("iw", IDX_HEADS), ("ik", HEAD_DIM)):
        parts[name] = w_in[:, o:o + width]
        o += width
    qs = HEAD_DIM ** -0.5
    pad = jnp.zeros((w_in.shape[0], LANES - FOX_HEADS - IDX_HEADS), w_in.dtype)
    w = jnp.concatenate([parts["fq"] * qs, parts["fk"], parts["fv"], parts["dq"] * qs, parts["dk"], parts["dv"],
                         parts["iq"] * qs, parts["ik"], parts["ik"],
                         parts["ff"], parts["iw"] * IDX_HEADS ** -0.5, pad], axis=1)
    bf = jnp.concatenate([b_f, jnp.zeros((LANES - FOX_HEADS,), b_f.dtype)]).reshape(1, LANES)
    return w.astype(BF16), bf.astype(F32)


def _mla_weights(w_down, w_uq, w_ukv):
    kr = w_down[:, MLA_Q_LORA + MLA_KV_LORA:]
    wd = jnp.concatenate([w_down, kr], axis=1)
    uq = w_uq.reshape(MLA_Q_LORA, MLA_HEADS, MLA_NOPE + MLA_ROPE)
    wuq = jnp.concatenate([uq[:, :, :MLA_NOPE].reshape(MLA_Q_LORA, -1), uq[:, :, MLA_NOPE:].reshape(MLA_Q_LORA, -1)],
                          axis=1)
    ukv = w_ukv.reshape(MLA_KV_LORA, MLA_HEADS, MLA_NOPE + MLA_V)
    wukv = jnp.concatenate([ukv[:, :, :MLA_NOPE].reshape(MLA_KV_LORA, -1),
                            ukv[:, :, MLA_NOPE:].reshape(MLA_KV_LORA, -1)], axis=1)
    return wd.astype(BF16), wuq.astype(BF16), wukv.astype(BF16)


def kernel(x, c, positions, ada_w, ada_b, norm_g, ffn_w_gate, ffn_w_up, ffn_w_down, hyb_w_in, fox_b_f, hyb_w_out,
           mla_w_down, mla_q_norm, mla_kv_norm, mla_w_uq, mla_w_ukv, mla_w_out, final_g):
    b, s, d = x.shape
    depth = ada_w.shape[0]
    t = b * s
    assert s % TM == 0 and s % ATT_TQ == 0 and s % DSA_TQ == 0 and s % DSA_TK == 0
    assert DSA_TQ % DSA_ATT_TQ == 0 and ffn_w_gate.shape[-1] % FFN_CHUNK == 0

    mod = _modulation(c, ada_w, ada_b).reshape(depth, b, N_MOD, 1, d)
    cos_d, sin_d, cos_m, sin_m = _rope_tables(positions)
    wg = ffn_w_gate.astype(BF16)
    wu = ffn_w_up.astype(BF16)
    wdn = ffn_w_down.astype(BF16)

    h = x.reshape(t, d)
    for i in range(depth):
        sh1, sc1, g1, sh2, sc2, g2, sh3, sc3, g3 = [mod[i, :, j] for j in range(N_MOD)]
        ng = norm_g[i].reshape(3, 1, d)
        h = _ffn(h, sh1, sc1, g1, ng[0], wg[i, 0], wu[i, 0], wdn[i, 0], s)
        j = i // 2
        if i % 2 == 0:
            w_in, bf = _hyb_weight(hyb_w_in[j], fox_b_f[j])
            fq, fk, fvt, dq, dk, dvt, iq, ik, cparts, iw = _hyb_proj(h, sh2, sc2, ng[1], w_in, bf, cos_d, sin_d, b, s)
            out_a = _fox_attention(fq, fk, cparts, fvt)
            out_b = _dsa_attention(dq, dk, dvt, iq, ik, iw)
            w_out = hyb_w_out[j].astype(BF16)
            half = out_a.shape[2]
            mix = (g2, [out_a.reshape(t, half), out_b.reshape(t, half)], [w_out[:half], w_out[half:]])
        else:
            wd, wuq, wukv = _mla_weights(mla_w_down[j], mla_w_uq[j], mla_w_ukv[j])
            qn, qr, kn, vt, kr = _mla_proj(h, sh2, sc2, ng[1], wd, mla_q_norm[j].reshape(1, -1),
                                           mla_kv_norm[j].reshape(1, -1), wuq, wukv, cos_m, sin_m, b, s)
            out = _mla_attention(qn, qr, kn, kr, vt)
            mix = (g2, [out.reshape(t, -1)], [mla_w_out[j].astype(BF16)])
        last = i == depth - 1
        h = _ffn(h, sh3, sc3, g3, ng[2], wg[i, 1], wu[i, 1], wdn[i, 1], s,
                 final_g=final_g.reshape(1, d) if last else None, mix=mix)
    return h.reshape(b, s, d)
```

```python
import functools
import math

import jax
import jax.numpy as jnp
from jax import lax
from jax.experimental import pallas as pl
from jax.experimental.pallas import tpu as pltpu

F32 = jnp.float32
BF16 = jnp.bfloat16

CHUNK = 64
RMS_EPS = 1e-6
ROPE_THETA = 500000.0
MLA_ROPE_THETA = 10000.0
MACARON_WEIGHT = 0.5
N_MOD = 9
FOX_HEADS = 8
HEAD_DIM = 64
DSA_ROT = 16
IDX_HEADS = 4
DSA_TOPK_MAX = 256
MLA_HEADS = 8
MLA_NOPE = 128
MLA_ROPE = 64
MLA_V = 128
MLA_Q_LORA = 384
MLA_KV_LORA = 256

LANES = 128
VMEM_LIMIT = 56 * 1024 * 1024

LOG2E = math.log2(math.e)
NEG_LOGIT = -1e30
NEG_SCORE = -3e38

TM = 512
FFN_CHUNK = 256
ATT_TQ = 512
ATT_TK = 512
ATT_HEADS = 4
FOX_STEP_HEADS = 8
DSA_TQ = 256
DSA_ATT_TQ = 256
DSA_TK = 512
COUNT_ROWS = 32


def _cparams(sem):
    return pltpu.CompilerParams(dimension_semantics=sem, vmem_limit_bytes=VMEM_LIMIT)


def _const_spec(shape):
    nd = len(shape)
    return pl.BlockSpec(shape, lambda *_: (0,) * nd, pipeline_mode=pl.Buffered(1))


def _silu(x):
    return x * jax.nn.sigmoid(x)


def _modulated_norm(x, g, sc, sh):
    ms = jnp.mean(x * x, axis=-1, keepdims=True)
    y = x * lax.rsqrt(ms + RMS_EPS) * g
    return y * (1.0 + sc) + sh


def _mod_body(c_ref, w_ref, b_ref, o_ref):
    cond = _silu(c_ref[...]).astype(BF16)
    o_ref[0] = jnp.dot(cond, w_ref[0].astype(BF16), preferred_element_type=F32) + b_ref[0]


def _modulation(c, ada_w, ada_b):
    depth, d, n = ada_w.shape
    b = c.shape[0]
    tn = n // N_MOD
    return pl.pallas_call(
        _mod_body,
        grid=(depth, n // tn),
        in_specs=[pl.BlockSpec((b, d), lambda i, j: (0, 0)),
                  pl.BlockSpec((1, d, tn), lambda i, j: (i, 0, j)),
                  pl.BlockSpec((1, 1, tn), lambda i, j: (i, 0, j))],
        out_specs=pl.BlockSpec((1, b, tn), lambda i, j: (i, 0, j)),
        out_shape=jax.ShapeDtypeStruct((depth, b, n), F32),
        compiler_params=_cparams(("arbitrary", "arbitrary")),
        name="adaln_mod",
    )(c, ada_w, ada_b.reshape(depth, 1, n))


def _rope_table_body(pos_ref, invd_ref, sgnd_ref, invm_ref, sgnm_ref, cd_ref, sd_ref, cm_ref, sm_ref):
    pos = pos_ref[...].astype(F32)
    for inv_ref, sgn_ref, c_ref, s_ref in ((invd_ref, sgnd_ref, cd_ref, sd_ref),
                                           (invm_ref, sgnm_ref, cm_ref, sm_ref)):
        ang = pos * inv_ref[...]
        sgn = sgn_ref[...]
        c_ref[...] = jnp.where(sgn != 0.0, jnp.cos(ang), 1.0)
        s_ref[...] = sgn * jnp.sin(ang)


def _lane_pattern(rot, theta):
    half = rot // 2
    inv_freq = jnp.exp(-math.log(theta) * 2.0 * jnp.arange(half, dtype=F32) / rot)
    d = jnp.arange(LANES) % HEAD_DIM
    inv = jnp.where(d < rot, inv_freq[d % half], 0.0).astype(F32)
    sgn = jnp.where(d < half, -1.0, jnp.where(d < rot, 1.0, 0.0)).astype(F32)
    return inv.reshape(1, LANES), sgn.reshape(1, LANES)


def _rope_tables(positions):
    t = positions.size
    invd, sgnd = _lane_pattern(DSA_ROT, ROPE_THETA)
    invm, sgnm = _lane_pattern(MLA_ROPE, MLA_ROPE_THETA)
    tm = 1024
    row = pl.BlockSpec((tm, LANES), lambda i: (i, 0))
    vec = pl.BlockSpec((1, LANES), lambda i: (0, 0))
    tab = jax.ShapeDtypeStruct((t, LANES), F32)
    return pl.pallas_call(
        _rope_table_body,
        grid=(t // tm,),
        in_specs=[pl.BlockSpec((tm, 1), lambda i: (i, 0)), vec, vec, vec, vec],
        out_specs=[row, row, row, row],
        out_shape=[tab, tab, tab, tab],
        compiler_params=_cparams(("arbitrary",)),
        name="rope_tables",
    )(positions.reshape(t, 1), invd, sgnd, invm, sgnm)


def _rope_slab(y, cos, sin, half):
    lane = lax.broadcasted_iota(jnp.int32, (1, LANES), 1) % HEAD_DIM
    first = lane < half
    partner = jnp.where(first, pltpu.roll(y, LANES - half, 1), pltpu.roll(y, half, 1))
    return y * cos + partner * sin


def _ffn_body(h_ref, sh_ref, sc_ref, gt_ref, g_ref, wg_ref, wu_ref, wd_ref, *rest, nf, fc, final, n_mix):
    rest = list(rest)
    x = h_ref[...]
    if n_mix:
        mg_ref = rest.pop(0)
        mix = None
        for x_ref, w_ref in zip(rest[:n_mix], rest[n_mix:2 * n_mix]):
            part = jnp.dot(x_ref[...], w_ref[...], preferred_element_type=F32)
            mix = part if mix is None else mix + part
        x = x + mg_ref[0] * mix
        rest = rest[2 * n_mix:]
    if final:
        fg_ref, o_ref, acc_ref = rest
    else:
        o_ref, acc_ref = rest
    u = _modulated_norm(x, g_ref[...], sc_ref[0], sh_ref[0]).astype(BF16)
    for f in range(nf):
        sl = slice(f * fc, (f + 1) * fc)
        gp = jnp.dot(u, wg_ref[:, sl], preferred_element_type=F32)
        up = jnp.dot(u, wu_ref[:, sl], preferred_element_type=F32)
        a = (_silu(gp) * up).astype(BF16)
        d = jnp.dot(a, wd_ref[sl, :], preferred_element_type=F32)
        if f == 0:
            acc_ref[...] = d
        else:
            acc_ref[...] += d
    y = x + (MACARON_WEIGHT * gt_ref[0]) * acc_ref[...]
    if final:
        ms = jnp.mean(y * y, axis=-1, keepdims=True)
        y = y * lax.rsqrt(ms + RMS_EPS) * fg_ref[...]
    o_ref[...] = y


def _ffn(h, sh, sc, gt, g, wg, wu, wd, s, final_g=None, mix=None):
    t, d = h.shape
    f = wg.shape[1]
    tpb = s // TM
    row = pl.BlockSpec((TM, d), lambda i: (i, 0))
    mod = pl.BlockSpec((1, 1, d), lambda i: (i // tpb, 0, 0))
    in_specs = [row, mod, mod, mod, _const_spec((1, d)),
                _const_spec((d, f)), _const_spec((d, f)), _const_spec((f, d))]
    args = [h, sh, sc, gt, g, wg, wu, wd]
    n_mix = 0
    if mix is not None:
        mgate, xs, ws = mix
        n_mix = len(xs)
        in_specs += [mod] + [pl.BlockSpec((TM, x.shape[1]), lambda i: (i, 0)) for x in xs] + [_const_spec(w.shape) for w in ws]
        args += [mgate, *xs, *ws]
    if final_g is not None:
        in_specs.append(_const_spec((1, d)))
        args.append(final_g)
    return pl.pallas_call(
        functools.partial(_ffn_body, nf=f // FFN_CHUNK, fc=FFN_CHUNK, final=final_g is not None, n_mix=n_mix),
        grid=(t // TM,),
        in_specs=in_specs,
        out_specs=row,
        out_shape=jax.ShapeDtypeStruct((t, d), F32),
        scratch_shapes=[pltpu.VMEM((TM, d), F32)],
        compiler_params=_cparams(("arbitrary",)),
        name="swiglu_half_step",
    )(*args)


HYB_W = 512
HYB_COLS = 6 * HYB_W + 256 + 128 + 128


def _hyb_proj_body(h_ref, sh_ref, sc_ref, g_ref, w_ref, bf_ref, cos_ref, sin_ref,
                   fq_ref, fk_ref, fvt_ref, dq_ref, dk_ref, dvt_ref, iq_ref, ik_ref, cum_ref, iw_ref,
                   carry_ref, *, tpb):
    i = pl.program_id(0)
    x = h_ref[...]
    tm = x.shape[0]
    u = _modulated_norm(x, g_ref[...], sc_ref[0], sh_ref[0]).astype(BF16)
    cos = cos_ref[...]
    sin = sin_ref[...]
    half = DSA_ROT // 2

    def proj(c0, width):
        return jnp.dot(u, w_ref[:, c0:c0 + width], preferred_element_type=F32)

    def roped(y):
        return jnp.concatenate(
            [_rope_slab(y[:, j * LANES:(j + 1) * LANES], cos, sin, half) for j in range(y.shape[1] // LANES)],
            axis=1)

    fq_ref[0] = (proj(0 * HYB_W, HYB_W) * LOG2E).astype(BF16)
    fk_ref[0] = proj(1 * HYB_W, HYB_W).astype(BF16)
    fvt_ref[0] = proj(2 * HYB_W, HYB_W).T.astype(BF16)
    dq_ref[0] = (roped(proj(3 * HYB_W, HYB_W)) * LOG2E).astype(BF16)
    dk_ref[0] = roped(proj(4 * HYB_W, HYB_W)).astype(BF16)
    dvt_ref[0] = proj(5 * HYB_W, HYB_W).T.astype(BF16)
    tail = proj(6 * HYB_W, 512)
    iq_ref[0] = roped(tail[:, 0:256]).astype(BF16)
    ik_ref[0] = roped(tail[:, 256:384]).astype(BF16)
    gates = tail[:, 384:512]
    iw_ref[0] = gates.T[FOX_HEADS:2 * FOX_HEADS, :]

    z = gates + bf_ref[...]
    logf = jnp.minimum(z, 0.0) - jnp.log1p(jnp.exp(-jnp.abs(z)))
    rows = lax.broadcasted_iota(jnp.int32, (tm, LANES), 0)
    c = logf
    k = 1
    while k < tm:
        c = c + jnp.where(rows >= k, pltpu.roll(c, k, 0), 0.0)
        k *= 2

    @pl.when(i % tpb == 0)
    def _():
        carry_ref[...] = jnp.zeros_like(carry_ref)

    c = c + carry_ref[0:1, :]
    carry_ref[...] = jnp.broadcast_to(c[tm - 1:tm, :], carry_ref.shape)
    lane = lax.broadcasted_iota(jnp.int32, (1, LANES), 1)
    c = jnp.where(lane < FOX_HEADS, c * LOG2E, 0.0)
    hi = c.astype(BF16).astype(F32)
    mid = (c - hi).astype(BF16).astype(F32)
    lo = (c - hi - mid).astype(BF16).astype(F32)
    cum_ref[0] = (hi + pltpu.roll(mid, FOX_HEADS, 1) + pltpu.roll(lo, 2 * FOX_HEADS, 1)).astype(BF16)


def _hyb_proj(h, sh, sc, g, w, bf, cos, sin, b, s):
    t, d = h.shape
    tpb = s // TM
    row = pl.BlockSpec((TM, d), lambda i: (i, 0))
    mod = pl.BlockSpec((1, 1, d), lambda i: (i // tpb, 0, 0))
    tab = pl.BlockSpec((TM, LANES), lambda i: (i, 0))

    def out(width):
        return pl.BlockSpec((1, TM, width), lambda i: (i // tpb, i % tpb, 0))

    def shp(width, dt=BF16):
        return jax.ShapeDtypeStruct((b, s, width), dt)

    def out_t(width):
        return pl.BlockSpec((1, width, TM), lambda i: (i // tpb, 0, i % tpb))

    def shp_t(width, dt=BF16):
        return jax.ShapeDtypeStruct((b, width, s), dt)

    return pl.pallas_call(
        functools.partial(_hyb_proj_body, tpb=tpb),
        grid=(t // TM,),
        in_specs=[row, mod, mod, _const_spec((1, d)), _const_spec((d, HYB_COLS)), _const_spec((1, LANES)), tab, tab],
        out_specs=[out(HYB_W), out(HYB_W), out_t(HYB_W), out(HYB_W), out(HYB_W), out_t(HYB_W),
                   out(256), out(LANES), out(LANES), out_t(FOX_HEADS)],
        out_shape=[shp(HYB_W), shp(HYB_W), shp_t(HYB_W), shp(HYB_W), shp(HYB_W), shp_t(HYB_W),
                   shp(256), shp(LANES), shp(LANES), shp_t(FOX_HEADS, F32)],
        scratch_shapes=[pltpu.VMEM((8, LANES), F32)],
        compiler_params=_cparams(("arbitrary",)),
        name="hybrid_in_proj",
    )(h, sh, sc, g, w, bf, cos, sin)


def _attend_scratch(n_heads, tq, tk, n):
    return [pltpu.VMEM((2, n_heads, tk, tq), F32), pltpu.VMEM((2, n_heads, 1, tq), F32),
            pltpu.VMEM((n_heads, 1, tq), F32), pltpu.VMEM((n_heads, 1, tq), F32), pltpu.VMEM((n_heads, n, tq), F32)]


def _attend(n_heads, n_tiles, qk, values, scratch, mask_last=None):
    s_ref, smax_ref, m_ref, l_ref, acc_ref = scratch

    def stage(slot, kt):
        for hd in range(n_heads):
            s = qk(hd, kt)
            s_ref[slot, hd] = s
            smax_ref[slot, hd] = jnp.max(s, axis=0, keepdims=True)

    def step(slot, kt, mask=None):
        for hd in range(n_heads):
            s = s_ref[slot, hd]
            smax = smax_ref[slot, hd]
            if mask is not None:
                s = mask(s)
                smax = jnp.max(s, axis=0, keepdims=True)
            m = m_ref[hd]
            m_new = jnp.maximum(m, smax)
            alpha = jnp.exp2(m - m_new)
            p = jnp.exp2(s - m_new)
            m_ref[hd] = m_new
            l_ref[hd] = alpha * l_ref[hd] + jnp.sum(p, axis=0, keepdims=True)
            acc_ref[hd] = alpha * acc_ref[hd] + jnp.dot(values(hd, kt), p.astype(BF16), preferred_element_type=F32)

    m_ref[...] = jnp.full(m_ref.shape, NEG_LOGIT, F32)
    l_ref[...] = jnp.zeros(l_ref.shape, F32)
    acc_ref[...] = jnp.zeros(acc_ref.shape, F32)
    stage(0, 0)
    n_pairs = (n_tiles - 1) // 2

    def pair(i, _):
        stage(1, 2 * i + 1)
        step(0, 2 * i)
        stage(0, 2 * i + 2)
        step(1, 2 * i + 1)
        return 0

    lax.fori_loop(0, n_pairs, pair, 0)
    odd_tail = n_tiles - 2 * n_pairs == 2

    @pl.when(odd_tail)
    def _():
        stage(1, n_tiles - 1)
        step(0, n_tiles - 2)

    step(jnp.where(odd_tail, 1, 0), n_tiles - 1, mask=mask_last)
    return [acc_ref[hd] / l_ref[hd] for hd in range(n_heads)]


def _head_mask(j):
    lane = lax.broadcasted_iota(jnp.int32, (1, LANES), 1)
    return (lane // HEAD_DIM) == j


def _qk(q, k):
    return lax.dot_general(q, k, (((1,), (1,)), ((), ())), preferred_element_type=F32)


def _pair_out(outs):
    return jnp.concatenate(outs, axis=0).T.astype(BF16)


def _fox_body(q_ref, k_ref, c_ref, vt_ref, o_ref, *scratch, tq, tk):
    n_heads = q_ref.shape[2] // HEAD_DIM
    hg = pl.program_id(1)
    qi = pl.program_id(2)
    krow = lax.broadcasted_iota(jnp.int32, (tk, tq), 0)
    qcol = lax.broadcasted_iota(jnp.int32, (tk, tq), 1)
    lane = lax.broadcasted_iota(jnp.int32, (tq, LANES), 1)
    qs = []
    for hd in range(n_heads):
        pick = (lane % FOX_HEADS == hg * n_heads + hd) & (lane < 3 * FOX_HEADS)
        sel = jnp.where(pick, -1.0, 0.0).astype(BF16)
        qfull = q_ref[0, :, (hd // 2) * LANES:(hd // 2 + 1) * LANES]
        qs.append(jnp.concatenate([jnp.where(_head_mask(hd % 2), qfull, jnp.zeros_like(qfull)), sel], axis=1))

    def qk(hd, kt):
        ks = pl.multiple_of(kt * tk, tk)
        lanes = slice((hd // 2) * LANES, (hd // 2 + 1) * LANES)
        k = jnp.concatenate([k_ref[0, pl.ds(ks, tk), lanes], c_ref[0, pl.ds(ks, tk), :]], axis=1)
        return _qk(k, qs[hd])

    def values(hd, kt):
        return vt_ref[0, hd * HEAD_DIM:(hd + 1) * HEAD_DIM, pl.ds(pl.multiple_of(kt * tk, tk), tk)]

    def causal(s):
        return jnp.where(krow <= qcol, s, NEG_LOGIT)

    outs = _attend(n_heads, qi + 1, qk, values, scratch, mask_last=causal)
    for hp in range(n_heads // 2):
        o_ref[0, :, hp * LANES:(hp + 1) * LANES] = _pair_out(outs[2 * hp:2 * hp + 2])


def _fox_attention(fq, fk, cparts, fvt):
    b, s, w = fq.shape
    tq = tk = ATT_TQ
    gw = FOX_STEP_HEADS * HEAD_DIM
    qspec = pl.BlockSpec((1, tq, gw), lambda bi, hg, qi: (bi, qi, hg))
    kspec = pl.BlockSpec((1, s, gw), lambda bi, hg, qi: (bi, 0, hg))
    cspec = pl.BlockSpec((1, s, LANES), lambda bi, hg, qi: (bi, 0, 0))
    vspec = pl.BlockSpec((1, gw, s), lambda bi, hg, qi: (bi, hg, 0))
    return pl.pallas_call(
        functools.partial(_fox_body, tq=tq, tk=tk),
        grid=(b, w // gw, s // tq),
        in_specs=[qspec, kspec, cspec, vspec],
        out_specs=qspec,
        out_shape=jax.ShapeDtypeStruct((b, s, w), BF16),
        scratch_shapes=_attend_scratch(FOX_STEP_HEADS, tq, tk, HEAD_DIM),
        compiler_params=_cparams(("arbitrary", "arbitrary", "arbitrary")),
        name="fox_attention",
    )(fq, fk, cparts, fvt)


def _hi_key_to_bits(h):
    return jnp.where(h >= 0x8000, h - 0x8000, (~h) & 0xFFFF)


def _tree_sum(parts):
    while len(parts) > 1:
        parts = [a + b for a, b in zip(parts[0::2], parts[1::2])] + ([parts[-1]] if len(parts) % 2 else [])
    return parts[0]


def _dsa_body(dq_ref, iq_ref, iw_ref, dk_ref, dvt_ref, ik_ref, o_ref, sc_ref, a1_ref, d1_ref, d0_ref, a2_ref,
              *scratch, tq, tqa, tk, top_k):
    qi = pl.program_id(1)
    n_tiles = ((qi + 1) * tq + tk - 1) // tk
    qpos = lax.broadcasted_iota(jnp.int32, (1, tq), 1) + qi * tq
    limit = (qpos // CHUNK + 1) * CHUNK
    krow = lax.broadcasted_iota(jnp.int32, (tk, tq), 0)
    kf = jnp.float32(top_k)

    iqf = iq_ref[0]
    iw = iw_ref[0]
    iqs = [jnp.where(_head_mask(h % 2), iqf[:, (h // 2) * LANES:(h // 2 + 1) * LANES], jnp.zeros((tq, LANES), BF16))
           for h in range(IDX_HEADS)]
    ws = [iw[h:h + 1, :] for h in range(IDX_HEADS)]

    def score_tile(kt, _):
        ks = pl.multiple_of(kt * tk, tk)
        ik = ik_ref[0, pl.ds(ks, tk), :]
        sc = None
        for h in range(IDX_HEADS):
            term = jnp.maximum(_qk(ik, iqs[h]), 0.0) * ws[h]
            sc = term if sc is None else sc + term
        sc = jnp.where(krow + ks < limit, sc, NEG_SCORE)
        sc = jnp.where(sc == 0.0, 0.0, sc)
        sc_ref[pl.ds(ks, tk), :] = sc
        bits = pltpu.bitcast(sc, jnp.int32)
        a1_ref[pl.ds(ks, tk), :] = pltpu.bitcast(bits & jnp.int32(-65536), F32).astype(BF16)
        lo = bits & 0xFFFF
        klo = jnp.where(bits < 0, 0xFFFF - lo, lo)
        d1_ref[pl.ds(ks, tk), :] = lax.shift_right_logical(klo, 8).astype(F32).astype(BF16)
        d0_ref[pl.ds(ks, tk), :] = (klo & 0xFF).astype(F32).astype(BF16)
        return 0

    lax.fori_loop(0, n_tiles, score_tile, 0)

    def tile_count(mask):
        c = jnp.where(mask, jnp.ones((), BF16), jnp.zeros((), BF16))
        return _tree_sum([c[i * COUNT_ROWS:(i + 1) * COUNT_ROWS, :] for i in range(tk // COUNT_ROWS)]).astype(F32)

    zero_cnt = jnp.zeros((COUNT_ROWS, tq), F32)

    def count_packed(ref, op, ref_b):
        def body(kt, cnt):
            return cnt + tile_count(op(ref[pl.ds(pl.multiple_of(kt * tk, tk), tk), :], ref_b))
        return jnp.sum(lax.fori_loop(0, n_tiles, body, zero_cnt), axis=0, keepdims=True)

    def hi_float(h):
        return pltpu.bitcast(lax.shift_left(_hi_key_to_bits(h), 16), F32).astype(BF16)

    def hi_step(it, h):
        cand = h | lax.shift_left(jnp.int32(1), 15 - it)
        cnt = count_packed(a1_ref, lambda x, c: x >= c, hi_float(cand))
        return jnp.where(cnt >= kf, cand, h)

    h = lax.fori_loop(0, 16, hi_step, jnp.zeros((1, tq), jnp.int32))
    hbits = _hi_key_to_bits(h)
    hbits = jnp.where((hbits & 0x7F80) == 0, hbits & 0x8000, hbits)
    thr_hi = pltpu.bitcast(lax.shift_left(hbits, 16), F32).astype(BF16)

    def restrict(src_ref, match, digit_ref):
        def body(kt, cnt):
            rows = pl.ds(pl.multiple_of(kt * tk, tk), tk)
            x = src_ref[rows, :]
            a2_ref[rows, :] = jnp.where(x == match, digit_ref[rows, :], -jnp.ones((), BF16))
            return cnt + tile_count(x > match)
        return jnp.sum(lax.fori_loop(0, n_tiles, body, zero_cnt), axis=0, keepdims=True)

    def digit(need):
        def step(it, v):
            cand = v + lax.shift_left(jnp.int32(1), 7 - it).astype(F32)
            cnt = count_packed(a2_ref, lambda x, c: x >= c, cand.astype(BF16))
            return jnp.where(cnt >= need, cand, v)
        return lax.fori_loop(0, 8, step, jnp.zeros((1, tq), F32))

    need_hi = kf - restrict(a1_ref, thr_hi, d1_ref)
    v1 = digit(need_hi)
    need_lo = need_hi - restrict(a2_ref, v1.astype(BF16), d0_ref)
    v0 = digit(need_lo)
    klo = (v1 * 256.0 + v0).astype(jnp.int32)
    lo = jnp.where(hbits >= 0x8000, 0xFFFF - klo, klo)
    thr = pltpu.bitcast(lax.shift_left(hbits, 16) | lo, F32)

    v0b = v0.astype(BF16)

    need = need_lo - count_packed(a2_ref, lambda x, c: x > c, v0b)
    need = jnp.where(thr <= NEG_SCORE, 0.0, need)

    lower = jnp.where(lax.broadcasted_iota(jnp.int32, (tk, tk), 0) >= lax.broadcasted_iota(jnp.int32, (tk, tk), 1),
                      1.0, 0.0).astype(BF16)

    def bias_tile(kt, seen):
        rows = pl.ds(pl.multiple_of(kt * tk, tk), tk)
        sc = sc_ref[rows, :]
        tied = jnp.where(a2_ref[rows, :] == v0b, jnp.ones((), BF16), jnp.zeros((), BF16))
        rank = jnp.dot(lower, tied, preferred_element_type=F32) + seen
        tie_bias = jnp.where(rank <= need, 0.0, NEG_LOGIT)
        sc_ref[rows, :] = jnp.where(sc > thr, 0.0, jnp.where(sc == thr, tie_bias, NEG_LOGIT))
        return rank[tk - 1:tk, :]

    lax.fori_loop(0, n_tiles, bias_tile, jnp.zeros((1, tq), F32))

    n_heads = dq_ref.shape[2] // HEAD_DIM

    def values(hd, kt):
        return dvt_ref[0, hd * HEAD_DIM:(hd + 1) * HEAD_DIM, pl.ds(pl.multiple_of(kt * tk, tk), tk)]

    for sb in range(tq // tqa):
        qrows = slice(sb * tqa, (sb + 1) * tqa)
        qs = []
        for hd in range(n_heads):
            qfull = dq_ref[0, qrows, (hd // 2) * LANES:(hd // 2 + 1) * LANES]
            qs.append(jnp.where(_head_mask(hd % 2), qfull, jnp.zeros_like(qfull)))

        def qk(hd, kt, qs=qs, qrows=qrows):
            ks = pl.multiple_of(kt * tk, tk)
            lanes = slice((hd // 2) * LANES, (hd // 2 + 1) * LANES)
            return _qk(dk_ref[0, pl.ds(ks, tk), lanes], qs[hd]) + sc_ref[pl.ds(ks, tk), qrows]

        sub_tiles = (qi * tq + (sb + 1) * tqa + tk - 1) // tk
        outs = _attend(n_heads, sub_tiles, qk, values, scratch)
        for hp in range(n_heads // 2):
            o_ref[0, qrows, hp * LANES:(hp + 1) * LANES] = _pair_out(outs[2 * hp:2 * hp + 2])


def _dsa_attention(dq, dk, dvt, iq, ik, iw):
    b, s, w = dq.shape
    tq, tqa, tk = DSA_TQ, DSA_ATT_TQ, DSA_TK
    top_k = min(DSA_TOPK_MAX, s // 4)

    def qspec(width):
        return pl.BlockSpec((1, tq, width), lambda bi, qi: (bi, qi, 0))

    def kspec(width):
        return pl.BlockSpec((1, s, width), lambda bi, qi: (bi, 0, 0))

    return pl.pallas_call(
        functools.partial(_dsa_body, tq=tq, tqa=tqa, tk=tk, top_k=top_k),
        grid=(b, s // tq),
        in_specs=[qspec(w), qspec(iq.shape[2]), pl.BlockSpec((1, iw.shape[1], tq), lambda bi, qi: (bi, 0, qi)),
                  kspec(w), pl.BlockSpec((1, w, s), lambda bi, qi: (bi, 0, 0)), kspec(LANES)],
        out_specs=qspec(w),
        out_shape=jax.ShapeDtypeStruct((b, s, w), BF16),
        scratch_shapes=[pltpu.VMEM((s, tq), F32)] + [pltpu.VMEM((s, tq), BF16)] * 4
        + _attend_scratch(w // HEAD_DIM, tqa, tk, HEAD_DIM),
        compiler_params=_cparams(("arbitrary", "arbitrary")),
        name="dsa_attention",
    )(dq, iq, iw, dk, dvt, ik)


MLA_DOWN_COLS = MLA_Q_LORA + MLA_KV_LORA + 2 * MLA_ROPE


def _mla_proj_body(h_ref, sh_ref, sc_ref, g_ref, wd_ref, qn_ref, kvn_ref, wuq_ref, wukv_ref, cos_ref, sin_ref,
                   qnope_ref, qrope_ref, knope_ref, v_ref, krope_ref, *, scale):
    x = h_ref[...]
    u = _modulated_norm(x, g_ref[...], sc_ref[0], sh_ref[0]).astype(BF16)
    cos = cos_ref[...]
    sin = sin_ref[...]
    half = MLA_ROPE // 2
    down = jnp.dot(u, wd_ref[...], preferred_element_type=F32)

    def norm(z, g):
        ms = jnp.mean(z * z, axis=-1, keepdims=True)
        return (z * lax.rsqrt(ms + RMS_EPS) * g).astype(BF16)

    cq = norm(down[:, 0:MLA_Q_LORA], qn_ref[...])
    ckv = norm(down[:, MLA_Q_LORA:MLA_Q_LORA + MLA_KV_LORA], kvn_ref[...])
    krope_ref[0] = _rope_slab(down[:, MLA_Q_LORA + MLA_KV_LORA:], cos, sin, half).astype(BF16)

    n_nope = MLA_HEADS * MLA_NOPE
    qn = jnp.dot(cq, wuq_ref[:, 0:n_nope], preferred_element_type=F32)
    qnope_ref[0] = (qn * scale).astype(BF16)
    qr = jnp.dot(cq, wuq_ref[:, n_nope:], preferred_element_type=F32)
    qr = jnp.concatenate(
        [_rope_slab(qr[:, j * LANES:(j + 1) * LANES], cos, sin, half) for j in range(qr.shape[1] // LANES)], axis=1)
    qrope_ref[0] = (qr * scale).astype(BF16)
    knope_ref[0] = jnp.dot(ckv, wukv_ref[:, 0:n_nope], preferred_element_type=F32).astype(BF16)
    v_ref[0] = jnp.dot(ckv, wukv_ref[:, n_nope:], preferred_element_type=F32).T.astype(BF16)


def _mla_proj(h, sh, sc, g, wd, qn, kvn, wuq, wukv, cos, sin, b, s):
    t, d = h.shape
    tpb = s // TM
    row = pl.BlockSpec((TM, d), lambda i: (i, 0))
    mod = pl.BlockSpec((1, 1, d), lambda i: (i // tpb, 0, 0))
    tab = pl.BlockSpec((TM, LANES), lambda i: (i, 0))

    def out(width):
        return pl.BlockSpec((1, TM, width), lambda i: (i // tpb, i % tpb, 0))

    def shp(width):
        return jax.ShapeDtypeStruct((b, s, width), BF16)

    n_nope = MLA_HEADS * MLA_NOPE
    n_rope = MLA_HEADS * MLA_ROPE
    n_v = MLA_HEADS * MLA_V
    scale = (MLA_NOPE + MLA_ROPE) ** -0.5 * LOG2E
    return pl.pallas_call(
        functools.partial(_mla_proj_body, scale=scale),
        grid=(t // TM,),
        in_specs=[row, mod, mod, _const_spec((1, d)), _const_spec((d, MLA_DOWN_COLS)),
                  _const_spec((1, MLA_Q_LORA)), _const_spec((1, MLA_KV_LORA)),
                  _const_spec((MLA_Q_LORA, n_nope + n_rope)), _const_spec((MLA_KV_LORA, n_nope + n_v)), tab, tab],
        out_specs=[out(n_nope), out(n_rope), out(n_nope),
                   pl.BlockSpec((1, n_v, TM), lambda i: (i // tpb, 0, i % tpb)), out(LANES)],
        out_shape=[shp(n_nope), shp(n_rope), shp(n_nope), jax.ShapeDtypeStruct((b, n_v, s), BF16), shp(LANES)],
        compiler_params=_cparams(("arbitrary",)),
        name="mla_in_proj",
    )(h, sh, sc, g, wd, qn, kvn, wuq, wukv, cos, sin)


def _mla_body(qn_ref, qr_ref, kn_ref, kr_ref, vt_ref, o_ref, *scratch, tq, tk):
    n_heads = qn_ref.shape[2] // MLA_NOPE
    qi = pl.program_id(2)
    qs = []
    for hd in range(n_heads):
        qr = qr_ref[0, :, (hd // 2) * LANES:(hd // 2 + 1) * LANES]
        qs.append(jnp.concatenate([qn_ref[0, :, hd * LANES:(hd + 1) * LANES],
                                   jnp.where(_head_mask(hd % 2), qr, jnp.zeros_like(qr))], axis=1))
    krow = lax.broadcasted_iota(jnp.int32, (tk, tq), 0)
    qcol = lax.broadcasted_iota(jnp.int32, (tk, tq), 1)

    def qk(hd, kt):
        ks = pl.multiple_of(kt * tk, tk)
        k = jnp.concatenate([kn_ref[0, pl.ds(ks, tk), hd * LANES:(hd + 1) * LANES], kr_ref[0, pl.ds(ks, tk), :]],
                            axis=1)
        return _qk(k, qs[hd])

    def values(hd, kt):
        return vt_ref[0, hd * LANES:(hd + 1) * LANES, pl.ds(pl.multiple_of(kt * tk, tk), tk)]

    def chunk_causal(s):
        return jnp.where(krow // CHUNK <= qcol // CHUNK, s, NEG_LOGIT)

    outs = _attend(n_heads, qi + 1, qk, values, scratch, mask_last=chunk_causal)
    for hd in range(n_heads):
        o_ref[0, :, hd * LANES:(hd + 1) * LANES] = outs[hd].T.astype(BF16)


def _mla_attention(qnope, qrope, knope, krope, vt):
    b, s, w = qnope.shape
    tq = tk = ATT_TQ
    gw = ATT_HEADS * MLA_NOPE
    gr = ATT_HEADS * MLA_ROPE
    qspec = pl.BlockSpec((1, tq, gw), lambda bi, hg, qi: (bi, qi, hg))
    qrspec = pl.BlockSpec((1, tq, gr), lambda bi, hg, qi: (bi, qi, hg))
    kspec = pl.BlockSpec((1, s, gw), lambda bi, hg, qi: (bi, 0, hg))
    krspec = pl.BlockSpec((1, s, LANES), lambda bi, hg, qi: (bi, 0, 0))
    vspec = pl.BlockSpec((1, gw, s), lambda bi, hg, qi: (bi, hg, 0))
    return pl.pallas_call(
        functools.partial(_mla_body, tq=tq, tk=tk),
        grid=(b, w // gw, s // tq),
        in_specs=[qspec, qrspec, kspec, krspec, vspec],
        out_specs=qspec,
        out_shape=jax.ShapeDtypeStruct((b, s, w), BF16),
        scratch_shapes=_attend_scratch(ATT_HEADS, tq, tk, LANES),
        compiler_params=_cparams(("arbitrary", "arbitrary", "arbitrary")),
        name="mla_attention",
    )(qnope, qrope, knope, krope, vt)


def _hyb_weight(w_in, b_f):
    o = 0
    parts = {}
    for name, width in (("fq", 512), ("fk", 512), ("fv", 512), ("ff", FOX_HEADS), ("dq", 512), ("dk", 512),
                        ("dv", 512), ("iq", 256), ("iw", IDX_HEADS), ("ik", HEAD_DIM)):
        parts[name] = w_in[:, o:o + width]
        o += width
    qs = HEAD_DIM ** -0.5
    pad = jnp.zeros((w_in.shape[0], LANES - FOX_HEADS - IDX_HEADS), w_in.dtype)
    w = jnp.concatenate([parts["fq"] * qs, parts["fk"], parts["fv"], parts["dq"] * qs, parts["dk"], parts["dv"],
                         parts["iq"] * qs, parts["ik"], parts["ik"],
                         parts["ff"], parts["iw"] * IDX_HEADS ** -0.5, pad], axis=1)
    bf = jnp.concatenate([b_f, jnp.zeros((LANES - FOX_HEADS,), b_f.dtype)]).reshape(1, LANES)
    return w.astype(BF16), bf.astype(F32)


def _mla_weights(w_down, w_uq, w_ukv):
    kr = w_down[:, MLA_Q_LORA + MLA_KV_LORA:]
    wd = jnp.concatenate([w_down, kr], axis=1)
    uq = w_uq.reshape(MLA_Q_LORA, MLA_HEADS, MLA_NOPE + MLA_ROPE)
    wuq = jnp.concatenate([uq[:, :, :MLA_NOPE].reshape(MLA_Q_LORA, -1), uq[:, :, MLA_NOPE:].reshape(MLA_Q_LORA, -1)],
                          axis=1)
    ukv = w_ukv.reshape(MLA_KV_LORA, MLA_HEADS, MLA_NOPE + MLA_V)
    wukv = jnp.concatenate([ukv[:, :, :MLA_NOPE].reshape(MLA_KV_LORA, -1),
                            ukv[:, :, MLA_NOPE:].reshape(MLA_KV_LORA, -1)], axis=1)
    return wd.astype(BF16), wuq.astype(BF16), wukv.astype(BF16)


def kernel(x, c, positions, ada_w, ada_b, norm_g, ffn_w_gate, ffn_w_up, ffn_w_down, hyb_w_in, fox_b_f, hyb_w_out,
           mla_w_down, mla_q_norm, mla_kv_norm, mla_w_uq, mla_w_ukv, mla_w_out, final_g):
    b, s, d = x.shape
    depth = ada_w.shape[0]
    t = b * s
    assert s % TM == 0 and s % ATT_TQ == 0 and s % DSA_TQ == 0 and s % DSA_TK == 0
    assert DSA_TQ % DSA_ATT_TQ == 0 and ffn_w_gate.shape[-1] % FFN_CHUNK == 0

    mod = _modulation(c, ada_w, ada_b).reshape(depth, b, N_MOD, 1, d)
    cos_d, sin_d, cos_m, sin_m = _rope_tables(positions)
    wg = ffn_w_gate.astype(BF16)
    wu = ffn_w_up.astype(BF16)
    wdn = ffn_w_down.astype(BF16)

    h = x.reshape(t, d)
    for i in range(depth):
        sh1, sc1, g1, sh2, sc2, g2, sh3, sc3, g3 = [mod[i, :, j] for j in range(N_MOD)]
        ng = norm_g[i].reshape(3, 1, d)
        h = _ffn(h, sh1, sc1, g1, ng[0], wg[i, 0], wu[i, 0], wdn[i, 0], s)
        j = i // 2
        if i % 2 == 0:
            w_in, bf = _hyb_weight(hyb_w_in[j], fox_b_f[j])
            fq, fk, fvt, dq, dk, dvt, iq, ik, cparts, iw = _hyb_proj(h, sh2, sc2, ng[1], w_in, bf, cos_d, sin_d, b, s)
            out_a = _fox_attention(fq, fk, cparts, fvt)
            out_b = _dsa_attention(dq, dk, dvt, iq, ik, iw)
            w_out = hyb_w_out[j].astype(BF16)
            half = out_a.shape[2]
            mix = (g2, [out_a.reshape(t, half), out_b.reshape(t, half)], [w_out[:half], w_out[half:]])
        else:
            wd, wuq, wukv = _mla_weights(mla_w_down[j], mla_w_uq[j], mla_w_ukv[j])
            qn, qr, kn, vt, kr = _mla_proj(h, sh2, sc2, ng[1], wd, mla_q_norm[j].reshape(1, -1),
                                           mla_kv_norm[j].reshape(1, -1), wuq, wukv, cos_m, sin_m, b, s)
            out = _mla_attention(qn, qr, kn, kr, vt)
            mix = (g2, [out.reshape(t, -1)], [mla_w_out[j].astype(BF16)])
        last = i == depth - 1
        h = _ffn(h, sh3, sc3, g3, ng[2], wg[i, 1], wu[i, 1], wdn[i, 1], s,
                 final_g=final_g.reshape(1, d) if last else None, mix=mix)
    return h.reshape(b, s, d)
```

```python
import functools
import math

import jax
import jax.numpy as jnp
from jax import lax
from jax.experimental import pallas as pl
from jax.experimental.pallas import tpu as pltpu

F32 = jnp.float32
BF16 = jnp.bfloat16

CHUNK = 64
RMS_EPS = 1e-6
ROPE_THETA = 500000.0
MLA_ROPE_THETA = 10000.0
MACARON_WEIGHT = 0.5
N_MOD = 9
FOX_HEADS = 8
HEAD_DIM = 64
DSA_ROT = 16
IDX_HEADS = 4
DSA_TOPK_MAX = 256
MLA_HEADS = 8
MLA_NOPE = 128
MLA_ROPE = 64
MLA_V = 128
MLA_Q_LORA = 384
MLA_KV_LORA = 256

LANES = 128
VMEM_LIMIT = 56 * 1024 * 1024

LOG2E = math.log2(math.e)
NEG_LOGIT = -1e30
NEG_SCORE = -3e38

TM = 1024
FFN_TM = 512
FFN_CHUNK = 256
ATT_TQ = 512
ATT_TK = 512
ATT_HEADS = 8
FOX_STEP_HEADS = 8
DSA_TQ = 256
DSA_ATT_TQ = 256
DSA_TK = 512
COUNT_ROWS = 32


def _cparams(sem):
    return pltpu.CompilerParams(dimension_semantics=sem, vmem_limit_bytes=VMEM_LIMIT)


def _const_spec(shape):
    nd = len(shape)
    return pl.BlockSpec(shape, lambda *_: (0,) * nd, pipeline_mode=pl.Buffered(1))


def _silu(x):
    return x * jax.nn.sigmoid(x)


def _modulated_norm(x, g, sc, sh):
    ms = jnp.mean(x * x, axis=-1, keepdims=True)
    y = x * lax.rsqrt(ms + RMS_EPS) * g
    return y * (1.0 + sc) + sh


def _mod_body(c_ref, w_ref, b_ref, o_ref):
    cond = _silu(c_ref[...]).astype(BF16)
    o_ref[0] = jnp.dot(cond, w_ref[0].astype(BF16), preferred_element_type=F32) + b_ref[0]


def _modulation(c, ada_w, ada_b):
    depth, d, n = ada_w.shape
    b = c.shape[0]
    tn = n // N_MOD
    return pl.pallas_call(
        _mod_body,
        grid=(depth, n // tn),
        in_specs=[pl.BlockSpec((b, d), lambda i, j: (0, 0)),
                  pl.BlockSpec((1, d, tn), lambda i, j: (i, 0, j)),
                  pl.BlockSpec((1, 1, tn), lambda i, j: (i, 0, j))],
        out_specs=pl.BlockSpec((1, b, tn), lambda i, j: (i, 0, j)),
        out_shape=jax.ShapeDtypeStruct((depth, b, n), F32),
        compiler_params=_cparams(("arbitrary", "arbitrary")),
        name="adaln_mod",
    )(c, ada_w, ada_b.reshape(depth, 1, n))


def _rope_table_body(pos_ref, invd_ref, sgnd_ref, invm_ref, sgnm_ref, cd_ref, sd_ref, cm_ref, sm_ref):
    pos = pos_ref[...].astype(F32)
    for inv_ref, sgn_ref, c_ref, s_ref in ((invd_ref, sgnd_ref, cd_ref, sd_ref),
                                           (invm_ref, sgnm_ref, cm_ref, sm_ref)):
        ang = pos * inv_ref[...]
        sgn = sgn_ref[...]
        c_ref[...] = jnp.where(sgn != 0.0, jnp.cos(ang), 1.0)
        s_ref[...] = sgn * jnp.sin(ang)


def _lane_pattern(rot, theta):
    half = rot // 2
    inv_freq = jnp.exp(-math.log(theta) * 2.0 * jnp.arange(half, dtype=F32) / rot)
    d = jnp.arange(LANES) % HEAD_DIM
    inv = jnp.where(d < rot, inv_freq[d % half], 0.0).astype(F32)
    sgn = jnp.where(d < half, -1.0, jnp.where(d < rot, 1.0, 0.0)).astype(F32)
    return inv.reshape(1, LANES), sgn.reshape(1, LANES)


def _rope_tables(positions):
    t = positions.size
    invd, sgnd = _lane_pattern(DSA_ROT, ROPE_THETA)
    invm, sgnm = _lane_pattern(MLA_ROPE, MLA_ROPE_THETA)
    tm = 1024
    row = pl.BlockSpec((tm, LANES), lambda i: (i, 0))
    vec = pl.BlockSpec((1, LANES), lambda i: (0, 0))
    tab = jax.ShapeDtypeStruct((t, LANES), F32)
    return pl.pallas_call(
        _rope_table_body,
        grid=(t // tm,),
        in_specs=[pl.BlockSpec((tm, 1), lambda i: (i, 0)), vec, vec, vec, vec],
        out_specs=[row, row, row, row],
        out_shape=[tab, tab, tab, tab],
        compiler_params=_cparams(("arbitrary",)),
        name="rope_tables",
    )(positions.reshape(t, 1), invd, sgnd, invm, sgnm)


def _rope_slab(y, cos, sin, half):
    lane = lax.broadcasted_iota(jnp.int32, (1, LANES), 1) % HEAD_DIM
    first = lane < half
    partner = jnp.where(first, pltpu.roll(y, LANES - half, 1), pltpu.roll(y, half, 1))
    return y * cos + partner * sin


def _ffn_body(h_ref, sh_ref, sc_ref, gt_ref, g_ref, wg_ref, wu_ref, wd_ref, *rest, nf, fc, final, n_mix):
    rest = list(rest)
    x = h_ref[...]
    if n_mix:
        mg_ref = rest.pop(0)
        mix = None
        for x_ref, w_ref in zip(rest[:n_mix], rest[n_mix:2 * n_mix]):
            part = jnp.dot(x_ref[...], w_ref[...], preferred_element_type=F32)
            mix = part if mix is None else mix + part
        x = x + mg_ref[0] * mix
        rest = rest[2 * n_mix:]
    if final:
        fg_ref, o_ref, acc_ref = rest
    else:
        o_ref, acc_ref = rest
    u = _modulated_norm(x, g_ref[...], sc_ref[0], sh_ref[0]).astype(BF16)
    for f in range(nf):
        sl = slice(f * fc, (f + 1) * fc)
        gp = jnp.dot(u, wg_ref[:, sl], preferred_element_type=F32)
        up = jnp.dot(u, wu_ref[:, sl], preferred_element_type=F32)
        a = (_silu(gp) * up).astype(BF16)
        d = jnp.dot(a, wd_ref[sl, :], preferred_element_type=F32)
        if f == 0:
            acc_ref[...] = d
        else:
            acc_ref[...] += d
    y = x + (MACARON_WEIGHT * gt_ref[0]) * acc_ref[...]
    if final:
        ms = jnp.mean(y * y, axis=-1, keepdims=True)
        y = y * lax.rsqrt(ms + RMS_EPS) * fg_ref[...]
    o_ref[...] = y


def _ffn(h, sh, sc, gt, g, wg, wu, wd, s, final_g=None, mix=None):
    t, d = h.shape
    f = wg.shape[1]
    tm = FFN_TM
    tpb = s // tm
    row = pl.BlockSpec((tm, d), lambda i: (i, 0))
    mod = pl.BlockSpec((1, 1, d), lambda i: (i // tpb, 0, 0))
    in_specs = [row, mod, mod, mod, _const_spec((1, d)),
                _const_spec((d, f)), _const_spec((d, f)), _const_spec((f, d))]
    args = [h, sh, sc, gt, g, wg, wu, wd]
    n_mix = 0
    if mix is not None:
        mgate, xs, ws = mix
        n_mix = len(xs)
        in_specs += [mod] + [pl.BlockSpec((tm, x.shape[1]), lambda i: (i, 0)) for x in xs] + [_const_spec(w.shape) for w in ws]
        args += [mgate, *xs, *ws]
    if final_g is not None:
        in_specs.append(_const_spec((1, d)))
        args.append(final_g)
    return pl.pallas_call(
        functools.partial(_ffn_body, nf=f // FFN_CHUNK, fc=FFN_CHUNK, final=final_g is not None, n_mix=n_mix),
        grid=(t // tm,),
        in_specs=in_specs,
        out_specs=row,
        out_shape=jax.ShapeDtypeStruct((t, d), F32),
        scratch_shapes=[pltpu.VMEM((tm, d), F32)],
        compiler_params=_cparams(("arbitrary",)),
        name="swiglu_half_step",
    )(*args)


HYB_W = 512
HYB_COLS = 6 * HYB_W + 256 + 128 + 128


def _hyb_proj_body(h_ref, sh_ref, sc_ref, g_ref, w_ref, bf_ref, cos_ref, sin_ref,
                   fq_ref, fk_ref, fvt_ref, dq_ref, dk_ref, dvt_ref, iq_ref, ik_ref, cum_ref, iw_ref,
                   carry_ref, *, tpb):
    i = pl.program_id(0)
    x = h_ref[...]
    tm = x.shape[0]
    u = _modulated_norm(x, g_ref[...], sc_ref[0], sh_ref[0]).astype(BF16)
    cos = cos_ref[...]
    sin = sin_ref[...]
    half = DSA_ROT // 2

    def proj(c0, width):
        return jnp.dot(u, w_ref[:, c0:c0 + width], preferred_element_type=F32)

    def roped(y):
        return jnp.concatenate(
            [_rope_slab(y[:, j * LANES:(j + 1) * LANES], cos, sin, half) for j in range(y.shape[1] // LANES)],
            axis=1)

    fq_ref[0] = (proj(0 * HYB_W, HYB_W) * LOG2E).astype(BF16)
    fk_ref[0] = proj(1 * HYB_W, HYB_W).astype(BF16)
    fvt_ref[0] = proj(2 * HYB_W, HYB_W).T.astype(BF16)
    dq_ref[0] = (roped(proj(3 * HYB_W, HYB_W)) * LOG2E).astype(BF16)
    dk_ref[0] = roped(proj(4 * HYB_W, HYB_W)).astype(BF16)
    dvt_ref[0] = proj(5 * HYB_W, HYB_W).T.astype(BF16)
    tail = proj(6 * HYB_W, 512)
    iq_ref[0] = roped(tail[:, 0:256]).astype(BF16)
    ik_ref[0] = roped(tail[:, 256:384]).astype(BF16)
    gates = tail[:, 384:512]
    iw_ref[0] = gates.T[FOX_HEADS:2 * FOX_HEADS, :]

    z = gates + bf_ref[...]
    logf = jnp.minimum(z, 0.0) - jnp.log1p(jnp.exp(-jnp.abs(z)))
    rows = lax.broadcasted_iota(jnp.int32, (tm, LANES), 0)
    c = logf
    k = 1
    while k < tm:
        c = c + jnp.where(rows >= k, pltpu.roll(c, k, 0), 0.0)
        k *= 2

    @pl.when(i % tpb == 0)
    def _():
        carry_ref[...] = jnp.zeros_like(carry_ref)

    c = c + carry_ref[0:1, :]
    carry_ref[...] = jnp.broadcast_to(c[tm - 1:tm, :], carry_ref.shape)
    lane = lax.broadcasted_iota(jnp.int32, (1, LANES), 1)
    c = jnp.where(lane < FOX_HEADS, c * LOG2E, 0.0)
    hi = c.astype(BF16).astype(F32)
    mid = (c - hi).astype(BF16).astype(F32)
    lo = (c - hi - mid).astype(BF16).astype(F32)
    cum_ref[0] = (hi + pltpu.roll(mid, FOX_HEADS, 1) + pltpu.roll(lo, 2 * FOX_HEADS, 1)).astype(BF16)


def _hyb_proj(h, sh, sc, g, w, bf, cos, sin, b, s):
    t, d = h.shape
    tpb = s // TM
    row = pl.BlockSpec((TM, d), lambda i: (i, 0))
    mod = pl.BlockSpec((1, 1, d), lambda i: (i // tpb, 0, 0))
    tab = pl.BlockSpec((TM, LANES), lambda i: (i, 0))

    def out(width):
        return pl.BlockSpec((1, TM, width), lambda i: (i // tpb, i % tpb, 0))

    def shp(width, dt=BF16):
        return jax.ShapeDtypeStruct((b, s, width), dt)

    def out_t(width):
        return pl.BlockSpec((1, width, TM), lambda i: (i // tpb, 0, i % tpb))

    def shp_t(width, dt=BF16):
        return jax.ShapeDtypeStruct((b, width, s), dt)

    return pl.pallas_call(
        functools.partial(_hyb_proj_body, tpb=tpb),
        grid=(t // TM,),
        in_specs=[row, mod, mod, _const_spec((1, d)), _const_spec((d, HYB_COLS)), _const_spec((1, LANES)), tab, tab],
        out_specs=[out(HYB_W), out(HYB_W), out_t(HYB_W), out(HYB_W), out(HYB_W), out_t(HYB_W),
                   out(256), out(LANES), out(LANES), out_t(FOX_HEADS)],
        out_shape=[shp(HYB_W), shp(HYB_W), shp_t(HYB_W), shp(HYB_W), shp(HYB_W), shp_t(HYB_W),
                   shp(256), shp(LANES), shp(LANES), shp_t(FOX_HEADS, F32)],
        scratch_shapes=[pltpu.VMEM((8, LANES), F32)],
        compiler_params=_cparams(("arbitrary",)),
        name="hybrid_in_proj",
    )(h, sh, sc, g, w, bf, cos, sin)


def _attend_scratch(n_heads, tq, tk, n):
    return [pltpu.VMEM((2, n_heads, tk, tq), F32), pltpu.VMEM((2, n_heads, 1, tq), F32),
            pltpu.VMEM((n_heads, 1, tq), F32), pltpu.VMEM((n_heads, 1, tq), F32), pltpu.VMEM((n_heads, n, tq), F32)]


def _attend(n_heads, n_tiles, qk, values, scratch, mask_last=None):
    s_ref, smax_ref, m_ref, l_ref, acc_ref = scratch

    def stage(slot, kt):
        for hd in range(n_heads):
            s = qk(hd, kt)
            s_ref[slot, hd] = s
            smax_ref[slot, hd] = jnp.max(s, axis=0, keepdims=True)

    def step(slot, kt, mask=None):
        for hd in range(n_heads):
            s = s_ref[slot, hd]
            smax = smax_ref[slot, hd]
            if mask is not None:
                s = mask(s)
                smax = jnp.max(s, axis=0, keepdims=True)
            m = m_ref[hd]
            m_new = jnp.maximum(m, smax)
            alpha = jnp.exp2(m - m_new)
            p = jnp.exp2(s - m_new)
            m_ref[hd] = m_new
            l_ref[hd] = alpha * l_ref[hd] + jnp.sum(p, axis=0, keepdims=True)
            acc_ref[hd] = alpha * acc_ref[hd] + jnp.dot(values(hd, kt), p.astype(BF16), preferred_element_type=F32)

    m_ref[...] = jnp.full(m_ref.shape, NEG_LOGIT, F32)
    l_ref[...] = jnp.zeros(l_ref.shape, F32)
    acc_ref[...] = jnp.zeros(acc_ref.shape, F32)
    stage(0, 0)
    n_pairs = (n_tiles - 1) // 2

    def pair(i, _):
        stage(1, 2 * i + 1)
        step(0, 2 * i)
        stage(0, 2 * i + 2)
        step(1, 2 * i + 1)
        return 0

    lax.fori_loop(0, n_pairs, pair, 0)
    odd_tail = n_tiles - 2 * n_pairs == 2

    @pl.when(odd_tail)
    def _():
        stage(1, n_tiles - 1)
        step(0, n_tiles - 2)

    step(jnp.where(odd_tail, 1, 0), n_tiles - 1, mask=mask_last)
    return [acc_ref[hd] / l_ref[hd] for hd in range(n_heads)]


def _head_mask(j):
    lane = lax.broadcasted_iota(jnp.int32, (1, LANES), 1)
    return (lane // HEAD_DIM) == j


def _qk(q, k):
    return lax.dot_general(q, k, (((1,), (1,)), ((), ())), preferred_element_type=F32)


def _pair_out(outs):
    return jnp.concatenate(outs, axis=0).T.astype(BF16)


def _fox_body(q_ref, k_ref, c_ref, vt_ref, o_ref, *scratch, tq, tk):
    n_heads = q_ref.shape[2] // HEAD_DIM
    hg = pl.program_id(1)
    qi = pl.program_id(2)
    krow = lax.broadcasted_iota(jnp.int32, (tk, tq), 0)
    qcol = lax.broadcasted_iota(jnp.int32, (tk, tq), 1)
    lane = lax.broadcasted_iota(jnp.int32, (tq, LANES), 1)
    qs = []
    for hd in range(n_heads):
        pick = (lane % FOX_HEADS == hg * n_heads + hd) & (lane < 3 * FOX_HEADS)
        sel = jnp.where(pick, -1.0, 0.0).astype(BF16)
        qfull = q_ref[0, :, (hd // 2) * LANES:(hd // 2 + 1) * LANES]
        qs.append(jnp.concatenate([jnp.where(_head_mask(hd % 2), qfull, jnp.zeros_like(qfull)), sel], axis=1))

    def qk(hd, kt):
        ks = pl.multiple_of(kt * tk, tk)
        lanes = slice((hd // 2) * LANES, (hd // 2 + 1) * LANES)
        k = jnp.concatenate([k_ref[0, pl.ds(ks, tk), lanes], c_ref[0, pl.ds(ks, tk), :]], axis=1)
        return _qk(k, qs[hd])

    def values(hd, kt):
        return vt_ref[0, hd * HEAD_DIM:(hd + 1) * HEAD_DIM, pl.ds(pl.multiple_of(kt * tk, tk), tk)]

    def causal(s):
        return jnp.where(krow <= qcol, s, NEG_LOGIT)

    outs = _attend(n_heads, qi + 1, qk, values, scratch, mask_last=causal)
    for hp in range(n_heads // 2):
        o_ref[0, :, hp * LANES:(hp + 1) * LANES] = _pair_out(outs[2 * hp:2 * hp + 2])


def _fox_attention(fq, fk, cparts, fvt):
    b, s, w = fq.shape
    tq = tk = ATT_TQ
    gw = FOX_STEP_HEADS * HEAD_DIM
    qspec = pl.BlockSpec((1, tq, gw), lambda bi, hg, qi: (bi, qi, hg))
    kspec = pl.BlockSpec((1, s, gw), lambda bi, hg, qi: (bi, 0, hg))
    cspec = pl.BlockSpec((1, s, LANES), lambda bi, hg, qi: (bi, 0, 0))
    vspec = pl.BlockSpec((1, gw, s), lambda bi, hg, qi: (bi, hg, 0))
    return pl.pallas_call(
        functools.partial(_fox_body, tq=tq, tk=tk),
        grid=(b, w // gw, s // tq),
        in_specs=[qspec, kspec, cspec, vspec],
        out_specs=qspec,
        out_shape=jax.ShapeDtypeStruct((b, s, w), BF16),
        scratch_shapes=_attend_scratch(FOX_STEP_HEADS, tq, tk, HEAD_DIM),
        compiler_params=_cparams(("arbitrary", "arbitrary", "arbitrary")),
        name="fox_attention",
    )(fq, fk, cparts, fvt)


def _hi_key_to_bits(h):
    return jnp.where(h >= 0x8000, h - 0x8000, (~h) & 0xFFFF)


def _tree_sum(parts):
    while len(parts) > 1:
        parts = [a + b for a, b in zip(parts[0::2], parts[1::2])] + ([parts[-1]] if len(parts) % 2 else [])
    return parts[0]


def _dsa_body(dq_ref, iq_ref, iw_ref, dk_ref, dvt_ref, ik_ref, o_ref, sc_ref, a1_ref, d1_ref, d0_ref, a2_ref,
              *scratch, tq, tqa, tk, top_k):
    qi = pl.program_id(1)
    n_tiles = ((qi + 1) * tq + tk - 1) // tk
    qpos = lax.broadcasted_iota(jnp.int32, (1, tq), 1) + qi * tq
    limit = (qpos // CHUNK + 1) * CHUNK
    krow = lax.broadcasted_iota(jnp.int32, (tk, tq), 0)
    kf = jnp.float32(top_k)

    iqf = iq_ref[0]
    iw = iw_ref[0]
    iqs = [jnp.where(_head_mask(h % 2), iqf[:, (h // 2) * LANES:(h // 2 + 1) * LANES], jnp.zeros((tq, LANES), BF16))
           for h in range(IDX_HEADS)]
    ws = [iw[h:h + 1, :] for h in range(IDX_HEADS)]

    def score_tile(kt, _):
        ks = pl.multiple_of(kt * tk, tk)
        ik = ik_ref[0, pl.ds(ks, tk), :]
        sc = None
        for h in range(IDX_HEADS):
            term = jnp.maximum(_qk(ik, iqs[h]), 0.0) * ws[h]
            sc = term if sc is None else sc + term
        sc = jnp.where(krow + ks < limit, sc, NEG_SCORE)
        sc = jnp.where(sc == 0.0, 0.0, sc)
        sc_ref[pl.ds(ks, tk), :] = sc
        bits = pltpu.bitcast(sc, jnp.int32)
        a1_ref[pl.ds(ks, tk), :] = pltpu.bitcast(bits & jnp.int32(-65536), F32).astype(BF16)
        lo = bits & 0xFFFF
        klo = jnp.where(bits < 0, 0xFFFF - lo, lo)
        d1_ref[pl.ds(ks, tk), :] = lax.shift_right_logical(klo, 8).astype(F32).astype(BF16)
        d0_ref[pl.ds(ks, tk), :] = (klo & 0xFF).astype(F32).astype(BF16)
        return 0

    lax.fori_loop(0, n_tiles, score_tile, 0)

    def tile_count(mask):
        c = jnp.where(mask, jnp.ones((), BF16), jnp.zeros((), BF16))
        return _tree_sum([c[i * COUNT_ROWS:(i + 1) * COUNT_ROWS, :] for i in range(tk // COUNT_ROWS)]).astype(F32)

    zero_cnt = jnp.zeros((COUNT_ROWS, tq), F32)

    def count_packed(ref, op, ref_b):
        def body(kt, cnt):
            return cnt + tile_count(op(ref[pl.ds(pl.multiple_of(kt * tk, tk), tk), :], ref_b))
        return jnp.sum(lax.fori_loop(0, n_tiles, body, zero_cnt), axis=0, keepdims=True)

    def hi_float(h):
        return pltpu.bitcast(lax.shift_left(_hi_key_to_bits(h), 16), F32).astype(BF16)

    def hi_step(it, h):
        cand = h | lax.shift_left(jnp.int32(1), 15 - it)
        cnt = count_packed(a1_ref, lambda x, c: x >= c, hi_float(cand))
        return jnp.where(cnt >= kf, cand, h)

    h = lax.fori_loop(0, 16, hi_step, jnp.zeros((1, tq), jnp.int32))
    hbits = _hi_key_to_bits(h)
    hbits = jnp.where((hbits & 0x7F80) == 0, hbits & 0x8000, hbits)
    thr_hi = pltpu.bitcast(lax.shift_left(hbits, 16), F32).astype(BF16)

    def restrict(src_ref, match, digit_ref):
        def body(kt, cnt):
            rows = pl.ds(pl.multiple_of(kt * tk, tk), tk)
            x = src_ref[rows, :]
            a2_ref[rows, :] = jnp.where(x == match, digit_ref[rows, :], -jnp.ones((), BF16))
            return cnt + tile_count(x > match)
        return jnp.sum(lax.fori_loop(0, n_tiles, body, zero_cnt), axis=0, keepdims=True)

    def digit(need):
        def step(it, v):
            cand = v + lax.shift_left(jnp.int32(1), 7 - it).astype(F32)
            cnt = count_packed(a2_ref, lambda x, c: x >= c, cand.astype(BF16))
            return jnp.where(cnt >= need, cand, v)
        return lax.fori_loop(0, 8, step, jnp.zeros((1, tq), F32))

    need_hi = kf - restrict(a1_ref, thr_hi, d1_ref)
    v1 = digit(need_hi)
    need_lo = need_hi - restrict(a2_ref, v1.astype(BF16), d0_ref)
    v0 = digit(need_lo)
    klo = (v1 * 256.0 + v0).astype(jnp.int32)
    lo = jnp.where(hbits >= 0x8000, 0xFFFF - klo, klo)
    thr = pltpu.bitcast(lax.shift_left(hbits, 16) | lo, F32)

    v0b = v0.astype(BF16)

    need = need_lo - count_packed(a2_ref, lambda x, c: x > c, v0b)
    need = jnp.where(thr <= NEG_SCORE, 0.0, need)

    lower = jnp.where(lax.broadcasted_iota(jnp.int32, (tk, tk), 0) >= lax.broadcasted_iota(jnp.int32, (tk, tk), 1),
                      1.0, 0.0).astype(BF16)

    def bias_tile(kt, seen):
        rows = pl.ds(pl.multiple_of(kt * tk, tk), tk)
        sc = sc_ref[rows, :]
        tied = jnp.where(a2_ref[rows, :] == v0b, jnp.ones((), BF16), jnp.zeros((), BF16))
        rank = jnp.dot(lower, tied, preferred_element_type=F32) + seen
        tie_bias = jnp.where(rank <= need, 0.0, NEG_LOGIT)
        sc_ref[rows, :] = jnp.where(sc > thr, 0.0, jnp.where(sc == thr, tie_bias, NEG_LOGIT))
        return rank[tk - 1:tk, :]

    lax.fori_loop(0, n_tiles, bias_tile, jnp.zeros((1, tq), F32))

    n_heads = dq_ref.shape[2] // HEAD_DIM

    def values(hd, kt):
        return dvt_ref[0, hd * HEAD_DIM:(hd + 1) * HEAD_DIM, pl.ds(pl.multiple_of(kt * tk, tk), tk)]

    for sb in range(tq // tqa):
        qrows = slice(sb * tqa, (sb + 1) * tqa)
        qs = []
        for hd in range(n_heads):
            qfull = dq_ref[0, qrows, (hd // 2) * LANES:(hd // 2 + 1) * LANES]
            qs.append(jnp.where(_head_mask(hd % 2), qfull, jnp.zeros_like(qfull)))

        def qk(hd, kt, qs=qs, qrows=qrows):
            ks = pl.multiple_of(kt * tk, tk)
            lanes = slice((hd // 2) * LANES, (hd // 2 + 1) * LANES)
            return _qk(dk_ref[0, pl.ds(ks, tk), lanes], qs[hd]) + sc_ref[pl.ds(ks, tk), qrows]

        sub_tiles = (qi * tq + (sb + 1) * tqa + tk - 1) // tk
        outs = _attend(n_heads, sub_tiles, qk, values, scratch)
        for hp in range(n_heads // 2):
            o_ref[0, qrows, hp * LANES:(hp + 1) * LANES] = _pair_out(outs[2 * hp:2 * hp + 2])


def _dsa_attention(dq, dk, dvt, iq, ik, iw):
    b, s, w = dq.shape
    tq, tqa, tk = DSA_TQ, DSA_ATT_TQ, DSA_TK
    top_k = min(DSA_TOPK_MAX, s // 4)

    def qspec(width):
        return pl.BlockSpec((1, tq, width), lambda bi, qi: (bi, qi, 0))

    def kspec(width):
        return pl.BlockSpec((1, s, width), lambda bi, qi: (bi, 0, 0))

    return pl.pallas_call(
        functools.partial(_dsa_body, tq=tq, tqa=tqa, tk=tk, top_k=top_k),
        grid=(b, s // tq),
        in_specs=[qspec(w), qspec(iq.shape[2]), pl.BlockSpec((1, iw.shape[1], tq), lambda bi, qi: (bi, 0, qi)),
                  kspec(w), pl.BlockSpec((1, w, s), lambda bi, qi: (bi, 0, 0)), kspec(LANES)],
        out_specs=qspec(w),
        out_shape=jax.ShapeDtypeStruct((b, s, w), BF16),
        scratch_shapes=[pltpu.VMEM((s, tq), F32)] + [pltpu.VMEM((s, tq), BF16)] * 4
        + _attend_scratch(w // HEAD_DIM, tqa, tk, HEAD_DIM),
        compiler_params=_cparams(("arbitrary", "arbitrary")),
        name="dsa_attention",
    )(dq, iq, iw, dk, dvt, ik)


MLA_DOWN_COLS = MLA_Q_LORA + MLA_KV_LORA + 2 * MLA_ROPE


def _mla_proj_body(h_ref, sh_ref, sc_ref, g_ref, wd_ref, qn_ref, kvn_ref, wuq_ref, wukv_ref, cos_ref, sin_ref,
                   qnope_ref, qrope_ref, knope_ref, v_ref, krope_ref, *, scale):
    x = h_ref[...]
    u = _modulated_norm(x, g_ref[...], sc_ref[0], sh_ref[0]).astype(BF16)
    cos = cos_ref[...]
    sin = sin_ref[...]
    half = MLA_ROPE // 2
    down = jnp.dot(u, wd_ref[...], preferred_element_type=F32)

    def norm(z, g):
        ms = jnp.mean(z * z, axis=-1, keepdims=True)
        return (z * lax.rsqrt(ms + RMS_EPS) * g).astype(BF16)

    cq = norm(down[:, 0:MLA_Q_LORA], qn_ref[...])
    ckv = norm(down[:, MLA_Q_LORA:MLA_Q_LORA + MLA_KV_LORA], kvn_ref[...])
    krope_ref[0] = _rope_slab(down[:, MLA_Q_LORA + MLA_KV_LORA:], cos, sin, half).astype(BF16)

    n_nope = MLA_HEADS * MLA_NOPE
    qn = jnp.dot(cq, wuq_ref[:, 0:n_nope], preferred_element_type=F32)
    qnope_ref[0] = (qn * scale).astype(BF16)
    qr = jnp.dot(cq, wuq_ref[:, n_nope:], preferred_element_type=F32)
    qr = jnp.concatenate(
        [_rope_slab(qr[:, j * LANES:(j + 1) * LANES], cos, sin, half) for j in range(qr.shape[1] // LANES)], axis=1)
    qrope_ref[0] = (qr * scale).astype(BF16)
    knope_ref[0] = jnp.dot(ckv, wukv_ref[:, 0:n_nope], preferred_element_type=F32).astype(BF16)
    v_ref[0] = jnp.dot(ckv, wukv_ref[:, n_nope:], preferred_element_type=F32).T.astype(BF16)


def _mla_proj(h, sh, sc, g, wd, qn, kvn, wuq, wukv, cos, sin, b, s):
    t, d = h.shape
    tpb = s // TM
    row = pl.BlockSpec((TM, d), lambda i: (i, 0))
    mod = pl.BlockSpec((1, 1, d), lambda i: (i // tpb, 0, 0))
    tab = pl.BlockSpec((TM, LANES), lambda i: (i, 0))

    def out(width):
        return pl.BlockSpec((1, TM, width), lambda i: (i // tpb, i % tpb, 0))

    def shp(width):
        return jax.ShapeDtypeStruct((b, s, width), BF16)

    n_nope = MLA_HEADS * MLA_NOPE
    n_rope = MLA_HEADS * MLA_ROPE
    n_v = MLA_HEADS * MLA_V
    scale = (MLA_NOPE + MLA_ROPE) ** -0.5 * LOG2E
    return pl.pallas_call(
        functools.partial(_mla_proj_body, scale=scale),
        grid=(t // TM,),
        in_specs=[row, mod, mod, _const_spec((1, d)), _const_spec((d, MLA_DOWN_COLS)),
                  _const_spec((1, MLA_Q_LORA)), _const_spec((1, MLA_KV_LORA)),
                  _const_spec((MLA_Q_LORA, n_nope + n_rope)), _const_spec((MLA_KV_LORA, n_nope + n_v)), tab, tab],
        out_specs=[out(n_nope), out(n_rope), out(n_nope),
                   pl.BlockSpec((1, n_v, TM), lambda i: (i // tpb, 0, i % tpb)), out(LANES)],
        out_shape=[shp(n_nope), shp(n_rope), shp(n_nope), jax.ShapeDtypeStruct((b, n_v, s), BF16), shp(LANES)],
        compiler_params=_cparams(("arbitrary",)),
        name="mla_in_proj",
    )(h, sh, sc, g, wd, qn, kvn, wuq, wukv, cos, sin)


def _mla_body(qn_ref, qr_ref, kn_ref, kr_ref, vt_ref, o_ref, *scratch, tq, tk):
    n_heads = qn_ref.shape[2] // MLA_NOPE
    qi = pl.program_id(2)
    qs = []
    for hd in range(n_heads):
        qr = qr_ref[0, :, (hd // 2) * LANES:(hd // 2 + 1) * LANES]
        qs.append(jnp.concatenate([qn_ref[0, :, hd * LANES:(hd + 1) * LANES],
                                   jnp.where(_head_mask(hd % 2), qr, jnp.zeros_like(qr))], axis=1))
    krow = lax.broadcasted_iota(jnp.int32, (tk, tq), 0)
    qcol = lax.broadcasted_iota(jnp.int32, (tk, tq), 1)

    def qk(hd, kt):
        ks = pl.multiple_of(kt * tk, tk)
        k = jnp.concatenate([kn_ref[0, pl.ds(ks, tk), hd * LANES:(hd + 1) * LANES], kr_ref[0, pl.ds(ks, tk), :]],
                            axis=1)
        return _qk(k, qs[hd])

    def values(hd, kt):
        return vt_ref[0, hd * LANES:(hd + 1) * LANES, pl.ds(pl.multiple_of(kt * tk, tk), tk)]

    def chunk_causal(s):
        return jnp.where(krow // CHUNK <= qcol // CHUNK, s, NEG_LOGIT)

    outs = _attend(n_heads, qi + 1, qk, values, scratch, mask_last=chunk_causal)
    for hd in range(n_heads):
        o_ref[0, :, hd * LANES:(hd + 1) * LANES] = outs[hd].T.astype(BF16)


def _mla_attention(qnope, qrope, knope, krope, vt):
    b, s, w = qnope.shape
    tq = tk = ATT_TQ
    gw = ATT_HEADS * MLA_NOPE
    gr = ATT_HEADS * MLA_ROPE
    qspec = pl.BlockSpec((1, tq, gw), lambda bi, hg, qi: (bi, qi, hg))
    qrspec = pl.BlockSpec((1, tq, gr), lambda bi, hg, qi: (bi, qi, hg))
    once = pl.Buffered(1)
    kspec = pl.BlockSpec((1, s, gw), lambda bi, hg, qi: (bi, 0, hg), pipeline_mode=once)
    krspec = pl.BlockSpec((1, s, LANES), lambda bi, hg, qi: (bi, 0, 0), pipeline_mode=once)
    vspec = pl.BlockSpec((1, gw, s), lambda bi, hg, qi: (bi, hg, 0), pipeline_mode=once)
    return pl.pallas_call(
        functools.partial(_mla_body, tq=tq, tk=tk),
        grid=(b, w // gw, s // tq),
        in_specs=[qspec, qrspec, kspec, krspec, vspec],
        out_specs=qspec,
        out_shape=jax.ShapeDtypeStruct((b, s, w), BF16),
        scratch_shapes=_attend_scratch(ATT_HEADS, tq, tk, LANES),
        compiler_params=_cparams(("arbitrary", "arbitrary", "arbitrary")),
        name="mla_attention",
    )(qnope, qrope, knope, krope, vt)


def _hyb_weight(w_in, b_f):
    o = 0
    parts = {}
    for name, width in (("fq", 512), ("fk", 512), ("fv", 512), ("ff", FOX_HEADS), ("dq", 512), ("dk", 512),
                        ("dv", 512), ("iq", 256), ("iw", IDX_HEADS), ("ik", HEAD_DIM)):
        parts[name] = w_in[:, o:o + width]
        o += width
    qs = HEAD_DIM ** -0.5
    pad = jnp.zeros((w_in.shape[0], LANES - FOX_HEADS - IDX_HEADS), w_in.dtype)
    w = jnp.concatenate([parts["fq"] * qs, parts["fk"], parts["fv"], parts["dq"] * qs, parts["dk"], parts["dv"],
                         parts["iq"] * qs, parts["ik"], parts["ik"],
                         parts["ff"], parts["iw"] * IDX_HEADS ** -0.5, pad], axis=1)
    bf = jnp.concatenate([b_f, jnp.zeros((LANES - FOX_HEADS,), b_f.dtype)]).reshape(1, LANES)
    return w.astype(BF16), bf.astype(F32)


def _mla_weights(w_down, w_uq, w_ukv):
    kr = w_down[:, MLA_Q_LORA + MLA_KV_LORA:]
    wd = jnp.concatenate([w_down, kr], axis=1)
    uq = w_uq.reshape(MLA_Q_LORA, MLA_HEADS, MLA_NOPE + MLA_ROPE)
    wuq = jnp.concatenate([uq[:, :, :MLA_NOPE].reshape(MLA_Q_LORA, -1), uq[:, :, MLA_NOPE:].reshape(MLA_Q_LORA, -1)],
                          axis=1)
    ukv = w_ukv.reshape(MLA_KV_LORA, MLA_HEADS, MLA_NOPE + MLA_V)
    wukv = jnp.concatenate([ukv[:, :, :MLA_NOPE].reshape(MLA_KV_LORA, -1),
                            ukv[:, :, MLA_NOPE:].reshape(MLA_KV_LORA, -1)], axis=1)
    return wd.astype(BF16), wuq.astype(BF16), wukv.astype(BF16)


def kernel(x, c, positions, ada_w, ada_b, norm_g, ffn_w_gate, ffn_w_up, ffn_w_down, hyb_w_in, fox_b_f, hyb_w_out,
           mla_w_down, mla_q_norm, mla_kv_norm, mla_w_uq, mla_w_ukv, mla_w_out, final_g):
    b, s, d = x.shape
    depth = ada_w.shape[0]
    t = b * s
    assert s % TM == 0 and s % FFN_TM == 0 and s % ATT_TQ == 0 and s % DSA_TQ == 0 and s % DSA_TK == 0
    assert DSA_TQ % DSA_ATT_TQ == 0 and ffn_w_gate.shape[-1] % FFN_CHUNK == 0

    mod = _modulation(c, ada_w, ada_b).reshape(depth, b, N_MOD, 1, d)
    cos_d, sin_d, cos_m, sin_m = _rope_tables(positions)
    wg = ffn_w_gate.astype(BF16)
    wu = ffn_w_up.astype(BF16)
    wdn = ffn_w_down.astype(BF16)

    h = x.reshape(t, d)
    for i in range(depth):
        sh1, sc1, g1, sh2, sc2, g2, sh3, sc3, g3 = [mod[i, :, j] for j in range(N_MOD)]
        ng = norm_g[i].reshape(3, 1, d)
        h = _ffn(h, sh1, sc1, g1, ng[0], wg[i, 0], wu[i, 0], wdn[i, 0], s)
        j = i // 2
        if i % 2 == 0:
            w_in, bf = _hyb_weight(hyb_w_in[j], fox_b_f[j])
            fq, fk, fvt, dq, dk, dvt, iq, ik, cparts, iw = _hyb_proj(h, sh2, sc2, ng[1], w_in, bf, cos_d, sin_d, b, s)
            out_a = _fox_attention(fq, fk, cparts, fvt)
            out_b = _dsa_attention(dq, dk, dvt, iq, ik, iw)
            w_out = hyb_w_out[j].astype(BF16)
            half = out_a.shape[2]
            mix = (g2, [out_a.reshape(t, half), out_b.reshape(t, half)], [w_out[:half], w_out[half:]])
        else:
            wd, wuq, wukv = _mla_weights(mla_w_down[j], mla_w_uq[j], mla_w_ukv[j])
            qn, qr, kn, vt, kr = _mla_proj(h, sh2, sc2, ng[1], wd, mla_q_norm[j].reshape(1, -1),
                                           mla_kv_norm[j].reshape(1, -1), wuq, wukv, cos_m, sin_m, b, s)
            out = _mla_attention(qn, qr, kn, kr, vt)
            mix = (g2, [out.reshape(t, -1)], [mla_w_out[j].astype(BF16)])
        last = i == depth - 1
        h = _ffn(h, sh3, sc3, g3, ng[2], wg[i, 1], wu[i, 1], wdn[i, 1], s,
                 final_g=final_g.reshape(1, d) if last else None, mix=mix)
    return h.reshape(b, s, d)
```

```python
import functools
import math

import jax
import jax.numpy as jnp
from jax import lax
from jax.experimental import pallas as pl
from jax.experimental.pallas import tpu as pltpu

F32 = jnp.float32
BF16 = jnp.bfloat16

CHUNK = 64
RMS_EPS = 1e-6
ROPE_THETA = 500000.0
MLA_ROPE_THETA = 10000.0
MACARON_WEIGHT = 0.5
N_MOD = 9
FOX_HEADS = 8
HEAD_DIM = 64
DSA_ROT = 16
IDX_HEADS = 4
DSA_TOPK_MAX = 256
MLA_HEADS = 8
MLA_NOPE = 128
MLA_ROPE = 64
MLA_V = 128
MLA_Q_LORA = 384
MLA_KV_LORA = 256

LANES = 128
VMEM_LIMIT = 56 * 1024 * 1024

LOG2E = math.log2(math.e)
NEG_LOGIT = -1e30
NEG_SCORE = -3e38

TM = 1024
FFN_TM = 512
FFN_CHUNK = 256
ATT_TQ = 512
ATT_HEADS = 8
FOX_STEP_HEADS = 8
DSA_TQ = 256
DSA_TK = 512
COUNT_ROWS = 32


def _cparams(sem):
    return pltpu.CompilerParams(dimension_semantics=sem, vmem_limit_bytes=VMEM_LIMIT)


def _const_spec(shape):
    nd = len(shape)
    return pl.BlockSpec(shape, lambda *_: (0,) * nd, pipeline_mode=pl.Buffered(1))


def _silu(x):
    return x * jax.nn.sigmoid(x)


def _modulated_norm(x, g, sc, sh):
    ms = jnp.mean(x * x, axis=-1, keepdims=True)
    y = x * lax.rsqrt(ms + RMS_EPS) * g
    return y * (1.0 + sc) + sh


def _mod_body(c_ref, w_ref, b_ref, o_ref):
    cond = _silu(c_ref[...]).astype(BF16)
    o_ref[0] = jnp.dot(cond, w_ref[0].astype(BF16), preferred_element_type=F32) + b_ref[0]


def _modulation(c, ada_w, ada_b):
    depth, d, n = ada_w.shape
    b = c.shape[0]
    tn = n // N_MOD
    return pl.pallas_call(
        _mod_body,
        grid=(depth, n // tn),
        in_specs=[pl.BlockSpec((b, d), lambda i, j: (0, 0)),
                  pl.BlockSpec((1, d, tn), lambda i, j: (i, 0, j)),
                  pl.BlockSpec((1, 1, tn), lambda i, j: (i, 0, j))],
        out_specs=pl.BlockSpec((1, b, tn), lambda i, j: (i, 0, j)),
        out_shape=jax.ShapeDtypeStruct((depth, b, n), F32),
        compiler_params=_cparams(("arbitrary", "arbitrary")),
        name="adaln_mod",
    )(c, ada_w, ada_b.reshape(depth, 1, n))


def _rope_table_body(pos_ref, invd_ref, sgnd_ref, invm_ref, sgnm_ref, cd_ref, sd_ref, cm_ref, sm_ref):
    pos = pos_ref[...].astype(F32)
    for inv_ref, sgn_ref, c_ref, s_ref in ((invd_ref, sgnd_ref, cd_ref, sd_ref),
                                           (invm_ref, sgnm_ref, cm_ref, sm_ref)):
        ang = pos * inv_ref[...]
        sgn = sgn_ref[...]
        c_ref[...] = jnp.where(sgn != 0.0, jnp.cos(ang), 1.0)
        s_ref[...] = sgn * jnp.sin(ang)


def _lane_pattern(rot, theta):
    half = rot // 2
    inv_freq = jnp.exp(-math.log(theta) * 2.0 * jnp.arange(half, dtype=F32) / rot)
    d = jnp.arange(LANES) % HEAD_DIM
    inv = jnp.where(d < rot, inv_freq[d % half], 0.0).astype(F32)
    sgn = jnp.where(d < half, -1.0, jnp.where(d < rot, 1.0, 0.0)).astype(F32)
    return inv.reshape(1, LANES), sgn.reshape(1, LANES)


def _rope_tables(positions):
    t = positions.size
    invd, sgnd = _lane_pattern(DSA_ROT, ROPE_THETA)
    invm, sgnm = _lane_pattern(MLA_ROPE, MLA_ROPE_THETA)
    tm = 1024
    row = pl.BlockSpec((tm, LANES), lambda i: (i, 0))
    vec = pl.BlockSpec((1, LANES), lambda i: (0, 0))
    tab = jax.ShapeDtypeStruct((t, LANES), F32)
    return pl.pallas_call(
        _rope_table_body,
        grid=(t // tm,),
        in_specs=[pl.BlockSpec((tm, 1), lambda i: (i, 0)), vec, vec, vec, vec],
        out_specs=[row, row, row, row],
        out_shape=[tab, tab, tab, tab],
        compiler_params=_cparams(("arbitrary",)),
        name="rope_tables",
    )(positions.reshape(t, 1), invd, sgnd, invm, sgnm)


def _rope_slab(y, cos, sin, half):
    lane = lax.broadcasted_iota(jnp.int32, (1, LANES), 1) % HEAD_DIM
    first = lane < half
    partner = jnp.where(first, pltpu.roll(y, LANES - half, 1), pltpu.roll(y, half, 1))
    return y * cos + partner * sin


def _ffn_body(h_ref, sh_ref, sc_ref, gt_ref, g_ref, wg_ref, wu_ref, wd_ref, *rest, nf, fc, final, n_mix):
    rest = list(rest)
    x = h_ref[...]
    if n_mix:
        mg_ref = rest.pop(0)
        mix = None
        for x_ref, w_ref in zip(rest[:n_mix], rest[n_mix:2 * n_mix]):
            part = jnp.dot(x_ref[...], w_ref[...], preferred_element_type=F32)
            mix = part if mix is None else mix + part
        x = x + mg_ref[0] * mix
        rest = rest[2 * n_mix:]
    if final:
        fg_ref, o_ref, acc_ref = rest
    else:
        o_ref, acc_ref = rest
    u = _modulated_norm(x, g_ref[...], sc_ref[0], sh_ref[0]).astype(BF16)
    for f in range(nf):
        sl = slice(f * fc, (f + 1) * fc)
        gp = jnp.dot(u, wg_ref[:, sl], preferred_element_type=F32)
        up = jnp.dot(u, wu_ref[:, sl], preferred_element_type=F32)
        a = (_silu(gp) * up).astype(BF16)
        d = jnp.dot(a, wd_ref[sl, :], preferred_element_type=F32)
        if f == 0:
            acc_ref[...] = d
        else:
            acc_ref[...] += d
    y = x + (MACARON_WEIGHT * gt_ref[0]) * acc_ref[...]
    if final:
        ms = jnp.mean(y * y, axis=-1, keepdims=True)
        y = y * lax.rsqrt(ms + RMS_EPS) * fg_ref[...]
    o_ref[...] = y


def _ffn(h, sh, sc, gt, g, wg, wu, wd, which, s, final_g=None, mix=None):
    t, d = h.shape
    f = wg.shape[-1]
    tm = FFN_TM
    tpb = s // tm
    row = pl.BlockSpec((tm, d), lambda i: (i, 0))
    mod = pl.BlockSpec((1, 1, d), lambda i: (i // tpb, 0, 0))

    def weight(rows, cols):
        return pl.BlockSpec((None, None, rows, cols), lambda i: (*which, 0, 0), pipeline_mode=pl.Buffered(1))

    in_specs = [row, mod, mod, mod, _const_spec((1, d)), weight(d, f), weight(d, f), weight(f, d)]
    args = [h, sh, sc, gt, g, wg, wu, wd]
    n_mix = 0
    if mix is not None:
        mgate, xs, ws = mix
        n_mix = len(xs)
        in_specs += [mod] + [pl.BlockSpec((tm, x.shape[1]), lambda i: (i, 0)) for x in xs] + [_const_spec(w.shape) for w in ws]
        args += [mgate, *xs, *ws]
    if final_g is not None:
        in_specs.append(_const_spec((1, d)))
        args.append(final_g)
    return pl.pallas_call(
        functools.partial(_ffn_body, nf=f // FFN_CHUNK, fc=FFN_CHUNK, final=final_g is not None, n_mix=n_mix),
        grid=(t // tm,),
        in_specs=in_specs,
        out_specs=row,
        out_shape=jax.ShapeDtypeStruct((t, d), F32),
        scratch_shapes=[pltpu.VMEM((tm, d), F32)],
        compiler_params=_cparams(("arbitrary",)),
        name="swiglu_half_step",
    )(*args)


HYB_W = 512
HYB_COLS = 6 * HYB_W + 256 + 128 + 128


def _hyb_proj_body(h_ref, sh_ref, sc_ref, g_ref, w_ref, bf_ref, cos_ref, sin_ref,
                   fq_ref, fk_ref, fvt_ref, dq_ref, dk_ref, dvt_ref, iq_ref, ik_ref, cum_ref, iw_ref,
                   carry_ref, *, tpb):
    i = pl.program_id(0)
    x = h_ref[...]
    tm = x.shape[0]
    u = _modulated_norm(x, g_ref[...], sc_ref[0], sh_ref[0]).astype(BF16)
    cos = cos_ref[...]
    sin = sin_ref[...]
    half = DSA_ROT // 2

    def proj(c0, width):
        return jnp.dot(u, w_ref[:, c0:c0 + width], preferred_element_type=F32)

    def roped(y):
        return jnp.concatenate(
            [_rope_slab(y[:, j * LANES:(j + 1) * LANES], cos, sin, half) for j in range(y.shape[1] // LANES)],
            axis=1)

    fq_ref[0] = (proj(0 * HYB_W, HYB_W) * LOG2E).astype(BF16)
    fk_ref[0] = proj(1 * HYB_W, HYB_W).astype(BF16)
    fvt_ref[0] = proj(2 * HYB_W, HYB_W).T.astype(BF16)
    dq_ref[0] = (roped(proj(3 * HYB_W, HYB_W)) * LOG2E).astype(BF16)
    dk_ref[0] = roped(proj(4 * HYB_W, HYB_W)).astype(BF16)
    dvt_ref[0] = proj(5 * HYB_W, HYB_W).T.astype(BF16)
    tail = proj(6 * HYB_W, 512)
    iq_ref[0] = roped(tail[:, 0:256]).astype(BF16)
    ik_ref[0] = roped(tail[:, 256:384]).astype(BF16)
    gates = tail[:, 384:512]
    iw_ref[0] = gates.T[FOX_HEADS:2 * FOX_HEADS, :]

    z = gates + bf_ref[...]
    logf = jnp.minimum(z, 0.0) - jnp.log1p(jnp.exp(-jnp.abs(z)))
    rows = lax.broadcasted_iota(jnp.int32, (tm, LANES), 0)
    c = logf
    k = 1
    while k < tm:
        c = c + jnp.where(rows >= k, pltpu.roll(c, k, 0), 0.0)
        k *= 2

    @pl.when(i % tpb == 0)
    def _():
        carry_ref[...] = jnp.zeros_like(carry_ref)

    c = c + carry_ref[0:1, :]
    carry_ref[...] = jnp.broadcast_to(c[tm - 1:tm, :], carry_ref.shape)
    lane = lax.broadcasted_iota(jnp.int32, (1, LANES), 1)
    c = jnp.where(lane < FOX_HEADS, c * LOG2E, 0.0)
    hi = c.astype(BF16).astype(F32)
    mid = (c - hi).astype(BF16).astype(F32)
    lo = (c - hi - mid).astype(BF16).astype(F32)
    cum_ref[0] = (hi + pltpu.roll(mid, FOX_HEADS, 1) + pltpu.roll(lo, 2 * FOX_HEADS, 1)).astype(BF16)


def _hyb_proj(h, sh, sc, g, w, bf, cos, sin, b, s):
    t, d = h.shape
    tpb = s // TM
    row = pl.BlockSpec((TM, d), lambda i: (i, 0))
    mod = pl.BlockSpec((1, 1, d), lambda i: (i // tpb, 0, 0))
    tab = pl.BlockSpec((TM, LANES), lambda i: (i, 0))

    def out(width):
        return pl.BlockSpec((1, TM, width), lambda i: (i // tpb, i % tpb, 0))

    def shp(width, dt=BF16):
        return jax.ShapeDtypeStruct((b, s, width), dt)

    def out_t(width):
        return pl.BlockSpec((1, width, TM), lambda i: (i // tpb, 0, i % tpb))

    def shp_t(width, dt=BF16):
        return jax.ShapeDtypeStruct((b, width, s), dt)

    return pl.pallas_call(
        functools.partial(_hyb_proj_body, tpb=tpb),
        grid=(t // TM,),
        in_specs=[row, mod, mod, _const_spec((1, d)), _const_spec((d, HYB_COLS)), _const_spec((1, LANES)), tab, tab],
        out_specs=[out(HYB_W), out(HYB_W), out_t(HYB_W), out(HYB_W), out(HYB_W), out_t(HYB_W),
                   out(256), out(LANES), out(LANES), out_t(FOX_HEADS)],
        out_shape=[shp(HYB_W), shp(HYB_W), shp_t(HYB_W), shp(HYB_W), shp(HYB_W), shp_t(HYB_W),
                   shp(256), shp(LANES), shp(LANES), shp_t(FOX_HEADS, F32)],
        scratch_shapes=[pltpu.VMEM((8, LANES), F32)],
        compiler_params=_cparams(("arbitrary",)),
        name="hybrid_in_proj",
    )(h, sh, sc, g, w, bf, cos, sin)


def _attend_scratch(n_heads, tq, tk, n):
    return [pltpu.VMEM((2, n_heads, tk, tq), F32), pltpu.VMEM((2, n_heads, 1, tq), F32),
            pltpu.VMEM((n_heads, 1, tq), F32), pltpu.VMEM((n_heads, 1, tq), F32), pltpu.VMEM((n_heads, n, tq), F32)]


def _attend(n_heads, n_tiles, qk, values, scratch, mask_last=None):
    s_ref, smax_ref, m_ref, l_ref, acc_ref = scratch

    def stage(slot, kt):
        for hd in range(n_heads):
            s = qk(hd, kt)
            s_ref[slot, hd] = s
            smax_ref[slot, hd] = jnp.max(s, axis=0, keepdims=True)

    def step(slot, kt, mask=None):
        for hd in range(n_heads):
            s = s_ref[slot, hd]
            smax = smax_ref[slot, hd]
            if mask is not None:
                s = mask(s)
                smax = jnp.max(s, axis=0, keepdims=True)
            m = m_ref[hd]
            m_new = jnp.maximum(m, smax)
            alpha = jnp.exp2(m - m_new)
            p = jnp.exp2(s - m_new)
            m_ref[hd] = m_new
            l_ref[hd] = alpha * l_ref[hd] + jnp.sum(p, axis=0, keepdims=True)
            acc_ref[hd] = alpha * acc_ref[hd] + jnp.dot(values(hd, kt), p.astype(BF16), preferred_element_type=F32)

    m_ref[...] = jnp.full(m_ref.shape, NEG_LOGIT, F32)
    l_ref[...] = jnp.zeros(l_ref.shape, F32)
    acc_ref[...] = jnp.zeros(acc_ref.shape, F32)
    stage(0, 0)
    n_pairs = (n_tiles - 1) // 2

    def pair(i, _):
        stage(1, 2 * i + 1)
        step(0, 2 * i)
        stage(0, 2 * i + 2)
        step(1, 2 * i + 1)
        return 0

    lax.fori_loop(0, n_pairs, pair, 0)
    odd_tail = n_tiles - 2 * n_pairs == 2

    @pl.when(odd_tail)
    def _():
        stage(1, n_tiles - 1)
        step(0, n_tiles - 2)

    step(jnp.where(odd_tail, 1, 0), n_tiles - 1, mask=mask_last)
    return [acc_ref[hd] / l_ref[hd] for hd in range(n_heads)]


def _head_mask(j):
    lane = lax.broadcasted_iota(jnp.int32, (1, LANES), 1)
    return (lane // HEAD_DIM) == j


def _qk(q, k):
    return lax.dot_general(q, k, (((1,), (1,)), ((), ())), preferred_element_type=F32)


def _pair_out(outs):
    return jnp.concatenate(outs, axis=0).T.astype(BF16)


def _fox_body(q_ref, k_ref, c_ref, vt_ref, o_ref, *scratch, tq, tk):
    n_heads = q_ref.shape[2] // HEAD_DIM
    hg = pl.program_id(1)
    qi = pl.program_id(2)
    krow = lax.broadcasted_iota(jnp.int32, (tk, tq), 0)
    qcol = lax.broadcasted_iota(jnp.int32, (tk, tq), 1)
    lane = lax.broadcasted_iota(jnp.int32, (tq, LANES), 1)
    qs = []
    for hd in range(n_heads):
        pick = (lane % FOX_HEADS == hg * n_heads + hd) & (lane < 3 * FOX_HEADS)
        sel = jnp.where(pick, -1.0, 0.0).astype(BF16)
        qfull = q_ref[0, :, (hd // 2) * LANES:(hd // 2 + 1) * LANES]
        qs.append(jnp.concatenate([jnp.where(_head_mask(hd % 2), qfull, jnp.zeros_like(qfull)), sel], axis=1))

    def qk(hd, kt):
        ks = pl.multiple_of(kt * tk, tk)
        lanes = slice((hd // 2) * LANES, (hd // 2 + 1) * LANES)
        k = jnp.concatenate([k_ref[0, pl.ds(ks, tk), lanes], c_ref[0, pl.ds(ks, tk), :]], axis=1)
        return _qk(k, qs[hd])

    def values(hd, kt):
        return vt_ref[0, hd * HEAD_DIM:(hd + 1) * HEAD_DIM, pl.ds(pl.multiple_of(kt * tk, tk), tk)]

    def causal(s):
        return jnp.where(krow <= qcol, s, NEG_LOGIT)

    outs = _attend(n_heads, qi + 1, qk, values, scratch, mask_last=causal)
    for hp in range(n_heads // 2):
        o_ref[0, :, hp * LANES:(hp + 1) * LANES] = _pair_out(outs[2 * hp:2 * hp + 2])


def _fox_attention(fq, fk, cparts, fvt):
    b, s, w = fq.shape
    tq = tk = ATT_TQ
    gw = FOX_STEP_HEADS * HEAD_DIM
    qspec = pl.BlockSpec((1, tq, gw), lambda bi, hg, qi: (bi, qi, hg))
    kspec = pl.BlockSpec((1, s, gw), lambda bi, hg, qi: (bi, 0, hg))
    cspec = pl.BlockSpec((1, s, LANES), lambda bi, hg, qi: (bi, 0, 0))
    vspec = pl.BlockSpec((1, gw, s), lambda bi, hg, qi: (bi, hg, 0))
    return pl.pallas_call(
        functools.partial(_fox_body, tq=tq, tk=tk),
        grid=(b, w // gw, s // tq),
        in_specs=[qspec, kspec, cspec, vspec],
        out_specs=qspec,
        out_shape=jax.ShapeDtypeStruct((b, s, w), BF16),
        scratch_shapes=_attend_scratch(FOX_STEP_HEADS, tq, tk, HEAD_DIM),
        compiler_params=_cparams(("arbitrary", "arbitrary", "arbitrary")),
        name="fox_attention",
    )(fq, fk, cparts, fvt)


def _hi_key_to_bits(h):
    return jnp.where(h >= 0x8000, h - 0x8000, (~h) & 0xFFFF)


def _tree_sum(parts):
    while len(parts) > 1:
        parts = [a + b for a, b in zip(parts[0::2], parts[1::2])] + ([parts[-1]] if len(parts) % 2 else [])
    return parts[0]


def _dsa_body(dq_ref, iq_ref, iw_ref, dk_ref, dvt_ref, ik_ref, o_ref, sc_ref, a1_ref, d1_ref, d0_ref, a2_ref,
              *scratch, tq, tk, top_k):
    qi = pl.program_id(1)
    n_tiles = ((qi + 1) * tq + tk - 1) // tk
    qpos = lax.broadcasted_iota(jnp.int32, (1, tq), 1) + qi * tq
    limit = (qpos // CHUNK + 1) * CHUNK
    krow = lax.broadcasted_iota(jnp.int32, (tk, tq), 0)
    kf = jnp.float32(top_k)

    iqf = iq_ref[0]
    iw = iw_ref[0]
    iqs = [jnp.where(_head_mask(h % 2), iqf[:, (h // 2) * LANES:(h // 2 + 1) * LANES], jnp.zeros((tq, LANES), BF16))
           for h in range(IDX_HEADS)]
    ws = [iw[h:h + 1, :] for h in range(IDX_HEADS)]

    def score_tile(kt, _):
        ks = pl.multiple_of(kt * tk, tk)
        ik = ik_ref[0, pl.ds(ks, tk), :]
        sc = None
        for h in range(IDX_HEADS):
            term = jnp.maximum(_qk(ik, iqs[h]), 0.0) * ws[h]
            sc = term if sc is None else sc + term
        sc = jnp.where(krow + ks < limit, sc, NEG_SCORE)
        sc = jnp.where(sc == 0.0, 0.0, sc)
        sc_ref[pl.ds(ks, tk), :] = sc
        bits = pltpu.bitcast(sc, jnp.int32)
        a1_ref[pl.ds(ks, tk), :] = pltpu.bitcast(bits & jnp.int32(-65536), F32).astype(BF16)
        lo = bits & 0xFFFF
        klo = jnp.where(bits < 0, 0xFFFF - lo, lo)
        d1_ref[pl.ds(ks, tk), :] = lax.shift_right_logical(klo, 8).astype(F32).astype(BF16)
        d0_ref[pl.ds(ks, tk), :] = (klo & 0xFF).astype(F32).astype(BF16)
        return 0

    lax.fori_loop(0, n_tiles, score_tile, 0)

    def tile_count(mask):
        c = jnp.where(mask, jnp.ones((), BF16), jnp.zeros((), BF16))
        return _tree_sum([c[i * COUNT_ROWS:(i + 1) * COUNT_ROWS, :] for i in range(tk // COUNT_ROWS)]).astype(F32)

    zero_cnt = jnp.zeros((COUNT_ROWS, tq), F32)

    def count_packed(ref, op, ref_b):
        def body(kt, cnt):
            return cnt + tile_count(op(ref[pl.ds(pl.multiple_of(kt * tk, tk), tk), :], ref_b))
        return jnp.sum(lax.fori_loop(0, n_tiles, body, zero_cnt), axis=0, keepdims=True)

    def hi_float(h):
        return pltpu.bitcast(lax.shift_left(_hi_key_to_bits(h), 16), F32).astype(BF16)

    def hi_step(it, h):
        cand = h | lax.shift_left(jnp.int32(1), 15 - it)
        cnt = count_packed(a1_ref, lambda x, c: x >= c, hi_float(cand))
        return jnp.where(cnt >= kf, cand, h)

    h = lax.fori_loop(0, 16, hi_step, jnp.zeros((1, tq), jnp.int32))
    hbits = _hi_key_to_bits(h)
    hbits = jnp.where((hbits & 0x7F80) == 0, hbits & 0x8000, hbits)
    thr_hi = pltpu.bitcast(lax.shift_left(hbits, 16), F32).astype(BF16)

    def restrict(src_ref, match, digit_ref):
        def body(kt, cnt):
            rows = pl.ds(pl.multiple_of(kt * tk, tk), tk)
            x = src_ref[rows, :]
            a2_ref[rows, :] = jnp.where(x == match, digit_ref[rows, :], -jnp.ones((), BF16))
            return cnt + tile_count(x > match)
        return jnp.sum(lax.fori_loop(0, n_tiles, body, zero_cnt), axis=0, keepdims=True)

    def digit(need):
        def step(it, v):
            cand = v + lax.shift_left(jnp.int32(1), 7 - it).astype(F32)
            cnt = count_packed(a2_ref, lambda x, c: x >= c, cand.astype(BF16))
            return jnp.where(cnt >= need, cand, v)
        return lax.fori_loop(0, 8, step, jnp.zeros((1, tq), F32))

    need_hi = kf - restrict(a1_ref, thr_hi, d1_ref)
    v1 = digit(need_hi)
    need_lo = need_hi - restrict(a2_ref, v1.astype(BF16), d0_ref)
    v0 = digit(need_lo)
    klo = (v1 * 256.0 + v0).astype(jnp.int32)
    lo = jnp.where(hbits >= 0x8000, 0xFFFF - klo, klo)
    thr = pltpu.bitcast(lax.shift_left(hbits, 16) | lo, F32)

    v0b = v0.astype(BF16)

    need = need_lo - count_packed(a2_ref, lambda x, c: x > c, v0b)
    need = jnp.where(thr <= NEG_SCORE, 0.0, need)

    lower = jnp.where(lax.broadcasted_iota(jnp.int32, (tk, tk), 0) >= lax.broadcasted_iota(jnp.int32, (tk, tk), 1),
                      1.0, 0.0).astype(BF16)

    def bias_tile(kt, seen):
        rows = pl.ds(pl.multiple_of(kt * tk, tk), tk)
        sc = sc_ref[rows, :]
        tied = jnp.where(a2_ref[rows, :] == v0b, jnp.ones((), BF16), jnp.zeros((), BF16))
        rank = jnp.dot(lower, tied, preferred_element_type=F32) + seen
        tie_bias = jnp.where(rank <= need, 0.0, NEG_LOGIT)
        sc_ref[rows, :] = jnp.where(sc > thr, 0.0, jnp.where(sc == thr, tie_bias, NEG_LOGIT))
        return rank[tk - 1:tk, :]

    lax.fori_loop(0, n_tiles, bias_tile, jnp.zeros((1, tq), F32))

    n_heads = dq_ref.shape[2] // HEAD_DIM
    qs = []
    for hd in range(n_heads):
        qfull = dq_ref[0, :, (hd // 2) * LANES:(hd // 2 + 1) * LANES]
        qs.append(jnp.where(_head_mask(hd % 2), qfull, jnp.zeros_like(qfull)))

    def qk(hd, kt):
        ks = pl.multiple_of(kt * tk, tk)
        lanes = slice((hd // 2) * LANES, (hd // 2 + 1) * LANES)
        return _qk(dk_ref[0, pl.ds(ks, tk), lanes], qs[hd]) + sc_ref[pl.ds(ks, tk), :]

    def values(hd, kt):
        return dvt_ref[0, hd * HEAD_DIM:(hd + 1) * HEAD_DIM, pl.ds(pl.multiple_of(kt * tk, tk), tk)]

    outs = _attend(n_heads, n_tiles, qk, values, scratch)
    for hp in range(n_heads // 2):
        o_ref[0, :, hp * LANES:(hp + 1) * LANES] = _pair_out(outs[2 * hp:2 * hp + 2])


def _dsa_attention(dq, dk, dvt, iq, ik, iw):
    b, s, w = dq.shape
    tq, tk = DSA_TQ, DSA_TK
    top_k = min(DSA_TOPK_MAX, s // 4)

    def qspec(width):
        return pl.BlockSpec((1, tq, width), lambda bi, qi: (bi, qi, 0))

    def kspec(width):
        return pl.BlockSpec((1, s, width), lambda bi, qi: (bi, 0, 0))

    return pl.pallas_call(
        functools.partial(_dsa_body, tq=tq, tk=tk, top_k=top_k),
        grid=(b, s // tq),
        in_specs=[qspec(w), qspec(iq.shape[2]), pl.BlockSpec((1, iw.shape[1], tq), lambda bi, qi: (bi, 0, qi)),
                  kspec(w), pl.BlockSpec((1, w, s), lambda bi, qi: (bi, 0, 0)), kspec(LANES)],
        out_specs=qspec(w),
        out_shape=jax.ShapeDtypeStruct((b, s, w), BF16),
        scratch_shapes=[pltpu.VMEM((s, tq), F32)] + [pltpu.VMEM((s, tq), BF16)] * 4
        + _attend_scratch(w // HEAD_DIM, tq, tk, HEAD_DIM),
        compiler_params=_cparams(("arbitrary", "arbitrary")),
        name="dsa_attention",
    )(dq, iq, iw, dk, dvt, ik)


MLA_DOWN_COLS = MLA_Q_LORA + MLA_KV_LORA + 2 * MLA_ROPE


def _mla_proj_body(h_ref, sh_ref, sc_ref, g_ref, wd_ref, qn_ref, kvn_ref, wuq_ref, wukv_ref, cos_ref, sin_ref,
                   qnope_ref, qrope_ref, knope_ref, v_ref, krope_ref, *, scale):
    x = h_ref[...]
    u = _modulated_norm(x, g_ref[...], sc_ref[0], sh_ref[0]).astype(BF16)
    cos = cos_ref[...]
    sin = sin_ref[...]
    half = MLA_ROPE // 2
    down = jnp.dot(u, wd_ref[...], preferred_element_type=F32)

    def norm(z, g):
        ms = jnp.mean(z * z, axis=-1, keepdims=True)
        return (z * lax.rsqrt(ms + RMS_EPS) * g).astype(BF16)

    cq = norm(down[:, 0:MLA_Q_LORA], qn_ref[...])
    ckv = norm(down[:, MLA_Q_LORA:MLA_Q_LORA + MLA_KV_LORA], kvn_ref[...])
    krope_ref[0] = _rope_slab(down[:, MLA_Q_LORA + MLA_KV_LORA:], cos, sin, half).astype(BF16)

    n_nope = MLA_HEADS * MLA_NOPE
    qn = jnp.dot(cq, wuq_ref[:, 0:n_nope], preferred_element_type=F32)
    qnope_ref[0] = (qn * scale).astype(BF16)
    qr = jnp.dot(cq, wuq_ref[:, n_nope:], preferred_element_type=F32)
    qr = jnp.concatenate(
        [_rope_slab(qr[:, j * LANES:(j + 1) * LANES], cos, sin, half) for j in range(qr.shape[1] // LANES)], axis=1)
    qrope_ref[0] = (qr * scale).astype(BF16)
    knope_ref[0] = jnp.dot(ckv, wukv_ref[:, 0:n_nope], preferred_element_type=F32).astype(BF16)
    v_ref[0] = jnp.dot(ckv, wukv_ref[:, n_nope:], preferred_element_type=F32).T.astype(BF16)


def _mla_proj(h, sh, sc, g, wd, qn, kvn, wuq, wukv, cos, sin, b, s):
    t, d = h.shape
    tpb = s // TM
    row = pl.BlockSpec((TM, d), lambda i: (i, 0))
    mod = pl.BlockSpec((1, 1, d), lambda i: (i // tpb, 0, 0))
    tab = pl.BlockSpec((TM, LANES), lambda i: (i, 0))

    def out(width):
        return pl.BlockSpec((1, TM, width), lambda i: (i // tpb, i % tpb, 0))

    def shp(width):
        return jax.ShapeDtypeStruct((b, s, width), BF16)

    n_nope = MLA_HEADS * MLA_NOPE
    n_rope = MLA_HEADS * MLA_ROPE
    n_v = MLA_HEADS * MLA_V
    scale = (MLA_NOPE + MLA_ROPE) ** -0.5 * LOG2E
    return pl.pallas_call(
        functools.partial(_mla_proj_body, scale=scale),
        grid=(t // TM,),
        in_specs=[row, mod, mod, _const_spec((1, d)), _const_spec((d, MLA_DOWN_COLS)),
                  _const_spec((1, MLA_Q_LORA)), _const_spec((1, MLA_KV_LORA)),
                  _const_spec((MLA_Q_LORA, n_nope + n_rope)), _const_spec((MLA_KV_LORA, n_nope + n_v)), tab, tab],
        out_specs=[out(n_nope), out(n_rope), out(n_nope),
                   pl.BlockSpec((1, n_v, TM), lambda i: (i // tpb, 0, i % tpb)), out(LANES)],
        out_shape=[shp(n_nope), shp(n_rope), shp(n_nope), jax.ShapeDtypeStruct((b, n_v, s), BF16), shp(LANES)],
        compiler_params=_cparams(("arbitrary",)),
        name="mla_in_proj",
    )(h, sh, sc, g, wd, qn, kvn, wuq, wukv, cos, sin)


def _mla_body(qn_ref, qr_ref, kn_ref, kr_ref, vt_ref, o_ref, *scratch, tq, tk):
    n_heads = qn_ref.shape[2] // MLA_NOPE
    qi = pl.program_id(2)
    qs = []
    for hd in range(n_heads):
        qr = qr_ref[0, :, (hd // 2) * LANES:(hd // 2 + 1) * LANES]
        qs.append(jnp.concatenate([qn_ref[0, :, hd * LANES:(hd + 1) * LANES],
                                   jnp.where(_head_mask(hd % 2), qr, jnp.zeros_like(qr))], axis=1))
    krow = lax.broadcasted_iota(jnp.int32, (tk, tq), 0)
    qcol = lax.broadcasted_iota(jnp.int32, (tk, tq), 1)

    def qk(hd, kt):
        ks = pl.multiple_of(kt * tk, tk)
        k = jnp.concatenate([kn_ref[0, pl.ds(ks, tk), hd * LANES:(hd + 1) * LANES], kr_ref[0, pl.ds(ks, tk), :]],
                            axis=1)
        return _qk(k, qs[hd])

    def values(hd, kt):
        return vt_ref[0, hd * LANES:(hd + 1) * LANES, pl.ds(pl.multiple_of(kt * tk, tk), tk)]

    def chunk_causal(s):
        return jnp.where(krow // CHUNK <= qcol // CHUNK, s, NEG_LOGIT)

    outs = _attend(n_heads, qi + 1, qk, values, scratch, mask_last=chunk_causal)
    for hd in range(n_heads):
        o_ref[0, :, hd * LANES:(hd + 1) * LANES] = outs[hd].T.astype(BF16)


def _mla_attention(qnope, qrope, knope, krope, vt):
    b, s, w = qnope.shape
    tq = tk = ATT_TQ
    gw = ATT_HEADS * MLA_NOPE
    gr = ATT_HEADS * MLA_ROPE
    qspec = pl.BlockSpec((1, tq, gw), lambda bi, hg, qi: (bi, qi, hg))
    qrspec = pl.BlockSpec((1, tq, gr), lambda bi, hg, qi: (bi, qi, hg))
    once = pl.Buffered(1)
    kspec = pl.BlockSpec((1, s, gw), lambda bi, hg, qi: (bi, 0, hg), pipeline_mode=once)
    krspec = pl.BlockSpec((1, s, LANES), lambda bi, hg, qi: (bi, 0, 0), pipeline_mode=once)
    vspec = pl.BlockSpec((1, gw, s), lambda bi, hg, qi: (bi, hg, 0), pipeline_mode=once)
    return pl.pallas_call(
        functools.partial(_mla_body, tq=tq, tk=tk),
        grid=(b, w // gw, s // tq),
        in_specs=[qspec, qrspec, kspec, krspec, vspec],
        out_specs=qspec,
        out_shape=jax.ShapeDtypeStruct((b, s, w), BF16),
        scratch_shapes=_attend_scratch(ATT_HEADS, tq, tk, LANES),
        compiler_params=_cparams(("arbitrary", "arbitrary", "arbitrary")),
        name="mla_attention",
    )(qnope, qrope, knope, krope, vt)


def _hyb_weight(w_in, b_f):
    o = 0
    parts = {}
    for name, width in (("fq", 512), ("fk", 512), ("fv", 512), ("ff", FOX_HEADS), ("dq", 512), ("dk", 512),
                        ("dv", 512), ("iq", 256), ("iw", IDX_HEADS), ("ik", HEAD_DIM)):
        parts[name] = w_in[:, o:o + width]
        o += width
    qs = HEAD_DIM ** -0.5
    pad = jnp.zeros((w_in.shape[0], LANES - FOX_HEADS - IDX_HEADS), w_in.dtype)
    w = jnp.concatenate([parts["fq"] * qs, parts["fk"], parts["fv"], parts["dq"] * qs, parts["dk"], parts["dv"],
                         parts["iq"] * qs, parts["ik"], parts["ik"],
                         parts["ff"], parts["iw"] * IDX_HEADS ** -0.5, pad], axis=1)
    bf = jnp.concatenate([b_f, jnp.zeros((LANES - FOX_HEADS,), b_f.dtype)]).reshape(1, LANES)
    return w.astype(BF16), bf.astype(F32)


def _mla_weights(w_down, w_uq, w_ukv):
    kr = w_down[:, MLA_Q_LORA + MLA_KV_LORA:]
    wd = jnp.concatenate([w_down, kr], axis=1)
    uq = w_uq.reshape(MLA_Q_LORA, MLA_HEADS, MLA_NOPE + MLA_ROPE)
    wuq = jnp.concatenate([uq[:, :, :MLA_NOPE].reshape(MLA_Q_LORA, -1), uq[:, :, MLA_NOPE:].reshape(MLA_Q_LORA, -1)],
                          axis=1)
    ukv = w_ukv.reshape(MLA_KV_LORA, MLA_HEADS, MLA_NOPE + MLA_V)
    wukv = jnp.concatenate([ukv[:, :, :MLA_NOPE].reshape(MLA_KV_LORA, -1),
                            ukv[:, :, MLA_NOPE:].reshape(MLA_KV_LORA, -1)], axis=1)
    return wd.astype(BF16), wuq.astype(BF16), wukv.astype(BF16)


def kernel(x, c, positions, ada_w, ada_b, norm_g, ffn_w_gate, ffn_w_up, ffn_w_down, hyb_w_in, fox_b_f, hyb_w_out,
           mla_w_down, mla_q_norm, mla_kv_norm, mla_w_uq, mla_w_ukv, mla_w_out, final_g):
    b, s, d = x.shape
    depth = ada_w.shape[0]
    t = b * s
    assert s % TM == 0 and s % FFN_TM == 0 and s % ATT_TQ == 0 and s % DSA_TQ == 0 and s % DSA_TK == 0
    assert ffn_w_gate.shape[-1] % FFN_CHUNK == 0

    mod = _modulation(c, ada_w, ada_b).reshape(depth, b, N_MOD, 1, d)
    cos_d, sin_d, cos_m, sin_m = _rope_tables(positions)
    wg = ffn_w_gate.astype(BF16)
    wu = ffn_w_up.astype(BF16)
    wdn = ffn_w_down.astype(BF16)

    h = x.reshape(t, d)
    for i in range(depth):
        sh1, sc1, g1, sh2, sc2, g2, sh3, sc3, g3 = [mod[i, :, j] for j in range(N_MOD)]
        ng = norm_g[i].reshape(3, 1, d)
        h = _ffn(h, sh1, sc1, g1, ng[0], wg, wu, wdn, (i, 0), s)
        j = i // 2
        if i % 2 == 0:
            w_in, bf = _hyb_weight(hyb_w_in[j], fox_b_f[j])
            fq, fk, fvt, dq, dk, dvt, iq, ik, cparts, iw = _hyb_proj(h, sh2, sc2, ng[1], w_in, bf, cos_d, sin_d, b, s)
            out_a = _fox_attention(fq, fk, cparts, fvt)
            out_b = _dsa_attention(dq, dk, dvt, iq, ik, iw)
            w_out = hyb_w_out[j].astype(BF16)
            half = out_a.shape[2]
            mix = (g2, [out_a.reshape(t, half), out_b.reshape(t, half)], [w_out[:half], w_out[half:]])
        else:
            wd, wuq, wukv = _mla_weights(mla_w_down[j], mla_w_uq[j], mla_w_ukv[j])
            qn, qr, kn, vt, kr = _mla_proj(h, sh2, sc2, ng[1], wd, mla_q_norm[j].reshape(1, -1),
                                           mla_kv_norm[j].reshape(1, -1), wuq, wukv, cos_m, sin_m, b, s)
            out = _mla_attention(qn, qr, kn, kr, vt)
            mix = (g2, [out.reshape(t, -1)], [mla_w_out[j].astype(BF16)])
        last = i == depth - 1
        h = _ffn(h, sh3, sc3, g3, ng[2], wg, wu, wdn, (i, 1), s,
                 final_g=final_g.reshape(1, d) if last else None, mix=mix)
    return h.reshape(b, s, d)
```

```python
import functools
import math

import jax
import jax.numpy as jnp
from jax import lax
from jax.experimental import pallas as pl
from jax.experimental.pallas import tpu as pltpu

F32 = jnp.float32
BF16 = jnp.bfloat16

CHUNK = 64
RMS_EPS = 1e-6
ROPE_THETA = 500000.0
MLA_ROPE_THETA = 10000.0
MACARON_WEIGHT = 0.5
N_MOD = 9
FOX_HEADS = 8
HEAD_DIM = 64
DSA_ROT = 16
IDX_HEADS = 4
DSA_TOPK_MAX = 256
MLA_HEADS = 8
MLA_NOPE = 128
MLA_ROPE = 64
MLA_V = 128
MLA_Q_LORA = 384
MLA_KV_LORA = 256

LANES = 128
VMEM_LIMIT = 56 * 1024 * 1024

LOG2E = math.log2(math.e)
NEG_LOGIT = -1e30
NEG_SCORE = -3e38

TM = 1024
FFN_TM = 512
FFN_CHUNK = 256
ATT_TQ = 512
ATT_HEADS = 8
FOX_STEP_HEADS = 8
DSA_TQ = 256
DSA_TK = 512
COUNT_ROWS = 32


def _cparams(sem):
    return pltpu.CompilerParams(dimension_semantics=sem, vmem_limit_bytes=VMEM_LIMIT)


def _const_spec(shape):
    nd = len(shape)
    return pl.BlockSpec(shape, lambda *_: (0,) * nd, pipeline_mode=pl.Buffered(1))


def _silu(x):
    return x * jax.nn.sigmoid(x)


def _modulated_norm(x, g, sc, sh):
    ms = jnp.mean(x * x, axis=-1, keepdims=True)
    y = x * lax.rsqrt(ms + RMS_EPS) * g
    return y * (1.0 + sc) + sh


def _mod_body(c_ref, w_ref, b_ref, o_ref):
    cond = _silu(c_ref[...]).astype(BF16)
    o_ref[0] = jnp.dot(cond, w_ref[0].astype(BF16), preferred_element_type=F32) + b_ref[0]


def _modulation(c, ada_w, ada_b):
    depth, d, n = ada_w.shape
    b = c.shape[0]
    tn = n // N_MOD
    return pl.pallas_call(
        _mod_body,
        grid=(depth, n // tn),
        in_specs=[pl.BlockSpec((b, d), lambda i, j: (0, 0)),
                  pl.BlockSpec((1, d, tn), lambda i, j: (i, 0, j)),
                  pl.BlockSpec((1, 1, tn), lambda i, j: (i, 0, j))],
        out_specs=pl.BlockSpec((1, b, tn), lambda i, j: (i, 0, j)),
        out_shape=jax.ShapeDtypeStruct((depth, b, n), F32),
        compiler_params=_cparams(("arbitrary", "arbitrary")),
        name="adaln_mod",
    )(c, ada_w, ada_b.reshape(depth, 1, n))


def _rope_table_body(pos_ref, invd_ref, sgnd_ref, invm_ref, sgnm_ref, cd_ref, sd_ref, cm_ref, sm_ref):
    pos = pos_ref[...].astype(F32)
    for inv_ref, sgn_ref, c_ref, s_ref in ((invd_ref, sgnd_ref, cd_ref, sd_ref),
                                           (invm_ref, sgnm_ref, cm_ref, sm_ref)):
        ang = pos * inv_ref[...]
        sgn = sgn_ref[...]
        c_ref[...] = jnp.where(sgn != 0.0, jnp.cos(ang), 1.0)
        s_ref[...] = sgn * jnp.sin(ang)


def _lane_pattern(rot, theta):
    half = rot // 2
    inv_freq = jnp.exp(-math.log(theta) * 2.0 * jnp.arange(half, dtype=F32) / rot)
    d = jnp.arange(LANES) % HEAD_DIM
    inv = jnp.where(d < rot, inv_freq[d % half], 0.0).astype(F32)
    sgn = jnp.where(d < half, -1.0, jnp.where(d < rot, 1.0, 0.0)).astype(F32)
    return inv.reshape(1, LANES), sgn.reshape(1, LANES)


def _rope_tables(positions):
    t = positions.size
    invd, sgnd = _lane_pattern(DSA_ROT, ROPE_THETA)
    invm, sgnm = _lane_pattern(MLA_ROPE, MLA_ROPE_THETA)
    tm = 1024
    row = pl.BlockSpec((tm, LANES), lambda i: (i, 0))
    vec = pl.BlockSpec((1, LANES), lambda i: (0, 0))
    tab = jax.ShapeDtypeStruct((t, LANES), F32)
    return pl.pallas_call(
        _rope_table_body,
        grid=(t // tm,),
        in_specs=[pl.BlockSpec((tm, 1), lambda i: (i, 0)), vec, vec, vec, vec],
        out_specs=[row, row, row, row],
        out_shape=[tab, tab, tab, tab],
        compiler_params=_cparams(("arbitrary",)),
        name="rope_tables",
    )(positions.reshape(t, 1), invd, sgnd, invm, sgnm)


def _rope_slab(y, cos, sin, half):
    lane = lax.broadcasted_iota(jnp.int32, (1, LANES), 1) % HEAD_DIM
    first = lane < half
    partner = jnp.where(first, pltpu.roll(y, LANES - half, 1), pltpu.roll(y, half, 1))
    return y * cos + partner * sin


def _ffn_body(h_ref, sh_ref, sc_ref, gt_ref, g_ref, wg_ref, wu_ref, wd_ref, *rest, nf, fc, final, n_mix):
    rest = list(rest)
    x = h_ref[...]
    if n_mix:
        mg_ref = rest.pop(0)
        mix = None
        for x_ref, w_ref in zip(rest[:n_mix], rest[n_mix:2 * n_mix]):
            part = jnp.dot(x_ref[...], w_ref[...], preferred_element_type=F32)
            mix = part if mix is None else mix + part
        x = x + mg_ref[0] * mix
        rest = rest[2 * n_mix:]
    if final:
        fg_ref, o_ref, acc_ref = rest
    else:
        o_ref, acc_ref = rest
    u = _modulated_norm(x, g_ref[...], sc_ref[0], sh_ref[0]).astype(BF16)
    for f in range(nf):
        sl = slice(f * fc, (f + 1) * fc)
        gp = jnp.dot(u, wg_ref[:, sl], preferred_element_type=F32)
        up = jnp.dot(u, wu_ref[:, sl], preferred_element_type=F32)
        a = (_silu(gp) * up).astype(BF16)
        d = jnp.dot(a, wd_ref[sl, :], preferred_element_type=F32)
        if f == 0:
            acc_ref[...] = d
        else:
            acc_ref[...] += d
    y = x + (MACARON_WEIGHT * gt_ref[0]) * acc_ref[...]
    if final:
        ms = jnp.mean(y * y, axis=-1, keepdims=True)
        y = y * lax.rsqrt(ms + RMS_EPS) * fg_ref[...]
    o_ref[...] = y


def _ffn(h, sh, sc, gt, g, wg, wu, wd, which, s, final_g=None, mix=None):
    t, d = h.shape
    f = wg.shape[-1]
    tm = FFN_TM
    tpb = s // tm
    row = pl.BlockSpec((tm, d), lambda i: (i, 0))
    mod = pl.BlockSpec((1, 1, d), lambda i: (i // tpb, 0, 0))

    def weight(rows, cols):
        return pl.BlockSpec((None, None, rows, cols), lambda i: (*which, 0, 0), pipeline_mode=pl.Buffered(1))

    in_specs = [row, mod, mod, mod, _const_spec((1, d)), weight(d, f), weight(d, f), weight(f, d)]
    args = [h, sh, sc, gt, g, wg, wu, wd]
    n_mix = 0
    if mix is not None:
        mgate, xs, ws = mix
        n_mix = len(xs)
        in_specs += [mod] + [pl.BlockSpec((tm, x.shape[1]), lambda i: (i, 0)) for x in xs] + [_const_spec(w.shape) for w in ws]
        args += [mgate, *xs, *ws]
    if final_g is not None:
        in_specs.append(_const_spec((1, d)))
        args.append(final_g)
    return pl.pallas_call(
        functools.partial(_ffn_body, nf=f // FFN_CHUNK, fc=FFN_CHUNK, final=final_g is not None, n_mix=n_mix),
        grid=(t // tm,),
        in_specs=in_specs,
        out_specs=row,
        out_shape=jax.ShapeDtypeStruct((t, d), F32),
        scratch_shapes=[pltpu.VMEM((tm, d), F32)],
        compiler_params=_cparams(("arbitrary",)),
        name="swiglu_half_step",
    )(*args)


HYB_W = 512
HYB_COLS = 6 * HYB_W + 256 + 128 + 128


def _hyb_proj_body(h_ref, sh_ref, sc_ref, g_ref, w_ref, bf_ref, cos_ref, sin_ref,
                   fq_ref, fk_ref, fvt_ref, dq_ref, dk_ref, dvt_ref, iq_ref, ik_ref, cum_ref, iw_ref,
                   carry_ref, *, tpb):
    i = pl.program_id(0)
    x = h_ref[...]
    tm = x.shape[0]
    u = _modulated_norm(x, g_ref[...], sc_ref[0], sh_ref[0]).astype(BF16)
    cos = cos_ref[...]
    sin = sin_ref[...]
    half = DSA_ROT // 2

    def proj(c0, width):
        return jnp.dot(u, w_ref[:, c0:c0 + width], preferred_element_type=F32)

    def roped(y):
        return jnp.concatenate(
            [_rope_slab(y[:, j * LANES:(j + 1) * LANES], cos, sin, half) for j in range(y.shape[1] // LANES)],
            axis=1)

    fq_ref[0] = (proj(0 * HYB_W, HYB_W) * LOG2E).astype(BF16)
    fk_ref[0] = proj(1 * HYB_W, HYB_W).astype(BF16)
    fvt_ref[0] = proj(2 * HYB_W, HYB_W).T.astype(BF16)
    dq_ref[0] = (roped(proj(3 * HYB_W, HYB_W)) * LOG2E).astype(BF16)
    dk_ref[0] = roped(proj(4 * HYB_W, HYB_W)).astype(BF16)
    dvt_ref[0] = proj(5 * HYB_W, HYB_W).T.astype(BF16)
    tail = proj(6 * HYB_W, 512)
    iq_ref[0] = roped(tail[:, 0:256]).astype(BF16)
    ik_ref[0] = roped(tail[:, 256:384]).astype(BF16)
    gates = tail[:, 384:512]
    iw_ref[0] = gates.T[FOX_HEADS:2 * FOX_HEADS, :]

    z = gates + bf_ref[...]
    logf = jnp.minimum(z, 0.0) - jnp.log1p(jnp.exp(-jnp.abs(z)))
    rows = lax.broadcasted_iota(jnp.int32, (tm, LANES), 0)
    c = logf
    k = 1
    while k < tm:
        c = c + jnp.where(rows >= k, pltpu.roll(c, k, 0), 0.0)
        k *= 2

    @pl.when(i % tpb == 0)
    def _():
        carry_ref[...] = jnp.zeros_like(carry_ref)

    c = c + carry_ref[0:1, :]
    carry_ref[...] = jnp.broadcast_to(c[tm - 1:tm, :], carry_ref.shape)
    lane = lax.broadcasted_iota(jnp.int32, (1, LANES), 1)
    c = jnp.where(lane < FOX_HEADS, c * LOG2E, 0.0)
    hi = c.astype(BF16).astype(F32)
    mid = (c - hi).astype(BF16).astype(F32)
    lo = (c - hi - mid).astype(BF16).astype(F32)
    cum_ref[0] = (hi + pltpu.roll(mid, FOX_HEADS, 1) + pltpu.roll(lo, 2 * FOX_HEADS, 1)).astype(BF16)


def _hyb_proj(h, sh, sc, g, w, bf, cos, sin, b, s):
    t, d = h.shape
    tpb = s // TM
    row = pl.BlockSpec((TM, d), lambda i: (i, 0))
    mod = pl.BlockSpec((1, 1, d), lambda i: (i // tpb, 0, 0))
    tab = pl.BlockSpec((TM, LANES), lambda i: (i, 0))

    def out(width):
        return pl.BlockSpec((1, TM, width), lambda i: (i // tpb, i % tpb, 0))

    def shp(width, dt=BF16):
        return jax.ShapeDtypeStruct((b, s, width), dt)

    def out_t(width):
        return pl.BlockSpec((1, width, TM), lambda i: (i // tpb, 0, i % tpb))

    def shp_t(width, dt=BF16):
        return jax.ShapeDtypeStruct((b, width, s), dt)

    return pl.pallas_call(
        functools.partial(_hyb_proj_body, tpb=tpb),
        grid=(t // TM,),
        in_specs=[row, mod, mod, _const_spec((1, d)), _const_spec((d, HYB_COLS)), _const_spec((1, LANES)), tab, tab],
        out_specs=[out(HYB_W), out(HYB_W), out_t(HYB_W), out(HYB_W), out(HYB_W), out_t(HYB_W),
                   out(256), out(LANES), out(LANES), out_t(FOX_HEADS)],
        out_shape=[shp(HYB_W), shp(HYB_W), shp_t(HYB_W), shp(HYB_W), shp(HYB_W), shp_t(HYB_W),
                   shp(256), shp(LANES), shp(LANES), shp_t(FOX_HEADS, F32)],
        scratch_shapes=[pltpu.VMEM((8, LANES), F32)],
        compiler_params=_cparams(("arbitrary",)),
        name="hybrid_in_proj",
    )(h, sh, sc, g, w, bf, cos, sin)


def _attend_scratch(n_heads, tq, tk, n):
    return [pltpu.VMEM((2, n_heads, tk, tq), F32), pltpu.VMEM((2, n_heads, 1, tq), F32),
            pltpu.VMEM((n_heads, 1, tq), F32), pltpu.VMEM((n_heads, 1, tq), F32), pltpu.VMEM((n_heads, n, tq), F32)]


SUM_ROWS = 16


def _attend(n_heads, n_tiles, qk, values, scratch, mask_last=None, mxu_sum=False):
    s_ref, smax_ref, m_ref, l_ref, acc_ref = scratch
    n_out = acc_ref.shape[1] - (SUM_ROWS if mxu_sum else 0)

    def stage(slot, kt):
        for hd in range(n_heads):
            s = qk(hd, kt)
            s_ref[slot, hd] = s
            smax_ref[slot, hd] = jnp.max(s, axis=0, keepdims=True)

    def step(slot, kt, mask=None):
        for hd in range(n_heads):
            s = s_ref[slot, hd]
            smax = smax_ref[slot, hd]
            if mask is not None:
                s = mask(s)
                smax = jnp.max(s, axis=0, keepdims=True)
            m = m_ref[hd]
            m_new = jnp.maximum(m, smax)
            alpha = jnp.exp2(m - m_new)
            p = jnp.exp2(s - m_new)
            m_ref[hd] = m_new
            vals = values(hd, kt)
            if mxu_sum:
                vals = jnp.concatenate([vals, jnp.ones((SUM_ROWS, vals.shape[1]), BF16)], axis=0)
            else:
                l_ref[hd] = alpha * l_ref[hd] + jnp.sum(p, axis=0, keepdims=True)
            acc_ref[hd] = alpha * acc_ref[hd] + jnp.dot(vals, p.astype(BF16), preferred_element_type=F32)

    m_ref[...] = jnp.full(m_ref.shape, NEG_LOGIT, F32)
    l_ref[...] = jnp.zeros(l_ref.shape, F32)
    acc_ref[...] = jnp.zeros(acc_ref.shape, F32)
    stage(0, 0)
    n_pairs = (n_tiles - 1) // 2

    def pair(i, _):
        stage(1, 2 * i + 1)
        step(0, 2 * i)
        stage(0, 2 * i + 2)
        step(1, 2 * i + 1)
        return 0

    lax.fori_loop(0, n_pairs, pair, 0)
    odd_tail = n_tiles - 2 * n_pairs == 2

    @pl.when(odd_tail)
    def _():
        stage(1, n_tiles - 1)
        step(0, n_tiles - 2)

    step(jnp.where(odd_tail, 1, 0), n_tiles - 1, mask=mask_last)
    if mxu_sum:
        return [acc_ref[hd, 0:n_out, :] / acc_ref[hd, n_out:n_out + 1, :] for hd in range(n_heads)]
    return [acc_ref[hd] / l_ref[hd] for hd in range(n_heads)]


def _head_mask(j):
    lane = lax.broadcasted_iota(jnp.int32, (1, LANES), 1)
    return (lane // HEAD_DIM) == j


def _qk(q, k):
    return lax.dot_general(q, k, (((1,), (1,)), ((), ())), preferred_element_type=F32)


def _pair_out(outs):
    return jnp.concatenate(outs, axis=0).T.astype(BF16)


def _fox_body(q_ref, k_ref, c_ref, vt_ref, o_ref, *scratch, tq, tk):
    n_heads = q_ref.shape[2] // HEAD_DIM
    hg = pl.program_id(1)
    qi = pl.program_id(2)
    krow = lax.broadcasted_iota(jnp.int32, (tk, tq), 0)
    qcol = lax.broadcasted_iota(jnp.int32, (tk, tq), 1)
    lane = lax.broadcasted_iota(jnp.int32, (tq, LANES), 1)
    qs = []
    for hd in range(n_heads):
        pick = (lane % FOX_HEADS == hg * n_heads + hd) & (lane < 3 * FOX_HEADS)
        sel = jnp.where(pick, -1.0, 0.0).astype(BF16)
        qfull = q_ref[0, :, (hd // 2) * LANES:(hd // 2 + 1) * LANES]
        qs.append(jnp.concatenate([jnp.where(_head_mask(hd % 2), qfull, jnp.zeros_like(qfull)), sel], axis=1))

    def qk(hd, kt):
        ks = pl.multiple_of(kt * tk, tk)
        lanes = slice((hd // 2) * LANES, (hd // 2 + 1) * LANES)
        k = jnp.concatenate([k_ref[0, pl.ds(ks, tk), lanes], c_ref[0, pl.ds(ks, tk), :]], axis=1)
        return _qk(k, qs[hd])

    def values(hd, kt):
        return vt_ref[0, hd * HEAD_DIM:(hd + 1) * HEAD_DIM, pl.ds(pl.multiple_of(kt * tk, tk), tk)]

    def causal(s):
        return jnp.where(krow <= qcol, s, NEG_LOGIT)

    outs = _attend(n_heads, qi + 1, qk, values, scratch, mask_last=causal)
    for hp in range(n_heads // 2):
        o_ref[0, :, hp * LANES:(hp + 1) * LANES] = _pair_out(outs[2 * hp:2 * hp + 2])


def _fox_attention(fq, fk, cparts, fvt):
    b, s, w = fq.shape
    tq = tk = ATT_TQ
    gw = FOX_STEP_HEADS * HEAD_DIM
    qspec = pl.BlockSpec((1, tq, gw), lambda bi, hg, qi: (bi, qi, hg))
    kspec = pl.BlockSpec((1, s, gw), lambda bi, hg, qi: (bi, 0, hg))
    cspec = pl.BlockSpec((1, s, LANES), lambda bi, hg, qi: (bi, 0, 0))
    vspec = pl.BlockSpec((1, gw, s), lambda bi, hg, qi: (bi, hg, 0))
    return pl.pallas_call(
        functools.partial(_fox_body, tq=tq, tk=tk),
        grid=(b, w // gw, s // tq),
        in_specs=[qspec, kspec, cspec, vspec],
        out_specs=qspec,
        out_shape=jax.ShapeDtypeStruct((b, s, w), BF16),
        scratch_shapes=_attend_scratch(FOX_STEP_HEADS, tq, tk, HEAD_DIM),
        compiler_params=_cparams(("arbitrary", "arbitrary", "arbitrary")),
        name="fox_attention",
    )(fq, fk, cparts, fvt)


def _hi_key_to_bits(h):
    return jnp.where(h >= 0x8000, h - 0x8000, (~h) & 0xFFFF)


def _tree_sum(parts):
    while len(parts) > 1:
        parts = [a + b for a, b in zip(parts[0::2], parts[1::2])] + ([parts[-1]] if len(parts) % 2 else [])
    return parts[0]


def _dsa_body(dq_ref, iq_ref, iw_ref, dk_ref, dvt_ref, ik_ref, o_ref, sc_ref, a1_ref, d1_ref, d0_ref, a2_ref,
              *scratch, tq, tk, top_k):
    qi = pl.program_id(1)
    n_tiles = ((qi + 1) * tq + tk - 1) // tk
    qpos = lax.broadcasted_iota(jnp.int32, (1, tq), 1) + qi * tq
    limit = (qpos // CHUNK + 1) * CHUNK
    krow = lax.broadcasted_iota(jnp.int32, (tk, tq), 0)
    kf = jnp.float32(top_k)

    iqf = iq_ref[0]
    iw = iw_ref[0]
    iqs = [jnp.where(_head_mask(h % 2), iqf[:, (h // 2) * LANES:(h // 2 + 1) * LANES], jnp.zeros((tq, LANES), BF16))
           for h in range(IDX_HEADS)]
    ws = [iw[h:h + 1, :] for h in range(IDX_HEADS)]

    def score_tile(kt, _):
        ks = pl.multiple_of(kt * tk, tk)
        ik = ik_ref[0, pl.ds(ks, tk), :]
        sc = None
        for h in range(IDX_HEADS):
            term = jnp.maximum(_qk(ik, iqs[h]), 0.0) * ws[h]
            sc = term if sc is None else sc + term
        sc = jnp.where(krow + ks < limit, sc, NEG_SCORE)
        sc = jnp.where(sc == 0.0, 0.0, sc)
        sc_ref[pl.ds(ks, tk), :] = sc
        bits = pltpu.bitcast(sc, jnp.int32)
        a1_ref[pl.ds(ks, tk), :] = pltpu.bitcast(bits & jnp.int32(-65536), F32).astype(BF16)
        lo = bits & 0xFFFF
        klo = jnp.where(bits < 0, 0xFFFF - lo, lo)
        d1_ref[pl.ds(ks, tk), :] = lax.shift_right_logical(klo, 8).astype(F32).astype(BF16)
        d0_ref[pl.ds(ks, tk), :] = (klo & 0xFF).astype(F32).astype(BF16)
        return 0

    lax.fori_loop(0, n_tiles, score_tile, 0)

    def tile_count(mask):
        c = jnp.where(mask, jnp.ones((), BF16), jnp.zeros((), BF16))
        return _tree_sum([c[i * COUNT_ROWS:(i + 1) * COUNT_ROWS, :] for i in range(tk // COUNT_ROWS)]).astype(F32)

    zero_cnt = jnp.zeros((COUNT_ROWS, tq), F32)

    def count_packed(ref, op, ref_b):
        def body(kt, cnt):
            return cnt + tile_count(op(ref[pl.ds(pl.multiple_of(kt * tk, tk), tk), :], ref_b))
        return jnp.sum(lax.fori_loop(0, n_tiles, body, zero_cnt), axis=0, keepdims=True)

    def hi_float(h):
        return pltpu.bitcast(lax.shift_left(_hi_key_to_bits(h), 16), F32).astype(BF16)

    def hi_step(it, h):
        cand = h | lax.shift_left(jnp.int32(1), 15 - it)
        cnt = count_packed(a1_ref, lambda x, c: x >= c, hi_float(cand))
        return jnp.where(cnt >= kf, cand, h)

    h = lax.fori_loop(0, 16, hi_step, jnp.zeros((1, tq), jnp.int32))
    hbits = _hi_key_to_bits(h)
    hbits = jnp.where((hbits & 0x7F80) == 0, hbits & 0x8000, hbits)
    thr_hi = pltpu.bitcast(lax.shift_left(hbits, 16), F32).astype(BF16)

    def restrict(src_ref, match, digit_ref):
        def body(kt, cnt):
            rows = pl.ds(pl.multiple_of(kt * tk, tk), tk)
            x = src_ref[rows, :]
            a2_ref[rows, :] = jnp.where(x == match, digit_ref[rows, :], -jnp.ones((), BF16))
            return cnt + tile_count(x > match)
        return jnp.sum(lax.fori_loop(0, n_tiles, body, zero_cnt), axis=0, keepdims=True)

    def digit(need):
        def step(it, v):
            cand = v + lax.shift_left(jnp.int32(1), 7 - it).astype(F32)
            cnt = count_packed(a2_ref, lambda x, c: x >= c, cand.astype(BF16))
            return jnp.where(cnt >= need, cand, v)
        return lax.fori_loop(0, 8, step, jnp.zeros((1, tq), F32))

    need_hi = kf - restrict(a1_ref, thr_hi, d1_ref)
    v1 = digit(need_hi)
    need_lo = need_hi - restrict(a2_ref, v1.astype(BF16), d0_ref)
    v0 = digit(need_lo)
    klo = (v1 * 256.0 + v0).astype(jnp.int32)
    lo = jnp.where(hbits >= 0x8000, 0xFFFF - klo, klo)
    thr = pltpu.bitcast(lax.shift_left(hbits, 16) | lo, F32)

    v0b = v0.astype(BF16)

    need = need_lo - count_packed(a2_ref, lambda x, c: x > c, v0b)
    need = jnp.where(thr <= NEG_SCORE, 0.0, need)

    lower = jnp.where(lax.broadcasted_iota(jnp.int32, (tk, tk), 0) >= lax.broadcasted_iota(jnp.int32, (tk, tk), 1),
                      1.0, 0.0).astype(BF16)

    def bias_tile(kt, seen):
        rows = pl.ds(pl.multiple_of(kt * tk, tk), tk)
        sc = sc_ref[rows, :]
        tied = jnp.where(a2_ref[rows, :] == v0b, jnp.ones((), BF16), jnp.zeros((), BF16))
        rank = jnp.dot(lower, tied, preferred_element_type=F32) + seen
        tie_bias = jnp.where(rank <= need, 0.0, NEG_LOGIT)
        sc_ref[rows, :] = jnp.where(sc > thr, 0.0, jnp.where(sc == thr, tie_bias, NEG_LOGIT))
        return rank[tk - 1:tk, :]

    lax.fori_loop(0, n_tiles, bias_tile, jnp.zeros((1, tq), F32))

    n_heads = dq_ref.shape[2] // HEAD_DIM
    qs = []
    for hd in range(n_heads):
        qfull = dq_ref[0, :, (hd // 2) * LANES:(hd // 2 + 1) * LANES]
        qs.append(jnp.where(_head_mask(hd % 2), qfull, jnp.zeros_like(qfull)))

    def qk(hd, kt):
        ks = pl.multiple_of(kt * tk, tk)
        lanes = slice((hd // 2) * LANES, (hd // 2 + 1) * LANES)
        return _qk(dk_ref[0, pl.ds(ks, tk), lanes], qs[hd]) + sc_ref[pl.ds(ks, tk), :]

    def values(hd, kt):
        return dvt_ref[0, hd * HEAD_DIM:(hd + 1) * HEAD_DIM, pl.ds(pl.multiple_of(kt * tk, tk), tk)]

    outs = _attend(n_heads, n_tiles, qk, values, scratch, mxu_sum=True)
    for hp in range(n_heads // 2):
        o_ref[0, :, hp * LANES:(hp + 1) * LANES] = _pair_out(outs[2 * hp:2 * hp + 2])


def _dsa_attention(dq, dk, dvt, iq, ik, iw):
    b, s, w = dq.shape
    tq, tk = DSA_TQ, DSA_TK
    top_k = min(DSA_TOPK_MAX, s // 4)

    def qspec(width):
        return pl.BlockSpec((1, tq, width), lambda bi, qi: (bi, qi, 0))

    def kspec(width):
        return pl.BlockSpec((1, s, width), lambda bi, qi: (bi, 0, 0))

    return pl.pallas_call(
        functools.partial(_dsa_body, tq=tq, tk=tk, top_k=top_k),
        grid=(b, s // tq),
        in_specs=[qspec(w), qspec(iq.shape[2]), pl.BlockSpec((1, iw.shape[1], tq), lambda bi, qi: (bi, 0, qi)),
                  kspec(w), pl.BlockSpec((1, w, s), lambda bi, qi: (bi, 0, 0)), kspec(LANES)],
        out_specs=qspec(w),
        out_shape=jax.ShapeDtypeStruct((b, s, w), BF16),
        scratch_shapes=[pltpu.VMEM((s, tq), F32)] + [pltpu.VMEM((s, tq), BF16)] * 4
        + _attend_scratch(w // HEAD_DIM, tq, tk, HEAD_DIM + SUM_ROWS),
        compiler_params=_cparams(("arbitrary", "arbitrary")),
        name="dsa_attention",
    )(dq, iq, iw, dk, dvt, ik)


MLA_DOWN_COLS = MLA_Q_LORA + MLA_KV_LORA + 2 * MLA_ROPE


def _mla_proj_body(h_ref, sh_ref, sc_ref, g_ref, wd_ref, qn_ref, kvn_ref, wuq_ref, wukv_ref, cos_ref, sin_ref,
                   qnope_ref, qrope_ref, knope_ref, v_ref, krope_ref, *, scale):
    x = h_ref[...]
    u = _modulated_norm(x, g_ref[...], sc_ref[0], sh_ref[0]).astype(BF16)
    cos = cos_ref[...]
    sin = sin_ref[...]
    half = MLA_ROPE // 2
    down = jnp.dot(u, wd_ref[...], preferred_element_type=F32)

    def norm(z, g):
        ms = jnp.mean(z * z, axis=-1, keepdims=True)
        return (z * lax.rsqrt(ms + RMS_EPS) * g).astype(BF16)

    cq = norm(down[:, 0:MLA_Q_LORA], qn_ref[...])
    ckv = norm(down[:, MLA_Q_LORA:MLA_Q_LORA + MLA_KV_LORA], kvn_ref[...])
    krope_ref[0] = _rope_slab(down[:, MLA_Q_LORA + MLA_KV_LORA:], cos, sin, half).astype(BF16)

    n_nope = MLA_HEADS * MLA_NOPE
    qn = jnp.dot(cq, wuq_ref[:, 0:n_nope], preferred_element_type=F32)
    qnope_ref[0] = (qn * scale).astype(BF16)
    qr = jnp.dot(cq, wuq_ref[:, n_nope:], preferred_element_type=F32)
    qr = jnp.concatenate(
        [_rope_slab(qr[:, j * LANES:(j + 1) * LANES], cos, sin, half) for j in range(qr.shape[1] // LANES)], axis=1)
    qrope_ref[0] = (qr * scale).astype(BF16)
    knope_ref[0] = jnp.dot(ckv, wukv_ref[:, 0:n_nope], preferred_element_type=F32).astype(BF16)
    v_ref[0] = jnp.dot(ckv, wukv_ref[:, n_nope:], preferred_element_type=F32).T.astype(BF16)


def _mla_proj(h, sh, sc, g, wd, qn, kvn, wuq, wukv, cos, sin, b, s):
    t, d = h.shape
    tpb = s // TM
    row = pl.BlockSpec((TM, d), lambda i: (i, 0))
    mod = pl.BlockSpec((1, 1, d), lambda i: (i // tpb, 0, 0))
    tab = pl.BlockSpec((TM, LANES), lambda i: (i, 0))

    def out(width):
        return pl.BlockSpec((1, TM, width), lambda i: (i // tpb, i % tpb, 0))

    def shp(width):
        return jax.ShapeDtypeStruct((b, s, width), BF16)

    n_nope = MLA_HEADS * MLA_NOPE
    n_rope = MLA_HEADS * MLA_ROPE
    n_v = MLA_HEADS * MLA_V
    scale = (MLA_NOPE + MLA_ROPE) ** -0.5 * LOG2E
    return pl.pallas_call(
        functools.partial(_mla_proj_body, scale=scale),
        grid=(t // TM,),
        in_specs=[row, mod, mod, _const_spec((1, d)), _const_spec((d, MLA_DOWN_COLS)),
                  _const_spec((1, MLA_Q_LORA)), _const_spec((1, MLA_KV_LORA)),
                  _const_spec((MLA_Q_LORA, n_nope + n_rope)), _const_spec((MLA_KV_LORA, n_nope + n_v)), tab, tab],
        out_specs=[out(n_nope), out(n_rope), out(n_nope),
                   pl.BlockSpec((1, n_v, TM), lambda i: (i // tpb, 0, i % tpb)), out(LANES)],
        out_shape=[shp(n_nope), shp(n_rope), shp(n_nope), jax.ShapeDtypeStruct((b, n_v, s), BF16), shp(LANES)],
        compiler_params=_cparams(("arbitrary",)),
        name="mla_in_proj",
    )(h, sh, sc, g, wd, qn, kvn, wuq, wukv, cos, sin)


def _mla_body(qn_ref, qr_ref, kn_ref, kr_ref, vt_ref, o_ref, *scratch, tq, tk):
    n_heads = qn_ref.shape[2] // MLA_NOPE
    qi = pl.program_id(2)
    qs = []
    for hd in range(n_heads):
        qr = qr_ref[0, :, (hd // 2) * LANES:(hd // 2 + 1) * LANES]
        qs.append(jnp.concatenate([qn_ref[0, :, hd * LANES:(hd + 1) * LANES],
                                   jnp.where(_head_mask(hd % 2), qr, jnp.zeros_like(qr))], axis=1))
    krow = lax.broadcasted_iota(jnp.int32, (tk, tq), 0)
    qcol = lax.broadcasted_iota(jnp.int32, (tk, tq), 1)

    def qk(hd, kt):
        ks = pl.multiple_of(kt * tk, tk)
        k = jnp.concatenate([kn_ref[0, pl.ds(ks, tk), hd * LANES:(hd + 1) * LANES], kr_ref[0, pl.ds(ks, tk), :]],
                            axis=1)
        return _qk(k, qs[hd])

    def values(hd, kt):
        return vt_ref[0, hd * LANES:(hd + 1) * LANES, pl.ds(pl.multiple_of(kt * tk, tk), tk)]

    def chunk_causal(s):
        return jnp.where(krow // CHUNK <= qcol // CHUNK, s, NEG_LOGIT)

    outs = _attend(n_heads, qi + 1, qk, values, scratch, mask_last=chunk_causal)
    for hd in range(n_heads):
        o_ref[0, :, hd * LANES:(hd + 1) * LANES] = outs[hd].T.astype(BF16)


def _mla_attention(qnope, qrope, knope, krope, vt):
    b, s, w = qnope.shape
    tq = tk = ATT_TQ
    gw = ATT_HEADS * MLA_NOPE
    gr = ATT_HEADS * MLA_ROPE
    qspec = pl.BlockSpec((1, tq, gw), lambda bi, hg, qi: (bi, qi, hg))
    qrspec = pl.BlockSpec((1, tq, gr), lambda bi, hg, qi: (bi, qi, hg))
    once = pl.Buffered(1)
    kspec = pl.BlockSpec((1, s, gw), lambda bi, hg, qi: (bi, 0, hg), pipeline_mode=once)
    krspec = pl.BlockSpec((1, s, LANES), lambda bi, hg, qi: (bi, 0, 0), pipeline_mode=once)
    vspec = pl.BlockSpec((1, gw, s), lambda bi, hg, qi: (bi, hg, 0), pipeline_mode=once)
    return pl.pallas_call(
        functools.partial(_mla_body, tq=tq, tk=tk),
        grid=(b, w // gw, s // tq),
        in_specs=[qspec, qrspec, kspec, krspec, vspec],
        out_specs=qspec,
        out_shape=jax.ShapeDtypeStruct((b, s, w), BF16),
        scratch_shapes=_attend_scratch(ATT_HEADS, tq, tk, LANES),
        compiler_params=_cparams(("arbitrary", "arbitrary", "arbitrary")),
        name="mla_attention",
    )(qnope, qrope, knope, krope, vt)


def _hyb_weight(w_in, b_f):
    o = 0
    parts = {}
    for name, width in (("fq", 512), ("fk", 512), ("fv", 512), ("ff", FOX_HEADS), ("dq", 512), ("dk", 512),
                        ("dv", 512), ("iq", 256), ("iw", IDX_HEADS), ("ik", HEAD_DIM)):
        parts[name] = w_in[:, o:o + width]
        o += width
    qs = HEAD_DIM ** -0.5
    pad = jnp.zeros((w_in.shape[0], LANES - FOX_HEADS - IDX_HEADS), w_in.dtype)
    w = jnp.concatenate([parts["fq"] * qs, parts["fk"], parts["fv"], parts["dq"] * qs, parts["dk"], parts["dv"],
                         parts["iq"] * qs, parts["ik"], parts["ik"],
                         parts["ff"], parts["iw"] * IDX_HEADS ** -0.5, pad], axis=1)
    bf = jnp.concatenate([b_f, jnp.zeros((LANES - FOX_HEADS,), b_f.dtype)]).reshape(1, LANES)
    return w.astype(BF16), bf.astype(F32)


def _mla_weights(w_down, w_uq, w_ukv):
    kr = w_down[:, MLA_Q_LORA + MLA_KV_LORA:]
    wd = jnp.concatenate([w_down, kr], axis=1)
    uq = w_uq.reshape(MLA_Q_LORA, MLA_HEADS, MLA_NOPE + MLA_ROPE)
    wuq = jnp.concatenate([uq[:, :, :MLA_NOPE].reshape(MLA_Q_LORA, -1), uq[:, :, MLA_NOPE:].reshape(MLA_Q_LORA, -1)],
                          axis=1)
    ukv = w_ukv.reshape(MLA_KV_LORA, MLA_HEADS, MLA_NOPE + MLA_V)
    wukv = jnp.concatenate([ukv[:, :, :MLA_NOPE].reshape(MLA_KV_LORA, -1),
                            ukv[:, :, MLA_NOPE:].reshape(MLA_KV_LORA, -1)], axis=1)
    return wd.astype(BF16), wuq.astype(BF16), wukv.astype(BF16)


def kernel(x, c, positions, ada_w, ada_b, norm_g, ffn_w_gate, ffn_w_up, ffn_w_down, hyb_w_in, fox_b_f, hyb_w_out,
           mla_w_down, mla_q_norm, mla_kv_norm, mla_w_uq, mla_w_ukv, mla_w_out, final_g):
    b, s, d = x.shape
    depth = ada_w.shape[0]
    t = b * s
    assert s % TM == 0 and s % FFN_TM == 0 and s % ATT_TQ == 0 and s % DSA_TQ == 0 and s % DSA_TK == 0
    assert ffn_w_gate.shape[-1] % FFN_CHUNK == 0

    mod = _modulation(c, ada_w, ada_b).reshape(depth, b, N_MOD, 1, d)
    cos_d, sin_d, cos_m, sin_m = _rope_tables(positions)
    wg = ffn_w_gate.astype(BF16)
    wu = ffn_w_up.astype(BF16)
    wdn = ffn_w_down.astype(BF16)

    h = x.reshape(t, d)
    for i in range(depth):
        sh1, sc1, g1, sh2, sc2, g2, sh3, sc3, g3 = [mod[i, :, j] for j in range(N_MOD)]
        ng = norm_g[i].reshape(3, 1, d)
        h = _ffn(h, sh1, sc1, g1, ng[0], wg, wu, wdn, (i, 0), s)
        j = i // 2
        if i % 2 == 0:
            w_in, bf = _hyb_weight(hyb_w_in[j], fox_b_f[j])
            fq, fk, fvt, dq, dk, dvt, iq, ik, cparts, iw = _hyb_proj(h, sh2, sc2, ng[1], w_in, bf, cos_d, sin_d, b, s)
            out_a = _fox_attention(fq, fk, cparts, fvt)
            out_b = _dsa_attention(dq, dk, dvt, iq, ik, iw)
            w_out = hyb_w_out[j].astype(BF16)
            half = out_a.shape[2]
            mix = (g2, [out_a.reshape(t, half), out_b.reshape(t, half)], [w_out[:half], w_out[half:]])
        else:
            wd, wuq, wukv = _mla_weights(mla_w_down[j], mla_w_uq[j], mla_w_ukv[j])
            qn, qr, kn, vt, kr = _mla_proj(h, sh2, sc2, ng[1], wd, mla_q_norm[j].reshape(1, -1),
                                           mla_kv_norm[j].reshape(1, -1), wuq, wukv, cos_m, sin_m, b, s)
            out = _mla_attention(qn, qr, kn, kr, vt)
            mix = (g2, [out.reshape(t, -1)], [mla_w_out[j].astype(BF16)])
        last = i == depth - 1
        h = _ffn(h, sh3, sc3, g3, ng[2], wg, wu, wdn, (i, 1), s,
                 final_g=final_g.reshape(1, d) if last else None, mix=mix)
    return h.reshape(b, s, d)
```
